```python
import math
import jax, jax.numpy as jnp
from jax import lax
import numpy as np

D_MODEL = 2048
BATCH = 4
SEQ = 4096
DEPTH = 2

GRID_W = 64
ROPE_THETA = 10000.0
N_MIXERS = 2
ATTN_HEADS = 16
ATTN_KV_HEADS = 4
ATTN_HEAD_DIM = D_MODEL // ATTN_HEADS
ATTN_GROUP = ATTN_HEADS // ATTN_KV_HEADS
Q_BLOCK = 128
QK_NORM_EPS = 1e-6
RET_HEADS = 8
RET_QK_DIM = D_MODEL // RET_HEADS
RET_V_DIM = 2 * RET_QK_DIM
RET_CHUNK = 128
RET_GN_EPS = 1e-5
N_EXPERTS = 32
TOP_K = 4
D_EXPERT = D_MODEL
SWIGLU_LIMIT = 7.0
SWIGLU_ALPHA = 1.702
MOE_BLOCK = 256
LN_EPS = 1e-5
DEEPNORM_ALPHA = (2 * DEPTH) ** 0.25
DEEPNORM_BETA = (8 * DEPTH) ** -0.25
N_ATTN_LAYERS = (DEPTH + 1) // 2
N_RET_LAYERS = DEPTH // 2

kernel_name = "hybrid_gqa_retention_moe_encoder"


def layer_norm(x, g, b):
    xf = x.astype(jnp.float32)
    mu = jnp.mean(xf, axis=-1, keepdims=True)
    var = jnp.mean(jnp.square(xf - mu), axis=-1, keepdims=True)
    return ((xf - mu) * lax.rsqrt(var + LN_EPS) * g.astype(jnp.float32) + b.astype(jnp.float32)).astype(x.dtype)


def rms_norm_heads(x, g):
    xf = x.astype(jnp.float32)
    return xf * lax.rsqrt(jnp.mean(jnp.square(xf), axis=-1, keepdims=True) + QK_NORM_EPS) * g.astype(jnp.float32)


def axial_rope_tables(seq, dim):
    rows = seq // GRID_W
    row_idx = jnp.repeat(jnp.arange(rows, dtype=jnp.float32), GRID_W)
    col_idx = jnp.tile(jnp.arange(GRID_W, dtype=jnp.float32), rows)
    quarter = dim // 4
    inv_freq = ROPE_THETA ** (-jnp.arange(quarter, dtype=jnp.float32) / quarter)
    ang_r = row_idx[:, None] * inv_freq[None, :]
    ang_c = col_idx[:, None] * inv_freq[None, :]
    ang = jnp.concatenate([ang_r, ang_r, ang_c, ang_c], axis=-1)
    return jnp.cos(ang), jnp.sin(ang)


def _rotate_half(x):
    h = x.shape[-1] // 2
    return jnp.concatenate([-x[..., h:], x[..., :h]], axis=-1)


def apply_axial_rope(x, cos, sin):
    half = x.shape[-1] // 2
    rot = jnp.concatenate([_rotate_half(x[..., :half]), _rotate_half(x[..., half:])], axis=-1)
    return x * cos[None, :, None, :] + rot * sin[None, :, None, :]


def attention_mixer(x, w_in, q_gain, k_gain, w_out, cos, sin):
    bsz, seq, _ = x.shape
    h = x @ w_in
    nq = ATTN_HEADS * ATTN_HEAD_DIM
    nkv = ATTN_KV_HEADS * ATTN_HEAD_DIM
    q, k, v = jnp.split(h, [nq, nq + nkv], axis=-1)
    q = q.reshape(bsz, seq, ATTN_HEADS, ATTN_HEAD_DIM)
    k = k.reshape(bsz, seq, ATTN_KV_HEADS, ATTN_HEAD_DIM)
    v = v.reshape(bsz, seq, ATTN_KV_HEADS, ATTN_HEAD_DIM)
    q = (apply_axial_rope(rms_norm_heads(q, q_gain), cos, sin) * (ATTN_HEAD_DIM ** -0.5)).astype(x.dtype)
    k = apply_axial_rope(rms_norm_heads(k, k_gain), cos, sin).astype(x.dtype)
    n_blocks = seq // Q_BLOCK
    qb = q.reshape(bsz, n_blocks, Q_BLOCK, ATTN_KV_HEADS, ATTN_GROUP, ATTN_HEAD_DIM).transpose(1, 0, 3, 4, 2, 5)
    kt = k.transpose(0, 2, 1, 3)
    vt = v.transpose(0, 2, 1, 3)

    def one_block(q_blk):
        s = jnp.einsum('bhgqd,bhkd->bhgqk', q_blk, kt).astype(jnp.float32)
        p = jax.nn.softmax(s, axis=-1).astype(vt.dtype)
        return jnp.einsum('bhgqk,bhkd->bhgqd', p, vt)

    o = lax.map(one_block, qb)
    o = o.transpose(1, 0, 4, 2, 3, 5).reshape(bsz, seq, nq)
    return o @ w_out


def retention_chunkwise(q, k, v, log_gamma):
    bsz, seq, nh, dk = q.shape
    dv = v.shape[-1]
    c = RET_CHUNK
    nc = seq // c
    to_chunks = lambda t: t.reshape(bsz, nc, c, nh, t.shape[-1]).transpose(1, 0, 3, 2, 4)
    qc, kc, vc = to_chunks(q), to_chunks(k), to_chunks(v)
    pos = jnp.arange(c, dtype=jnp.float32)
    dist = pos[:, None] - pos[None, :]
    lg = log_gamma.astype(jnp.float32)
    intra = jnp.where(dist[None] >= 0, jnp.exp(lg[:, None, None] * jnp.maximum(dist, 0.0)[None]), 0.0)
    q_decay = jnp.exp(lg[:, None] * (pos[None, :] + 1.0))
    k_decay = jnp.exp(lg[:, None] * (c - 1.0 - pos[None, :]))
    chunk_decay = jnp.exp(lg * c)

    def step(state, chunk):
        qi, ki, vi = chunk
        scores = jnp.einsum('bhqd,bhkd->bhqk', qi, ki) * intra[None]
        o = (jnp.einsum('bhqk,bhkv->bhqv', scores, vi)
             + jnp.einsum('bhqd,bhdv->bhqv', qi * q_decay[None, :, :, None], state))
        state = (state * chunk_decay[None, :, None, None]
                 + jnp.einsum('bhkd,bhkv->bhdv', ki * k_decay[None, :, :, None], vi))
        return state, o

    state0 = jnp.zeros((bsz, nh, dk, dv), jnp.float32)
    _, o = lax.scan(step, state0, (qc, kc, vc))
    return o.transpose(1, 0, 3, 2, 4).reshape(bsz, seq, nh, dv)


def retention_mixer(x, w_in, decay_fwd, decay_bwd, w_out, cos, sin):
    bsz, seq, _ = x.shape
    nqk = RET_HEADS * RET_QK_DIM
    nv = RET_HEADS * RET_V_DIM
    h = x @ w_in
    q, k, v, g = jnp.split(h, [nqk, 2 * nqk, 2 * nqk + nv], axis=-1)
    q = apply_axial_rope(q.astype(jnp.float32).reshape(bsz, seq, RET_HEADS, RET_QK_DIM), cos, sin)
    k = apply_axial_rope(k.astype(jnp.float32).reshape(bsz, seq, RET_HEADS, RET_QK_DIM), cos, sin) * (RET_QK_DIM ** -0.5)
    v = v.astype(jnp.float32).reshape(bsz, seq, RET_HEADS, RET_V_DIM)
    lg_f = jnp.log1p(-jnp.exp(decay_fwd.astype(jnp.float32)))
    lg_b = jnp.log1p(-jnp.exp(decay_bwd.astype(jnp.float32)))
    o_f = retention_chunkwise(q, k, v, lg_f)
    o_b = jnp.flip(retention_chunkwise(jnp.flip(q, 1), jnp.flip(k, 1), jnp.flip(v, 1), lg_b), 1)
    o = o_f + o_b
    mu = jnp.mean(o, axis=-1, keepdims=True)
    var = jnp.mean(jnp.square(o - mu), axis=-1, keepdims=True)
    o = ((o - mu) * lax.rsqrt(var + RET_GN_EPS)).reshape(bsz, seq, nv).astype(x.dtype)
    return (jax.nn.silu(g) * o) @ w_out


def moe_ffn(x2, router_w, router_b, w_gate_up, b_gate_up, w_down, b_down):
    t = x2.shape[0]
    logits = (x2 @ router_w).astype(jnp.float32) + router_b.astype(jnp.float32)
    top_vals, top_idx = lax.top_k(logits, TOP_K)
    gates = jax.nn.softmax(top_vals, axis=-1).astype(x2.dtype)
    n_assign = t * TOP_K
    n_blocks = -(-n_assign // MOE_BLOCK) + N_EXPERTS
    flat_e = top_idx.reshape(-1).astype(jnp.int32)
    flat_tok = jnp.repeat(jnp.arange(t, dtype=jnp.int32), TOP_K)
    flat_gate = gates.reshape(-1)
    order = jnp.argsort(flat_e)
    sorted_e = flat_e[order]
    counts = jnp.bincount(flat_e, length=N_EXPERTS).astype(jnp.int32)
    start = jnp.cumsum(counts) - counts
    padded = (counts + MOE_BLOCK - 1) // MOE_BLOCK * MOE_BLOCK
    pad_end = jnp.cumsum(padded)
    pad_start = pad_end - padded
    dest = pad_start[sorted_e] + (jnp.arange(n_assign, dtype=jnp.int32) - start[sorted_e])
    buf_tok = jnp.zeros((n_blocks * MOE_BLOCK,), jnp.int32).at[dest].set(flat_tok[order])
    buf_gate = jnp.zeros((n_blocks * MOE_BLOCK,), x2.dtype).at[dest].set(flat_gate[order])
    block_start = jnp.arange(n_blocks, dtype=jnp.int32) * MOE_BLOCK
    block_e = jnp.minimum(jnp.searchsorted(pad_end, block_start, side='right'), N_EXPERTS - 1).astype(jnp.int32)

    def expert_block(args):
        tok, e, gate = args
        xb = x2[tok]
        gu = xb @ w_gate_up[e] + b_gate_up[e]
        hg = jnp.minimum(gu[:, 0::2], SWIGLU_LIMIT)
        hu = jnp.clip(gu[:, 1::2], -SWIGLU_LIMIT, SWIGLU_LIMIT)
        act = (hu + 1.0) * (hg * jax.nn.sigmoid(SWIGLU_ALPHA * hg))
        return (act @ w_down[e] + b_down[e]) * gate[:, None]

    y = lax.map(expert_block, (buf_tok.reshape(n_blocks, MOE_BLOCK), block_e,
                               buf_gate.reshape(n_blocks, MOE_BLOCK)))
    y = y.reshape(n_blocks * MOE_BLOCK, -1)
    return jax.ops.segment_sum(y, buf_tok, num_segments=t).astype(x2.dtype)


def setup_inputs(seed: int = 0) -> dict:
    key = jax.random.key(seed)
    ks = jax.random.split(key, 20)
    f32 = jnp.float32
    d = D_MODEL
    nq = ATTN_HEADS * ATTN_HEAD_DIM
    nkv = ATTN_KV_HEADS * ATTN_HEAD_DIM
    nqk = RET_HEADS * RET_QK_DIM
    nv = RET_HEADS * RET_V_DIM
    x = jax.random.normal(ks[0], (BATCH, SEQ, d), f32)
    attn_w_in = jax.random.normal(ks[1], (N_ATTN_LAYERS, d, nq + 2 * nkv), f32) * d ** -0.5
    attn_w_in = attn_w_in.at[..., nq + nkv:].multiply(DEEPNORM_BETA)
    attn_q_gain = 1.0 + 0.02 * jax.random.normal(ks[2], (N_ATTN_LAYERS, ATTN_HEAD_DIM), f32)
    attn_k_gain = 1.0 + 0.02 * jax.random.normal(ks[3], (N_ATTN_LAYERS, ATTN_HEAD_DIM), f32)
    attn_w_out = jax.random.normal(ks[4], (N_ATTN_LAYERS, nq, d), f32) * (nq ** -0.5) * DEEPNORM_BETA
    ret_w_in = jax.random.normal(ks[5], (N_RET_LAYERS, d, 2 * nqk + 2 * nv), f32) * d ** -0.5
    ret_w_in = ret_w_in.at[..., 2 * nqk:2 * nqk + nv].multiply(DEEPNORM_BETA)
    base_decay = jnp.log(2.0 ** (-5.0 - jnp.arange(RET_HEADS, dtype=f32)))
    ret_decay_fwd = base_decay[None] + 0.05 * jax.random.normal(ks[6], (N_RET_LAYERS, RET_HEADS), f32)
    ret_decay_bwd = base_decay[None] + 0.05 * jax.random.normal(ks[7], (N_RET_LAYERS, RET_HEADS), f32)
    ret_w_out = jax.random.normal(ks[8], (N_RET_LAYERS, nv, d), f32) * (nv ** -0.5) * DEEPNORM_BETA
    ln_mix_g = 1.0 + 0.02 * jax.random.normal(ks[9], (DEPTH, d), f32)
    ln_mix_b = 0.02 * jax.random.normal(ks[10], (DEPTH, d), f32)
    router_w = jax.random.normal(ks[11], (DEPTH, d, N_EXPERTS), f32) * d ** -0.5
    router_b = 0.01 * jax.random.normal(ks[12], (DEPTH, N_EXPERTS), f32)
    expert_w_gate_up = jax.random.normal(ks[13], (DEPTH, N_EXPERTS, d, 2 * D_EXPERT), f32) * d ** -0.5
    expert_b_gate_up = 0.01 * jax.random.normal(ks[14], (DEPTH, N_EXPERTS, 2 * D_EXPERT), f32)
    expert_w_down = jax.random.normal(ks[15], (DEPTH, N_EXPERTS, D_EXPERT, d), f32) * (D_EXPERT ** -0.5) * DEEPNORM_BETA
    expert_b_down = 0.01 * jax.random.normal(ks[16], (DEPTH, N_EXPERTS, d), f32)
    ln_ffn_g = 1.0 + 0.02 * jax.random.normal(ks[17], (DEPTH, d), f32)
    ln_ffn_b = 0.02 * jax.random.normal(ks[18], (DEPTH, d), f32)
    return {"x": x, "attn_w_in": attn_w_in, "attn_q_gain": attn_q_gain, "attn_k_gain": attn_k_gain,
            "attn_w_out": attn_w_out, "ret_w_in": ret_w_in, "ret_decay_fwd": ret_decay_fwd,
            "ret_decay_bwd": ret_decay_bwd, "ret_w_out": ret_w_out, "ln_mix_g": ln_mix_g, "ln_mix_b": ln_mix_b,
            "router_w": router_w, "router_b": router_b, "expert_w_gate_up": expert_w_gate_up,
            "expert_b_gate_up": expert_b_gate_up, "expert_w_down": expert_w_down, "expert_b_down": expert_b_down,
            "ln_ffn_g": ln_ffn_g, "ln_ffn_b": ln_ffn_b}


def reference(x, attn_w_in, attn_q_gain, attn_k_gain, attn_w_out, ret_w_in, ret_decay_fwd, ret_decay_bwd,
              ret_w_out, ln_mix_g, ln_mix_b, router_w, router_b, expert_w_gate_up, expert_b_gate_up,
              expert_w_down, expert_b_down, ln_ffn_g, ln_ffn_b):
    bsz, seq, d = x.shape
    cos_a, sin_a = axial_rope_tables(seq, ATTN_HEAD_DIM)
    cos_r, sin_r = axial_rope_tables(seq, RET_QK_DIM)
    for i in range(DEPTH):
        j = i // N_MIXERS
        if i % N_MIXERS == 0:
            mix = attention_mixer(x, attn_w_in[j], attn_q_gain[j], attn_k_gain[j], attn_w_out[j], cos_a, sin_a)
        else:
            mix = retention_mixer(x, ret_w_in[j], ret_decay_fwd[j], ret_decay_bwd[j], ret_w_out[j], cos_r, sin_r)
        x = layer_norm(DEEPNORM_ALPHA * x + mix, ln_mix_g[i], ln_mix_b[i])
        ffn = moe_ffn(x.reshape(bsz * seq, d), router_w[i], router_b[i], expert_w_gate_up[i],
                      expert_b_gate_up[i], expert_w_down[i], expert_b_down[i]).reshape(bsz, seq, d)
        x = layer_norm(DEEPNORM_ALPHA * x + ffn, ln_ffn_g[i], ln_ffn_b[i])
    return x
```

```python
import functools

import jax
import jax.numpy as jnp
from jax import lax
from jax.experimental import pallas as pl
from jax.experimental.pallas import tpu as pltpu

F32, BF16, I32 = jnp.float32, jnp.bfloat16, jnp.int32

HEAD_DIM = 128
GRID_W = 64
ROPE_THETA = 10000.0
TOP_K = 4
QK_NORM_EPS = 1e-6
RET_CHUNK = 128
RET_GN_EPS = 1e-5
SWIGLU_LIMIT = 7.0
SWIGLU_ALPHA = 1.702
LN_EPS = 1e-5

V7X_VMEM_LIMIT_BYTES = 56 * 1024 * 1024
MOE_ROWS = 256

NT_DIMS = (((1,), (1,)), ((), ()))
TN_DIMS = (((0,), (0,)), ((), ()))


def _params(n_axes):
    return pltpu.CompilerParams(dimension_semantics=("arbitrary",) * n_axes,
                                vmem_limit_bytes=V7X_VMEM_LIMIT_BYTES)


def _tile(n, pref):
    t = min(n, pref)
    while n % t:
        t -= 128
    assert t > 0, (n, pref)
    return t


def _layer_norm(z, g, b):
    mu = jnp.mean(z, axis=-1, keepdims=True)
    zc = z - mu
    var = jnp.mean(zc * zc, axis=-1, keepdims=True)
    return zc * lax.rsqrt(var + LN_EPS) * g + b


def _rope_tables(seq, dim):
    rows = seq // GRID_W
    row_idx = jnp.repeat(jnp.arange(rows, dtype=F32), GRID_W)
    col_idx = jnp.tile(jnp.arange(GRID_W, dtype=F32), rows)
    quarter = dim // 4
    inv_freq = ROPE_THETA ** (-jnp.arange(quarter, dtype=F32) / quarter)
    ang_r = row_idx[:, None] * inv_freq[None, :]
    ang_c = col_idx[:, None] * inv_freq[None, :]
    ang = jnp.concatenate([ang_r, ang_r, ang_c, ang_c], axis=-1)
    cos, sin = jnp.cos(ang), jnp.sin(ang)
    first = (jnp.arange(dim) % (dim // 2)) < quarter
    return cos, jnp.where(first, -sin, 0.0), jnp.where(first, 0.0, sin)


def _rope(x, cos, sin_up, sin_dn):
    d = x.shape[-1]
    q = d // 4
    return x * cos + pltpu.roll(x, d - q, 1) * sin_up + pltpu.roll(x, q, 1) * sin_dn


def _matmul_kernel(x_ref, w_ref, o_ref):
    o_ref[...] = jnp.dot(x_ref[...].astype(BF16), w_ref[...],
                         preferred_element_type=F32).astype(o_ref.dtype)


def _matmul(x, w, *, tm=512, tn=1024):
    m, k = x.shape
    n = w.shape[1]
    tm, tn = _tile(m, tm), _tile(n, tn)
    return pl.pallas_call(
        _matmul_kernel,
        grid=(n // tn, m // tm),
        in_specs=[pl.BlockSpec((tm, k), lambda j, i: (i, 0)),
                  pl.BlockSpec((k, tn), lambda j, i: (0, j))],
        out_specs=pl.BlockSpec((tm, tn), lambda j, i: (i, j)),
        out_shape=jax.ShapeDtypeStruct((m, n), BF16),
        compiler_params=_params(2),
        name="dense_matmul",
    )(x, w)


def _attn_kernel(q_ref, k_ref, v_ref, cq_ref, uq_ref, dq_ref, ck_ref, uk_ref, dk_ref,
                 qg_ref, kg_ref, o_ref, k_scr, *, groups, scale):
    def norm_rope(x, gain, cos, up, dn):
        xn = x * lax.rsqrt(jnp.mean(x * x, axis=-1, keepdims=True) + QK_NORM_EPS) * gain
        return _rope(xn, cos, up, dn)

    @pl.when(pl.program_id(2) == 0)
    def _():
        k = k_ref[...].astype(F32)
        k_scr[...] = norm_rope(k, kg_ref[...], ck_ref[...], uk_ref[...], dk_ref[...]).astype(BF16)

    cos, up, dn = cq_ref[...], uq_ref[...], dq_ref[...]
    for g in range(groups):
        cols = slice(g * HEAD_DIM, (g + 1) * HEAD_DIM)
        q = q_ref[:, cols].astype(F32)
        qr = (norm_rope(q, qg_ref[...], cos, up, dn) * scale).astype(BF16)
        s = lax.dot_general(qr, k_scr[...], NT_DIMS, preferred_element_type=F32)
        p = jnp.exp(s - jnp.max(s, axis=-1, keepdims=True))
        l = jnp.sum(p, axis=-1, keepdims=True)
        o = jnp.dot(p.astype(BF16), v_ref[...], preferred_element_type=F32)
        o_ref[:, cols] = (o / l).astype(o_ref.dtype)


def _attention(h, q_gain, k_gain, bsz, seq, n_heads, n_kv, *, tq=256):
    t = h.shape[0]
    groups = n_heads // n_kv
    tq = _tile(seq, tq)
    nq = seq // tq
    cos, up, dn = _rope_tables(seq, HEAD_DIM)
    gw = groups * HEAD_DIM
    q_spec = pl.BlockSpec((tq, gw), lambda b, kv, i: (b * nq + i, kv))
    k_spec = pl.BlockSpec((seq, HEAD_DIM), lambda b, kv, i: (b, n_heads + kv))
    v_spec = pl.BlockSpec((seq, HEAD_DIM), lambda b, kv, i: (b, n_heads + n_kv + kv))
    tq_spec = pl.BlockSpec((tq, HEAD_DIM), lambda b, kv, i: (i, 0))
    tk_spec = pl.BlockSpec((seq, HEAD_DIM), lambda b, kv, i: (0, 0))
    gain_spec = pl.BlockSpec((1, HEAD_DIM), lambda b, kv, i: (0, 0))
    return pl.pallas_call(
        functools.partial(_attn_kernel, groups=groups, scale=HEAD_DIM ** -0.5),
        grid=(bsz, n_kv, nq),
        in_specs=[q_spec, k_spec, v_spec, tq_spec, tq_spec, tq_spec, tk_spec, tk_spec, tk_spec,
                  gain_spec, gain_spec],
        out_specs=pl.BlockSpec((tq, gw), lambda b, kv, i: (b * nq + i, kv)),
        out_shape=jax.ShapeDtypeStruct((t, n_heads * HEAD_DIM), BF16),
        scratch_shapes=[pltpu.VMEM((seq, HEAD_DIM), BF16)],
        compiler_params=_params(3),
        name="gqa_attention",
    )(h, h, h, cos, up, dn, cos, up, dn, q_gain.reshape(1, HEAD_DIM), k_gain.reshape(1, HEAD_DIM))


def _ret_kernel(lg_ref, q_ref, k_ref, v_ref, g_ref, cos_ref, up_ref, dn_ref, o_ref,
                state, o_acc, *, n_sub, k_scale):
    d = pl.program_id(2)
    sc = pl.program_id(3)
    n_sc = pl.num_programs(3)
    c = RET_CHUNK
    span = n_sub * c

    @pl.when(sc == 0)
    def _():
        state[...] = jnp.zeros_like(state)

    lg = lg_ref[0, 0][:1, :1]
    sgn = 1 - 2 * d
    dist = (lax.broadcasted_iota(I32, (c, c), 0) - lax.broadcasted_iota(I32, (c, c), 1)) * sgn
    intra = jnp.where(dist >= 0, jnp.exp(lg * jnp.maximum(dist, 0).astype(F32)), 0.0)
    pos = lax.broadcasted_iota(I32, (c, 1), 0)
    q_dec = jnp.exp(lg * jnp.where(d == 0, pos + 1, c - pos).astype(F32))
    k_dec = jnp.exp(lg * jnp.where(d == 0, c - 1 - pos, pos).astype(F32))
    chunk_dec = jnp.exp(lg * float(c))
    sci = sc + d * (n_sc - 1 - 2 * sc)

    for j in range(n_sub):
        cj = j + d * (n_sub - 1 - 2 * j)
        r0 = pl.multiple_of(cj * c, c)
        rows = pl.ds(r0, c)
        cos, up, dn = cos_ref[rows, :], up_ref[rows, :], dn_ref[rows, :]
        qr = _rope(q_ref[rows, :].astype(F32), cos, up, dn)
        kr = _rope(k_ref[rows, :].astype(F32), cos, up, dn) * k_scale
        v = v_ref[rows, :]
        s = lax.dot_general(qr.astype(BF16), kr.astype(BF16), NT_DIMS,
                            preferred_element_type=F32) * intra
        st = state[...]
        o = (jnp.dot(s.astype(BF16), v, preferred_element_type=F32)
             + jnp.dot((qr * q_dec).astype(BF16), st.astype(BF16), preferred_element_type=F32))
        state[...] = st * chunk_dec + lax.dot_general((kr * k_dec).astype(BF16), v, TN_DIMS,
                                                      preferred_element_type=F32)
        acc_rows = pl.ds(pl.multiple_of(sci * span + r0, c), c)

        @pl.when(d == 0)
        def _():
            o_acc[acc_rows, :] = o

        @pl.when(d == 1)
        def _():
            ot = o_acc[acc_rows, :] + o
            mu = jnp.mean(ot, axis=-1, keepdims=True)
            oc = ot - mu
            var = jnp.mean(oc * oc, axis=-1, keepdims=True)
            on = oc * lax.rsqrt(var + RET_GN_EPS)
            gate = g_ref[rows, :].astype(F32)
            o_ref[rows, :] = (gate / (1.0 + jnp.exp(-gate)) * on).astype(o_ref.dtype)


def _retention(h, lg_fwd, lg_bwd, bsz, seq, n_heads, dk, dv, *, span=512):
    t = h.shape[0]
    span = _tile(seq, span)
    n_sub = span // RET_CHUNK
    n_sc = seq // span
    cos, up, dn = _rope_tables(seq, dk)
    lg = jnp.broadcast_to(jnp.stack([lg_fwd, lg_bwd])[:, :, None, None], (2, n_heads, 8, 128)).astype(F32)
    k_off = n_heads
    v_off = 2 * n_heads * dk // dv
    g_off = v_off + n_heads

    def blk(b, d, s):
        return b * n_sc + s + d * (n_sc - 1 - 2 * s)

    def held(b, d, s):
        return b * n_sc + jnp.where(d == 0, n_sc - 1, n_sc - 1 - s)

    tab_spec = pl.BlockSpec((span, dk), lambda b, hh, d, s: (s + d * (n_sc - 1 - 2 * s), 0))
    return pl.pallas_call(
        functools.partial(_ret_kernel, n_sub=n_sub, k_scale=dk ** -0.5),
        grid=(bsz, n_heads, 2, n_sc),
        in_specs=[pl.BlockSpec((1, 1, 8, 128), lambda b, hh, d, s: (d, hh, 0, 0)),
                  pl.BlockSpec((span, dk), lambda b, hh, d, s: (blk(b, d, s), hh)),
                  pl.BlockSpec((span, dk), lambda b, hh, d, s: (blk(b, d, s), k_off + hh)),
                  pl.BlockSpec((span, dv), lambda b, hh, d, s: (blk(b, d, s), v_off + hh)),
                  pl.BlockSpec((span, dv), lambda b, hh, d, s: (held(b, d, s), g_off + hh)),
                  tab_spec, tab_spec, tab_spec],
        out_specs=pl.BlockSpec((span, dv), lambda b, hh, d, s: (held(b, d, s), hh)),
        out_shape=jax.ShapeDtypeStruct((t, n_heads * dv), BF16),
        scratch_shapes=[pltpu.VMEM((dk, dv), F32), pltpu.VMEM((seq, dv), F32)],
        compiler_params=_params(4),
        name="retention",
    )(lg, h, h, h, h, cos, up, dn)


def _pack_bf16_pairs(y):
    half = y.shape[1] // 2
    bits = lax.bitcast_convert_type(y.astype(BF16).astype(F32), I32)
    return (bits[:, half:] & jnp.int32(-65536)) | lax.shift_right_logical(bits[:, :half], 16)


def _unpack_bf16_pairs(w):
    lo = lax.bitcast_convert_type(lax.shift_left(w, 16), F32).astype(BF16)
    hi = lax.bitcast_convert_type(w & jnp.int32(-65536), F32).astype(BF16)
    return lo, hi


def _proj_route_kernel(a_ref, w_ref, res_ref, g_ref, b_ref, rwh_ref, rwl_ref, rb_ref,
                       x_ref, xp_ref, idx_ref, gate_ref, rank_ref, cnt_ref, carry, *, alpha):
    @pl.when(pl.program_id(0) == 0)
    def _():
        carry[...] = jnp.zeros_like(carry)

    acc = jnp.dot(a_ref[...], w_ref[...], preferred_element_type=F32)
    y = _layer_norm(alpha * res_ref[...] + acc, g_ref[...], b_ref[...])
    x_ref[...] = y
    xp_ref[...] = _pack_bf16_pairs(y)

    yh = y.astype(BF16)
    yl = (y - yh.astype(F32)).astype(BF16)
    logits = (lax.dot_general(rwh_ref[...], yh, NT_DIMS, preferred_element_type=F32)
              + lax.dot_general(rwl_ref[...], yh, NT_DIMS, preferred_element_type=F32)
              + lax.dot_general(rwh_ref[...], yl, NT_DIMS, preferred_element_type=F32)
              + rb_ref[...])
    n_exp, tm = logits.shape
    eidx = lax.broadcasted_iota(I32, logits.shape, 0).astype(F32)
    vals, idxs = [], []
    rest = logits
    for _ in range(TOP_K):
        m = jnp.max(rest, axis=0, keepdims=True)
        am = jnp.min(jnp.where(rest == m, eidx, float(n_exp)), axis=0, keepdims=True)
        vals.append(m)
        idxs.append(am)
        rest = jnp.where(eidx == am, -jnp.inf, rest)
    exps = [jnp.exp(v - vals[0]) for v in vals]
    den = exps[0] + exps[1] + exps[2] + exps[3]
    hots = [eidx == am for am in idxs]
    multi = sum(h.astype(F32) for h in hots)

    before = (lax.broadcasted_iota(I32, (tm, tm), 0) < lax.broadcasted_iota(I32, (tm, tm), 1))
    prefix = jnp.dot(multi.astype(BF16), before.astype(BF16), preferred_element_type=F32) + carry[:, :1]
    for k in range(TOP_K):
        idx_ref[k:k + 1, :] = idxs[k].astype(I32)
        gate_ref[k:k + 1, :] = exps[k] / den
        rank_ref[k:k + 1, :] = jnp.sum(jnp.where(hots[k], prefix, 0.0), axis=0, keepdims=True).astype(I32)
    carry[...] = carry[...] + jnp.sum(multi, axis=1, keepdims=True)
    cnt_ref[...] = carry[...].astype(I32)


def _proj_route(a, w, resid, ln_g, ln_b, router_w, router_b, alpha, *, tm=256):
    t, k = a.shape
    d = w.shape[1]
    n_exp = router_w.shape[1]
    tm = _tile(t, tm)
    rwt = router_w.T
    rwh = rwt.astype(BF16)
    rwl = (rwt - rwh.astype(F32)).astype(BF16)
    row = lambda i: (i, 0)
    fixed = lambda i: (0, 0)
    col = lambda i: (0, i)
    outs = pl.pallas_call(
        functools.partial(_proj_route_kernel, alpha=alpha),
        grid=(t // tm,),
        in_specs=[pl.BlockSpec((tm, k), row),
                  pl.BlockSpec((k, d), fixed, pipeline_mode=pl.Buffered(1)),
                  pl.BlockSpec((tm, d), row),
                  pl.BlockSpec((1, d), fixed), pl.BlockSpec((1, d), fixed),
                  pl.BlockSpec((n_exp, d), fixed), pl.BlockSpec((n_exp, d), fixed),
                  pl.BlockSpec((n_exp, 1), fixed)],
        out_specs=[pl.BlockSpec((tm, d), row), pl.BlockSpec((tm, d // 2), row),
                   pl.BlockSpec((TOP_K, tm), col), pl.BlockSpec((TOP_K, tm), col),
                   pl.BlockSpec((TOP_K, tm), col), pl.BlockSpec((n_exp, 128), fixed)],
        out_shape=[jax.ShapeDtypeStruct((t, d), F32), jax.ShapeDtypeStruct((t, d // 2), I32),
                   jax.ShapeDtypeStruct((TOP_K, t), I32), jax.ShapeDtypeStruct((TOP_K, t), F32),
                   jax.ShapeDtypeStruct((TOP_K, t), I32), jax.ShapeDtypeStruct((n_exp, 128), I32)],
        scratch_shapes=[pltpu.VMEM((n_exp, 128), F32)],
        compiler_params=_params(1),
        name="proj_ln_route",
    )(a, w, resid, ln_g.reshape(1, d), ln_b.reshape(1, d), rwh, rwl, router_b.reshape(n_exp, 1))
    return outs


def _dispatch_kernel(dest_ref, cnt_ref, start_ref, nu_ref, xp_ref, xs_hbm, zblk, sem, *, n_tok_steps, n_exp):
    i = pl.program_id(0)
    tm = xp_ref.shape[0]
    n_blocks = xs_hbm.shape[0] // MOE_ROWS

    def row_copy(src, dst_row):
        return pltpu.make_async_copy(src, xs_hbm.at[pl.ds(dst_row, 1)], sem)

    def block_copy(b):
        return pltpu.make_async_copy(zblk, xs_hbm.at[pl.ds(pl.multiple_of(b * MOE_ROWS, MOE_ROWS), MOE_ROWS)], sem)

    @pl.when(i < n_tok_steps)
    def _():
        def issue(t, carry):
            for k in range(TOP_K):
                row_copy(xp_ref.at[pl.ds(t, 1)], dest_ref[k, t]).start()
            return carry

        def drain(t, carry):
            for k in range(TOP_K):
                row_copy(xp_ref.at[pl.ds(0, 1)], 0).wait()
            return carry

        lax.fori_loop(0, tm, issue, 0)
        lax.fori_loop(0, tm, drain, 0)

    @pl.when(i == n_tok_steps)
    def _():
        zblk[...] = jnp.zeros_like(zblk)
        zrow = zblk.at[pl.ds(0, 1)]

        def per_expert(e, carry):
            cnt = cnt_ref[e]
            n_pad = lax.rem(MOE_ROWS - lax.rem(cnt, MOE_ROWS), MOE_ROWS)
            first = start_ref[e] + cnt

            def issue(r, c2):
                row_copy(zrow, first + r).start()
                return c2

            def drain(r, c2):
                row_copy(zrow, 0).wait()
                return c2

            lax.fori_loop(0, n_pad, issue, 0)
            lax.fori_loop(0, n_pad, drain, 0)
            return carry

        lax.fori_loop(0, n_exp, per_expert, 0)

        def tail_issue(b, carry):
            block_copy(b).start()
            return carry

        def tail_drain(b, carry):
            block_copy(0).wait()
            return carry

        lax.fori_loop(nu_ref[0], n_blocks, tail_issue, 0)
        lax.fori_loop(nu_ref[0], n_blocks, tail_drain, 0)


def _dispatch(dest, counts, starts, n_used, xp, n_rows, *, tm=256):
    t, w = xp.shape
    tm = _tile(t, tm)
    n_steps = t // tm
    n_exp = counts.shape[0]
    smem = functools.partial(pl.BlockSpec, memory_space=pltpu.SMEM)
    return pl.pallas_call(
        functools.partial(_dispatch_kernel, n_tok_steps=n_steps, n_exp=n_exp),
        grid=(n_steps + 1,),
        in_specs=[smem((TOP_K, tm), lambda i: (0, jnp.minimum(i, n_steps - 1))),
                  smem(), smem(), smem(),
                  pl.BlockSpec((tm, w), lambda i: (jnp.minimum(i, n_steps - 1), 0))],
        out_specs=pl.BlockSpec(memory_space=pl.ANY),
        out_shape=jax.ShapeDtypeStruct((n_rows, w), I32),
        scratch_shapes=[pltpu.VMEM((MOE_ROWS, w), I32), pltpu.SemaphoreType.DMA(())],
        compiler_params=_params(1),
        name="moe_dispatch",
    )(dest, counts, starts, n_used, xp)


def _gate_up_kernel(be_ref, nu_ref, xs_ref, wg_ref, wu_ref, bg_ref, bu_ref, h_ref):
    @pl.when(pl.program_id(1) < nu_ref[0])
    def _():
        lo, hi = _unpack_bf16_pairs(xs_ref[...])
        half = lo.shape[1]

        def proj(w_ref, b_ref):
            return (jnp.dot(lo, w_ref[0, :half, :], preferred_element_type=F32)
                    + jnp.dot(hi, w_ref[0, half:, :], preferred_element_type=F32) + b_ref[0])

        hg = jnp.minimum(proj(wg_ref, bg_ref), SWIGLU_LIMIT)
        hu = jnp.clip(proj(wu_ref, bu_ref), -SWIGLU_LIMIT, SWIGLU_LIMIT)
        act = (hu + 1.0) * (hg / (1.0 + jnp.exp(-SWIGLU_ALPHA * hg)))
        h_ref[...] = act.astype(h_ref.dtype)

    @pl.when(pl.program_id(1) >= nu_ref[0])
    def _():
        h_ref[...] = jnp.zeros_like(h_ref)


def _gate_up(xs, wg, wu, bg, bu, block_e, n_used, *, tn=1024):
    n_rows, half = xs.shape
    n_exp, d, f = wg.shape
    tn = _tile(f, tn)
    n_blocks = n_rows // MOE_ROWS
    row = lambda j, i, be, nu: (jnp.minimum(i, nu[0] - 1), 0)
    wsel = lambda j, i, be, nu: (be[i], 0, j)
    return pl.pallas_call(
        _gate_up_kernel,
        grid_spec=pltpu.PrefetchScalarGridSpec(
            num_scalar_prefetch=2,
            grid=(f // tn, n_blocks),
            in_specs=[pl.BlockSpec((MOE_ROWS, half), row),
                      pl.BlockSpec((1, d, tn), wsel), pl.BlockSpec((1, d, tn), wsel),
                      pl.BlockSpec((1, 1, tn), wsel), pl.BlockSpec((1, 1, tn), wsel)],
            out_specs=pl.BlockSpec((MOE_ROWS, tn), lambda j, i, be, nu: (i, j)),
        ),
        out_shape=jax.ShapeDtypeStruct((n_rows, f), BF16),
        compiler_params=_params(2),
        name="moe_gate_up",
    )(block_e, n_used, xs, wg, wu, bg, bu)


def _down_kernel(be_ref, nu_ref, h_ref, w_ref, b_ref, y_ref):
    @pl.when(pl.program_id(1) < nu_ref[0])
    def _():
        y_ref[...] = jnp.dot(h_ref[...], w_ref[0], preferred_element_type=F32) + b_ref[0]

    @pl.when(pl.program_id(1) >= nu_ref[0])
    def _():
        y_ref[...] = jnp.zeros_like(y_ref)


def _down(h, wd, bd, block_e, n_used, *, tn=1024):
    n_rows, f = h.shape
    d = wd.shape[2]
    tn = _tile(d, tn)
    n_blocks = n_rows // MOE_ROWS
    wsel = lambda j, i, be, nu: (be[i], 0, j)
    return pl.pallas_call(
        _down_kernel,
        grid_spec=pltpu.PrefetchScalarGridSpec(
            num_scalar_prefetch=2,
            grid=(d // tn, n_blocks),
            in_specs=[pl.BlockSpec((MOE_ROWS, f), lambda j, i, be, nu: (jnp.minimum(i, nu[0] - 1), 0)),
                      pl.BlockSpec((1, f, tn), wsel), pl.BlockSpec((1, 1, tn), wsel)],
            out_specs=pl.BlockSpec((MOE_ROWS, tn), lambda j, i, be, nu: (i, j)),
        ),
        out_shape=jax.ShapeDtypeStruct((n_rows, d), F32),
        compiler_params=_params(2),
        name="moe_down",
    )(block_e, n_used, h, wd, bd)


def _combine_kernel(dest_ref, gate_ref, x_ref, g_ref, b_ref, y_hbm, o_ref, buf, sem, *, alpha):
    tm = x_ref.shape[0]

    def row_copy(src_row, k, t):
        return pltpu.make_async_copy(y_hbm.at[pl.ds(src_row, 1)], buf.at[k, pl.ds(t, 1)], sem)

    def issue(t, carry):
        for k in range(TOP_K):
            row_copy(dest_ref[k, t], k, t).start()
        return carry

    def drain(t, carry):
        for k in range(TOP_K):
            row_copy(0, 0, 0).wait()
        return carry

    lax.fori_loop(0, tm, issue, 0)
    lax.fori_loop(0, tm, drain, 0)
    gates = gate_ref[...]
    ffn = gates[:, 0:1] * buf[0]
    for k in range(1, TOP_K):
        ffn = ffn + gates[:, k:k + 1] * buf[k]
    o_ref[...] = _layer_norm(alpha * x_ref[...] + ffn, g_ref[...], b_ref[...])


def _combine(dest, gates, x, y, ln_g, ln_b, alpha, *, tm=128):
    t, d = x.shape
    tm = _tile(t, tm)
    row = lambda i: (i, 0)
    fixed = lambda i: (0, 0)
    return pl.pallas_call(
        functools.partial(_combine_kernel, alpha=alpha),
        grid=(t // tm,),
        in_specs=[pl.BlockSpec((TOP_K, tm), lambda i: (0, i), memory_space=pltpu.SMEM),
                  pl.BlockSpec((tm, TOP_K), row), pl.BlockSpec((tm, d), row),
                  pl.BlockSpec((1, d), fixed), pl.BlockSpec((1, d), fixed),
                  pl.BlockSpec(memory_space=pl.ANY)],
        out_specs=pl.BlockSpec((tm, d), row),
        out_shape=jax.ShapeDtypeStruct((t, d), F32),
        scratch_shapes=[pltpu.VMEM((TOP_K, tm, d), F32), pltpu.SemaphoreType.DMA(())],
        compiler_params=_params(1),
        name="moe_combine_ln",
    )(dest, gates, x, ln_g.reshape(1, d), ln_b.reshape(1, d), y)


def _moe(x, xp, idx_t, gate_t, rank_t, counts2d, w_gate_up, b_gate_up, w_down, b_down, ln_g, ln_b, alpha):
    t, d = x.shape
    n_exp, _, f = w_down.shape[0], w_down.shape[1], w_gate_up.shape[2] // 2
    n_blocks = t * TOP_K // MOE_ROWS + n_exp
    n_rows = n_blocks * MOE_ROWS

    counts = counts2d[:, 0]
    padded = (counts + MOE_ROWS - 1) // MOE_ROWS * MOE_ROWS
    pad_end = jnp.cumsum(padded)
    starts = (pad_end - padded).astype(I32)
    n_used = (pad_end[-1] // MOE_ROWS).astype(I32)
    blk_start = jnp.arange(n_blocks, dtype=I32) * MOE_ROWS
    block_e = jnp.minimum(jnp.searchsorted(pad_end, blk_start, side="right"), n_exp - 1).astype(I32)
    block_e = block_e[jnp.minimum(jnp.arange(n_blocks), n_used - 1)]
    dest = starts[idx_t] + rank_t

    nu = n_used.reshape(1)
    xs = _dispatch(dest, counts, starts, nu, xp, n_rows)
    wg = w_gate_up[..., 0::2].astype(BF16)
    wu = w_gate_up[..., 1::2].astype(BF16)
    bg = b_gate_up[:, None, 0::2]
    bu = b_gate_up[:, None, 1::2]
    h = _gate_up(xs, wg, wu, bg, bu, block_e, nu)
    y = _down(h, w_down.astype(BF16), b_down[:, None, :], block_e, nu)
    return _combine(dest, gate_t.T, x, y, ln_g, ln_b, alpha)


def kernel(x, attn_w_in, attn_q_gain, attn_k_gain, attn_w_out, ret_w_in, ret_decay_fwd, ret_decay_bwd,
           ret_w_out, ln_mix_g, ln_mix_b, router_w, router_b, expert_w_gate_up, expert_b_gate_up,
           expert_w_down, expert_b_down, ln_ffn_g, ln_ffn_b):
    bsz, seq, d = x.shape
    depth = ln_mix_g.shape[0]
    alpha = (2 * depth) ** 0.25
    nq = attn_w_out.shape[1]
    n_heads = nq // HEAD_DIM
    n_kv = (attn_w_in.shape[2] - nq) // (2 * HEAD_DIM)
    ret_heads = ret_decay_fwd.shape[1]
    dv = ret_w_out.shape[1] // ret_heads
    dk = (ret_w_in.shape[2] - 2 * ret_heads * dv) // (2 * ret_heads)

    xf = x.reshape(bsz * seq, d)
    for i in range(depth):
        j = i // 2
        if i % 2 == 0:
            h = _matmul(xf, attn_w_in[j].astype(BF16))
            mix = _attention(h, attn_q_gain[j], attn_k_gain[j], bsz, seq, n_heads, n_kv)
            w_out = attn_w_out[j]
        else:
            h = _matmul(xf, ret_w_in[j].astype(BF16))
            lg_f = jnp.log1p(-jnp.exp(ret_decay_fwd[j].astype(F32)))
            lg_b = jnp.log1p(-jnp.exp(ret_decay_bwd[j].astype(F32)))
            mix = _retention(h, lg_f, lg_b, bsz, seq, ret_heads, dk, dv)
            w_out = ret_w_out[j]
        x1, xp, idx_t, gate_t, rank_t, counts = _proj_route(
            mix, w_out.astype(BF16), xf, ln_mix_g[i], ln_mix_b[i], router_w[i], router_b[i], alpha)
        xf = _moe(x1, xp, idx_t, gate_t, rank_t, counts, expert_w_gate_up[i], expert_b_gate_up[i],
                  expert_w_down[i], expert_b_down[i], ln_ffn_g[i], ln_ffn_b[i], alpha)
    return xf.reshape(bsz, seq, d)
```

```python
import functools

import jax
import jax.numpy as jnp
from jax import lax
from jax.experimental import pallas as pl
from jax.experimental.pallas import tpu as pltpu

F32, BF16, I32 = jnp.float32, jnp.bfloat16, jnp.int32

HEAD_DIM = 128
GRID_W = 64
ROPE_THETA = 10000.0
TOP_K = 4
QK_NORM_EPS = 1e-6
RET_CHUNK = 128
RET_GN_EPS = 1e-5
SWIGLU_LIMIT = 7.0
SWIGLU_ALPHA = 1.702
LN_EPS = 1e-5
LOG2_E = 1.4426950408889634

V7X_VMEM_LIMIT_BYTES = 56 * 1024 * 1024
MOE_ROWS = 256
MOE_GATE_UP_TILE = 1024

NT_DIMS = (((1,), (1,)), ((), ()))
TN_DIMS = (((0,), (0,)), ((), ()))


def _params(n_axes):
    return pltpu.CompilerParams(dimension_semantics=("arbitrary",) * n_axes,
                                vmem_limit_bytes=V7X_VMEM_LIMIT_BYTES)


def _tile(n, pref):
    t = min(n, pref)
    while n % t:
        t -= 128
    assert t > 0, (n, pref)
    return t


def _layer_norm(z, g, b):
    mu = jnp.mean(z, axis=-1, keepdims=True)
    zc = z - mu
    var = jnp.mean(zc * zc, axis=-1, keepdims=True)
    return zc * lax.rsqrt(var + LN_EPS) * g + b


def _rope_tables(seq, dim):
    rows = seq // GRID_W
    row_idx = jnp.repeat(jnp.arange(rows, dtype=F32), GRID_W)
    col_idx = jnp.tile(jnp.arange(GRID_W, dtype=F32), rows)
    quarter = dim // 4
    inv_freq = ROPE_THETA ** (-jnp.arange(quarter, dtype=F32) / quarter)
    ang_r = row_idx[:, None] * inv_freq[None, :]
    ang_c = col_idx[:, None] * inv_freq[None, :]
    ang = jnp.concatenate([ang_r, ang_r, ang_c, ang_c], axis=-1)
    cos, sin = jnp.cos(ang), jnp.sin(ang)
    first = (jnp.arange(dim) % (dim // 2)) < quarter
    return cos, jnp.where(first, -sin, 0.0), jnp.where(first, 0.0, sin)


def _rope(x, cos, sin_up, sin_dn):
    d = x.shape[-1]
    q = d // 4
    return x * cos + pltpu.roll(x, d - q, 1) * sin_up + pltpu.roll(x, q, 1) * sin_dn


def _matmul_kernel(x_ref, w_ref, o_ref):
    o_ref[...] = jnp.dot(x_ref[...].astype(BF16), w_ref[...],
                         preferred_element_type=F32).astype(o_ref.dtype)


def _matmul(x, w, *, tm=512, tn=1024):
    m, k = x.shape
    n = w.shape[1]
    tm, tn = _tile(m, tm), _tile(n, tn)
    return pl.pallas_call(
        _matmul_kernel,
        grid=(n // tn, m // tm),
        in_specs=[pl.BlockSpec((tm, k), lambda j, i: (i, 0)),
                  pl.BlockSpec((k, tn), lambda j, i: (0, j))],
        out_specs=pl.BlockSpec((tm, tn), lambda j, i: (i, j)),
        out_shape=jax.ShapeDtypeStruct((m, n), BF16),
        compiler_params=_params(2),
        name="dense_matmul",
    )(x, w)


def _attn_kernel(q_ref, k_ref, v_ref, cq_ref, uq_ref, dq_ref, ck_ref, uk_ref, dk_ref,
                 qg_ref, kg_ref, o_ref, k_scr, v_scr, *, groups, scale):
    def norm_rope(x, gain, cos, up, dn):
        xn = x * lax.rsqrt(jnp.mean(x * x, axis=-1, keepdims=True) + QK_NORM_EPS) * gain
        return _rope(xn, cos, up, dn)

    @pl.when(pl.program_id(2) == 0)
    def _():
        k = k_ref[...].astype(F32)
        k_scr[...] = norm_rope(k, kg_ref[...], ck_ref[...], uk_ref[...], dk_ref[...]).astype(BF16)
        v_scr[:, :HEAD_DIM] = v_ref[...]
        v_scr[:, HEAD_DIM:] = jnp.ones((v_scr.shape[0], HEAD_DIM), BF16)

    cos, up, dn = cq_ref[...], uq_ref[...], dq_ref[...]
    for g in range(groups):
        cols = slice(g * HEAD_DIM, (g + 1) * HEAD_DIM)
        q = q_ref[:, cols].astype(F32)
        qr = (norm_rope(q, qg_ref[...], cos, up, dn) * (scale * LOG2_E)).astype(BF16)
        s = lax.dot_general(qr, k_scr[...], NT_DIMS, preferred_element_type=F32)
        p = jnp.exp2(s - jnp.max(s, axis=-1, keepdims=True))
        o = jnp.dot(p.astype(BF16), v_scr[...], preferred_element_type=F32)
        o_ref[:, cols] = (o[:, :HEAD_DIM] / o[:, HEAD_DIM:HEAD_DIM + 1]).astype(o_ref.dtype)


def _attention(h, q_gain, k_gain, bsz, seq, n_heads, n_kv, *, tq=256):
    t = h.shape[0]
    groups = n_heads // n_kv
    tq = _tile(seq, tq)
    nq = seq // tq
    cos, up, dn = _rope_tables(seq, HEAD_DIM)
    gw = groups * HEAD_DIM
    q_spec = pl.BlockSpec((tq, gw), lambda b, kv, i: (b * nq + i, kv))
    k_spec = pl.BlockSpec((seq, HEAD_DIM), lambda b, kv, i: (b, n_heads + kv))
    v_spec = pl.BlockSpec((seq, HEAD_DIM), lambda b, kv, i: (b, n_heads + n_kv + kv))
    tq_spec = pl.BlockSpec((tq, HEAD_DIM), lambda b, kv, i: (i, 0))
    tk_spec = pl.BlockSpec((seq, HEAD_DIM), lambda b, kv, i: (0, 0))
    gain_spec = pl.BlockSpec((1, HEAD_DIM), lambda b, kv, i: (0, 0))
    return pl.pallas_call(
        functools.partial(_attn_kernel, groups=groups, scale=HEAD_DIM ** -0.5),
        grid=(bsz, n_kv, nq),
        in_specs=[q_spec, k_spec, v_spec, tq_spec, tq_spec, tq_spec, tk_spec, tk_spec, tk_spec,
                  gain_spec, gain_spec],
        out_specs=pl.BlockSpec((tq, gw), lambda b, kv, i: (b * nq + i, kv)),
        out_shape=jax.ShapeDtypeStruct((t, n_heads * HEAD_DIM), BF16),
        scratch_shapes=[pltpu.VMEM((seq, HEAD_DIM), BF16), pltpu.VMEM((seq, 2 * HEAD_DIM), BF16)],
        compiler_params=_params(3),
        name="gqa_attention",
    )(h, h, h, cos, up, dn, cos, up, dn, q_gain.reshape(1, HEAD_DIM), k_gain.reshape(1, HEAD_DIM))


def _ret_kernel(lg_ref, q_ref, k_ref, v_ref, g_ref, cos_ref, up_ref, dn_ref, o_ref,
                state, o_acc, *, n_sub, k_scale):
    d = pl.program_id(2)
    sc = pl.program_id(3)
    n_sc = pl.num_programs(3)
    c = RET_CHUNK
    span = n_sub * c

    @pl.when(sc == 0)
    def _():
        state[...] = jnp.zeros_like(state)

    lg = lg_ref[0, 0][:1, :1]
    sgn = 1 - 2 * d
    dist = (lax.broadcasted_iota(I32, (c, c), 0) - lax.broadcasted_iota(I32, (c, c), 1)) * sgn
    intra = jnp.where(dist >= 0, jnp.exp(lg * jnp.maximum(dist, 0).astype(F32)), 0.0)
    pos = lax.broadcasted_iota(I32, (c, 1), 0)
    q_dec = jnp.exp(lg * jnp.where(d == 0, pos + 1, c - pos).astype(F32))
    k_dec = jnp.exp(lg * jnp.where(d == 0, c - 1 - pos, pos).astype(F32))
    chunk_dec = jnp.exp(lg * float(c))
    sci = sc + d * (n_sc - 1 - 2 * sc)

    for j in range(n_sub):
        cj = j + d * (n_sub - 1 - 2 * j)
        r0 = pl.multiple_of(cj * c, c)
        rows = pl.ds(r0, c)
        cos, up, dn = cos_ref[rows, :], up_ref[rows, :], dn_ref[rows, :]
        qr = _rope(q_ref[rows, :].astype(F32), cos, up, dn)
        kr = _rope(k_ref[rows, :].astype(F32), cos, up, dn) * k_scale
        v = v_ref[rows, :]
        s = lax.dot_general(qr.astype(BF16), kr.astype(BF16), NT_DIMS,
                            preferred_element_type=F32) * intra
        st = state[...]
        o = (jnp.dot(s.astype(BF16), v, preferred_element_type=F32)
             + jnp.dot((qr * q_dec).astype(BF16), st.astype(BF16), preferred_element_type=F32))
        state[...] = st * chunk_dec + lax.dot_general((kr * k_dec).astype(BF16), v, TN_DIMS,
                                                      preferred_element_type=F32)
        acc_rows = pl.ds(pl.multiple_of(sci * span + r0, c), c)

        @pl.when(d == 0)
        def _():
            o_acc[acc_rows, :] = o

        @pl.when(d == 1)
        def _():
            ot = o_acc[acc_rows, :] + o
            mu = jnp.mean(ot, axis=-1, keepdims=True)
            oc = ot - mu
            var = jnp.mean(oc * oc, axis=-1, keepdims=True)
            on = oc * lax.rsqrt(var + RET_GN_EPS)
            gate = g_ref[rows, :].astype(F32)
            o_ref[rows, :] = (gate / (1.0 + jnp.exp(-gate)) * on).astype(o_ref.dtype)


def _retention(h, lg_fwd, lg_bwd, bsz, seq, n_heads, dk, dv, *, span=512):
    t = h.shape[0]
    span = _tile(seq, span)
    n_sub = span // RET_CHUNK
    n_sc = seq // span
    cos, up, dn = _rope_tables(seq, dk)
    lg = jnp.broadcast_to(jnp.stack([lg_fwd, lg_bwd])[:, :, None, None], (2, n_heads, 8, 128)).astype(F32)
    k_off = n_heads
    v_off = 2 * n_heads * dk // dv
    g_off = v_off + n_heads

    def blk(b, d, s):
        return b * n_sc + s + d * (n_sc - 1 - 2 * s)

    def held(b, d, s):
        return b * n_sc + jnp.where(d == 0, n_sc - 1, n_sc - 1 - s)

    tab_spec = pl.BlockSpec((span, dk), lambda b, hh, d, s: (s + d * (n_sc - 1 - 2 * s), 0))
    return pl.pallas_call(
        functools.partial(_ret_kernel, n_sub=n_sub, k_scale=dk ** -0.5),
        grid=(bsz, n_heads, 2, n_sc),
        in_specs=[pl.BlockSpec((1, 1, 8, 128), lambda b, hh, d, s: (d, hh, 0, 0)),
                  pl.BlockSpec((span, dk), lambda b, hh, d, s: (blk(b, d, s), hh)),
                  pl.BlockSpec((span, dk), lambda b, hh, d, s: (blk(b, d, s), k_off + hh)),
                  pl.BlockSpec((span, dv), lambda b, hh, d, s: (blk(b, d, s), v_off + hh)),
                  pl.BlockSpec((span, dv), lambda b, hh, d, s: (held(b, d, s), g_off + hh)),
                  tab_spec, tab_spec, tab_spec],
        out_specs=pl.BlockSpec((span, dv), lambda b, hh, d, s: (held(b, d, s), hh)),
        out_shape=jax.ShapeDtypeStruct((t, n_heads * dv), BF16),
        scratch_shapes=[pltpu.VMEM((dk, dv), F32), pltpu.VMEM((seq, dv), F32)],
        compiler_params=_params(4),
        name="retention",
    )(lg, h, h, h, h, cos, up, dn)


def _pack_bf16_pairs(y):
    half = y.shape[1] // 2
    bits = lax.bitcast_convert_type(y.astype(BF16).astype(F32), I32)
    return (bits[:, half:] & jnp.int32(-65536)) | lax.shift_right_logical(bits[:, :half], 16)


def _unpack_bf16_pairs(w):
    lo = lax.bitcast_convert_type(lax.shift_left(w, 16), F32).astype(BF16)
    hi = lax.bitcast_convert_type(w & jnp.int32(-65536), F32).astype(BF16)
    return lo, hi


def _proj_route_kernel(a_ref, w_ref, res_ref, g_ref, b_ref, rwh_ref, rwl_ref, rb_ref,
                       x_ref, xp_ref, idx_ref, gate_ref, rank_ref, cnt_ref, carry, *, alpha):
    @pl.when(pl.program_id(0) == 0)
    def _():
        carry[...] = jnp.zeros_like(carry)

    acc = jnp.dot(a_ref[...], w_ref[...], preferred_element_type=F32)
    y = _layer_norm(alpha * res_ref[...] + acc, g_ref[...], b_ref[...])
    x_ref[...] = y
    xp_ref[...] = _pack_bf16_pairs(y)

    yh = y.astype(BF16)
    yl = (y - yh.astype(F32)).astype(BF16)
    logits = (lax.dot_general(rwh_ref[...], yh, NT_DIMS, preferred_element_type=F32)
              + lax.dot_general(rwl_ref[...], yh, NT_DIMS, preferred_element_type=F32)
              + lax.dot_general(rwh_ref[...], yl, NT_DIMS, preferred_element_type=F32)
              + rb_ref[...])
    n_exp, tm = logits.shape
    eidx = lax.broadcasted_iota(I32, logits.shape, 0).astype(F32)
    vals, idxs = [], []
    rest = logits
    for _ in range(TOP_K):
        m = jnp.max(rest, axis=0, keepdims=True)
        am = jnp.min(jnp.where(rest == m, eidx, float(n_exp)), axis=0, keepdims=True)
        vals.append(m)
        idxs.append(am)
        rest = jnp.where(eidx == am, -jnp.inf, rest)
    exps = [jnp.exp(v - vals[0]) for v in vals]
    den = exps[0] + exps[1] + exps[2] + exps[3]
    hots = [eidx == am for am in idxs]
    multi = sum(h.astype(F32) for h in hots)

    before = (lax.broadcasted_iota(I32, (tm, tm), 0) < lax.broadcasted_iota(I32, (tm, tm), 1))
    prefix = jnp.dot(multi.astype(BF16), before.astype(BF16), preferred_element_type=F32) + carry[:, :1]
    for k in range(TOP_K):
        idx_ref[k:k + 1, :] = idxs[k].astype(I32)
        gate_ref[k:k + 1, :] = exps[k] / den
        rank_ref[k:k + 1, :] = jnp.sum(jnp.where(hots[k], prefix, 0.0), axis=0, keepdims=True).astype(I32)
    carry[...] = carry[...] + jnp.sum(multi, axis=1, keepdims=True)
    cnt_ref[...] = carry[...].astype(I32)


def _proj_route(a, w, resid, ln_g, ln_b, router_w, router_b, alpha, *, tm=256):
    t, k = a.shape
    d = w.shape[1]
    n_exp = router_w.shape[1]
    tm = _tile(t, tm)
    rwt = router_w.T
    rwh = rwt.astype(BF16)
    rwl = (rwt - rwh.astype(F32)).astype(BF16)
    row = lambda i: (i, 0)
    fixed = lambda i: (0, 0)
    col = lambda i: (0, i)
    outs = pl.pallas_call(
        functools.partial(_proj_route_kernel, alpha=alpha),
        grid=(t // tm,),
        in_specs=[pl.BlockSpec((tm, k), row),
                  pl.BlockSpec((k, d), fixed, pipeline_mode=pl.Buffered(1)),
                  pl.BlockSpec((tm, d), row),
                  pl.BlockSpec((1, d), fixed), pl.BlockSpec((1, d), fixed),
                  pl.BlockSpec((n_exp, d), fixed), pl.BlockSpec((n_exp, d), fixed),
                  pl.BlockSpec((n_exp, 1), fixed)],
        out_specs=[pl.BlockSpec((tm, d), row), pl.BlockSpec((tm, d // 2), row),
                   pl.BlockSpec((TOP_K, tm), col), pl.BlockSpec((TOP_K, tm), col),
                   pl.BlockSpec((TOP_K, tm), col), pl.BlockSpec((n_exp, 128), fixed)],
        out_shape=[jax.ShapeDtypeStruct((t, d), F32), jax.ShapeDtypeStruct((t, d // 2), I32),
                   jax.ShapeDtypeStruct((TOP_K, t), I32), jax.ShapeDtypeStruct((TOP_K, t), F32),
                   jax.ShapeDtypeStruct((TOP_K, t), I32), jax.ShapeDtypeStruct((n_exp, 128), I32)],
        scratch_shapes=[pltpu.VMEM((n_exp, 128), F32)],
        compiler_params=_params(1),
        name="proj_ln_route",
    )(a, w, resid, ln_g.reshape(1, d), ln_b.reshape(1, d), rwh, rwl, router_b.reshape(n_exp, 1))
    return outs


def _dispatch_kernel(dest_ref, cnt_ref, start_ref, nu_ref, xp_ref, xs_hbm, zblk, sem, *, n_tok_steps, n_exp):
    i = pl.program_id(0)
    tm = xp_ref.shape[0]
    n_blocks = xs_hbm.shape[0] // MOE_ROWS

    def row_copy(src, dst_row):
        return pltpu.make_async_copy(src, xs_hbm.at[pl.ds(dst_row, 1)], sem)

    def block_copy(b):
        return pltpu.make_async_copy(zblk, xs_hbm.at[pl.ds(pl.multiple_of(b * MOE_ROWS, MOE_ROWS), MOE_ROWS)], sem)

    @pl.when(i < n_tok_steps)
    def _():
        def issue(t, carry):
            for k in range(TOP_K):
                row_copy(xp_ref.at[pl.ds(t, 1)], dest_ref[k, t]).start(priority=k % 2)
            return carry

        lax.fori_loop(0, tm, issue, 0)
        for k in range(TOP_K):
            pltpu.make_async_copy(xp_ref, xs_hbm.at[pl.ds(0, tm)], sem).wait()

    @pl.when(i == n_tok_steps)
    def _():
        zblk[...] = jnp.zeros_like(zblk)
        zrow = zblk.at[pl.ds(0, 1)]

        def per_expert(e, carry):
            cnt = cnt_ref[e]
            n_pad = lax.rem(MOE_ROWS - lax.rem(cnt, MOE_ROWS), MOE_ROWS)
            first = start_ref[e] + cnt

            def issue(r, c2):
                row_copy(zrow, first + r).start()
                return c2

            def drain(r, c2):
                row_copy(zrow, 0).wait()
                return c2

            lax.fori_loop(0, n_pad, issue, 0)
            lax.fori_loop(0, n_pad, drain, 0)
            return carry

        lax.fori_loop(0, n_exp, per_expert, 0)

        def tail_issue(b, carry):
            block_copy(b).start()
            return carry

        def tail_drain(b, carry):
            block_copy(0).wait()
            return carry

        lax.fori_loop(nu_ref[0], n_blocks, tail_issue, 0)
        lax.fori_loop(nu_ref[0], n_blocks, tail_drain, 0)


def _dispatch(dest, counts, starts, n_used, xp, n_rows, *, tm=256):
    t, w = xp.shape
    tm = _tile(t, tm)
    n_steps = t // tm
    n_exp = counts.shape[0]
    smem = functools.partial(pl.BlockSpec, memory_space=pltpu.SMEM)
    return pl.pallas_call(
        functools.partial(_dispatch_kernel, n_tok_steps=n_steps, n_exp=n_exp),
        grid=(n_steps + 1,),
        in_specs=[smem((TOP_K, tm), lambda i: (0, jnp.minimum(i, n_steps - 1))),
                  smem(), smem(), smem(),
                  pl.BlockSpec((tm, w), lambda i: (jnp.minimum(i, n_steps - 1), 0))],
        out_specs=pl.BlockSpec(memory_space=pl.ANY),
        out_shape=jax.ShapeDtypeStruct((n_rows, w), I32),
        scratch_shapes=[pltpu.VMEM((MOE_ROWS, w), I32), pltpu.SemaphoreType.DMA(())],
        compiler_params=_params(1),
        name="moe_dispatch",
    )(dest, counts, starts, n_used, xp)


def _new_expert(be_ref, i):
    return (i == 0) | (be_ref[i] != be_ref[jnp.maximum(i - 1, 0)])


def _gate_up_kernel(be_ref, nu_ref, xs_ref, w_ref, b_ref, h_ref, wbf):
    i = pl.program_id(1)

    @pl.when(i < nu_ref[0])
    def _():
        @pl.when(_new_expert(be_ref, i))
        def _():
            wbf[...] = w_ref[0].astype(BF16)

        lo, hi = _unpack_bf16_pairs(xs_ref[...])
        half = lo.shape[1]
        gu = (jnp.dot(lo, wbf[:half, :], preferred_element_type=F32)
              + jnp.dot(hi, wbf[half:, :], preferred_element_type=F32) + b_ref[0])
        tq = gu.shape[1] // 2
        even = (lax.broadcasted_iota(I32, (gu.shape[0], 128), 1) & 1) == 0
        for c in range(tq // 128):
            a = gu[:, c * 128:(c + 1) * 128]
            b = gu[:, tq + c * 128:tq + (c + 1) * 128]
            gate = jnp.where(even, a, pltpu.roll(b, 1, 1))
            up = jnp.where(even, pltpu.roll(a, 127, 1), b)
            hg = jnp.minimum(gate, SWIGLU_LIMIT)
            hu = jnp.clip(up, -SWIGLU_LIMIT, SWIGLU_LIMIT)
            act = (hu + 1.0) * (hg / (1.0 + jnp.exp(-SWIGLU_ALPHA * hg)))
            h_ref[:, c * 128:(c + 1) * 128] = act.astype(h_ref.dtype)

    @pl.when(i >= nu_ref[0])
    def _():
        h_ref[...] = jnp.zeros_like(h_ref)


def _gate_up(xs, w, b, block_e, n_used, *, tn):
    n_rows, half = xs.shape
    n_exp, d, f2 = w.shape
    n_blocks = n_rows // MOE_ROWS
    row = lambda j, i, be, nu: (jnp.minimum(i, nu[0] - 1), 0)
    wsel = lambda j, i, be, nu: (be[i], 0, j)
    return pl.pallas_call(
        _gate_up_kernel,
        grid_spec=pltpu.PrefetchScalarGridSpec(
            num_scalar_prefetch=2,
            grid=(f2 // tn, n_blocks),
            in_specs=[pl.BlockSpec((MOE_ROWS, half), row),
                      pl.BlockSpec((1, d, tn), wsel), pl.BlockSpec((1, 1, tn), wsel)],
            out_specs=pl.BlockSpec((MOE_ROWS, tn // 2), lambda j, i, be, nu: (i, j)),
            scratch_shapes=[pltpu.VMEM((d, tn), BF16)],
        ),
        out_shape=jax.ShapeDtypeStruct((n_rows, f2 // 2), BF16),
        compiler_params=_params(2),
        name="moe_gate_up",
    )(block_e, n_used, xs, w, b)


def _down_kernel(be_ref, nu_ref, h_ref, w_ref, b_ref, y_ref, wbf, stage, *, group):
    i = pl.program_id(1)

    @pl.when(i < nu_ref[0])
    def _():
        @pl.when(_new_expert(be_ref, i))
        def _():
            half = group // 2
            for g0 in range(0, w_ref.shape[1], group):
                for c in range(w_ref.shape[2] // 128):
                    lanes = slice(c * 128, (c + 1) * 128)
                    stage[c, pl.ds(0, half, stride=2), :] = w_ref[0, g0:g0 + half, lanes]
                    stage[c, pl.ds(1, half, stride=2), :] = w_ref[0, g0 + half:g0 + group, lanes]
                    wbf[g0:g0 + group, lanes] = stage[c].astype(BF16)

        y_ref[...] = jnp.dot(h_ref[...], wbf[...], preferred_element_type=F32) + b_ref[0]

    @pl.when(i >= nu_ref[0])
    def _():
        y_ref[...] = jnp.zeros_like(y_ref)


def _down(h, wd, bd, block_e, n_used, *, group, tn=1024):
    n_rows, f = h.shape
    d = wd.shape[2]
    tn = _tile(d, tn)
    n_blocks = n_rows // MOE_ROWS
    wsel = lambda j, i, be, nu: (be[i], 0, j)
    return pl.pallas_call(
        functools.partial(_down_kernel, group=group),
        grid_spec=pltpu.PrefetchScalarGridSpec(
            num_scalar_prefetch=2,
            grid=(d // tn, n_blocks),
            in_specs=[pl.BlockSpec((MOE_ROWS, f), lambda j, i, be, nu: (jnp.minimum(i, nu[0] - 1), 0)),
                      pl.BlockSpec((1, f, tn), wsel), pl.BlockSpec((1, 1, tn), wsel)],
            out_specs=pl.BlockSpec((MOE_ROWS, tn), lambda j, i, be, nu: (i, j)),
            scratch_shapes=[pltpu.VMEM((f, tn), BF16), pltpu.VMEM((tn // 128, group, 128), F32)],
        ),
        out_shape=jax.ShapeDtypeStruct((n_rows, d), F32),
        compiler_params=_params(2),
        name="moe_down",
    )(block_e, n_used, h, wd, bd)


def _combine_kernel(dest_ref, next_ref, gate_ref, x_ref, g_ref, b_ref, y_hbm, o_ref, buf, sems, *, alpha):
    i = pl.program_id(0)
    tm = x_ref.shape[0]
    slot = lax.rem(i, 2)

    def gather(d_ref, s):
        def issue(t, carry):
            for k in range(TOP_K):
                pltpu.make_async_copy(y_hbm.at[pl.ds(d_ref[k, t], 1)], buf.at[s, k, pl.ds(t, 1)],
                                      sems.at[s]).start(priority=k % 2)
            return carry

        lax.fori_loop(0, tm, issue, 0)

    @pl.when(i == 0)
    def _():
        gather(dest_ref, 0)

    @pl.when(i + 1 < pl.num_programs(0))
    def _():
        gather(next_ref, 1 - slot)

    for k in range(TOP_K):
        pltpu.make_async_copy(y_hbm.at[pl.ds(0, tm)], buf.at[slot, k], sems.at[slot]).wait()
    gates = gate_ref[...]
    ffn = gates[:, 0:1] * buf[slot, 0]
    for k in range(1, TOP_K):
        ffn = ffn + gates[:, k:k + 1] * buf[slot, k]
    o_ref[...] = _layer_norm(alpha * x_ref[...] + ffn, g_ref[...], b_ref[...])


def _combine(dest, gates, x, y, ln_g, ln_b, alpha, *, tm=128):
    t, d = x.shape
    tm = _tile(t, tm)
    n_steps = t // tm
    row = lambda i: (i, 0)
    fixed = lambda i: (0, 0)
    return pl.pallas_call(
        functools.partial(_combine_kernel, alpha=alpha),
        grid=(n_steps,),
        in_specs=[pl.BlockSpec((TOP_K, tm), lambda i: (0, i), memory_space=pltpu.SMEM),
                  pl.BlockSpec((TOP_K, tm), lambda i: (0, jnp.minimum(i + 1, n_steps - 1)),
                               memory_space=pltpu.SMEM),
                  pl.BlockSpec((tm, TOP_K), row), pl.BlockSpec((tm, d), row),
                  pl.BlockSpec((1, d), fixed), pl.BlockSpec((1, d), fixed),
                  pl.BlockSpec(memory_space=pl.ANY)],
        out_specs=pl.BlockSpec((tm, d), row),
        out_shape=jax.ShapeDtypeStruct((t, d), F32),
        scratch_shapes=[pltpu.VMEM((2, TOP_K, tm, d), F32), pltpu.SemaphoreType.DMA((2,))],
        compiler_params=_params(1),
        name="moe_combine_ln",
    )(dest, dest, gates, x, ln_g.reshape(1, d), ln_b.reshape(1, d), y)


def _moe(x, xp, idx_t, gate_t, rank_t, counts2d, w_gate_up, b_gate_up, w_down, b_down, ln_g, ln_b, alpha):
    t, d = x.shape
    n_exp, _, f = w_down.shape[0], w_down.shape[1], w_gate_up.shape[2] // 2
    n_blocks = t * TOP_K // MOE_ROWS + n_exp
    n_rows = n_blocks * MOE_ROWS

    counts = counts2d[:, 0]
    padded = (counts + MOE_ROWS - 1) // MOE_ROWS * MOE_ROWS
    pad_end = jnp.cumsum(padded)
    starts = (pad_end - padded).astype(I32)
    n_used = (pad_end[-1] // MOE_ROWS).astype(I32)
    blk_start = jnp.minimum(jnp.arange(n_blocks, dtype=I32), n_used - 1) * MOE_ROWS
    block_e = jnp.sum(blk_start[:, None] >= pad_end[None, :], axis=1).astype(I32)
    block_e = jnp.minimum(block_e, n_exp - 1)
    experts = jnp.arange(n_exp, dtype=I32)
    dest = jnp.sum(jnp.where(idx_t[:, :, None] == experts, starts, 0), axis=-1) + rank_t

    nu = n_used.reshape(1)
    xs = _dispatch(dest, counts, starts, nu, xp, n_rows)
    tn = _tile(2 * f, MOE_GATE_UP_TILE)
    h = _gate_up(xs, w_gate_up, b_gate_up[:, None, :], block_e, nu, tn=tn)
    y = _down(h, w_down, b_down[:, None, :], block_e, nu, group=tn // 2)
    return _combine(dest, gate_t.T, x, y, ln_g, ln_b, alpha)


def kernel(x, attn_w_in, attn_q_gain, attn_k_gain, attn_w_out, ret_w_in, ret_decay_fwd, ret_decay_bwd,
           ret_w_out, ln_mix_g, ln_mix_b, router_w, router_b, expert_w_gate_up, expert_b_gate_up,
           expert_w_down, expert_b_down, ln_ffn_g, ln_ffn_b):
    bsz, seq, d = x.shape
    depth = ln_mix_g.shape[0]
    alpha = (2 * depth) ** 0.25
    nq = attn_w_out.shape[1]
    n_heads = nq // HEAD_DIM
    n_kv = (attn_w_in.shape[2] - nq) // (2 * HEAD_DIM)
    ret_heads = ret_decay_fwd.shape[1]
    dv = ret_w_out.shape[1] // ret_heads
    dk = (ret_w_in.shape[2] - 2 * ret_heads * dv) // (2 * ret_heads)

    xf = x.reshape(bsz * seq, d)
    for i in range(depth):
        j = i // 2
        if i % 2 == 0:
            h = _matmul(xf, attn_w_in[j].astype(BF16))
            mix = _attention(h, attn_q_gain[j], attn_k_gain[j], bsz, seq, n_heads, n_kv)
            w_out = attn_w_out[j]
        else:
            h = _matmul(xf, ret_w_in[j].astype(BF16))
            lg_f = jnp.log1p(-jnp.exp(ret_decay_fwd[j].astype(F32)))
            lg_b = jnp.log1p(-jnp.exp(ret_decay_bwd[j].astype(F32)))
            mix = _retention(h, lg_f, lg_b, bsz, seq, ret_heads, dk, dv)
            w_out = ret_w_out[j]
        x1, xp, idx_t, gate_t, rank_t, counts = _proj_route(
            mix, w_out.astype(BF16), xf, ln_mix_g[i], ln_mix_b[i], router_w[i], router_b[i], alpha)
        xf = _moe(x1, xp, idx_t, gate_t, rank_t, counts, expert_w_gate_up[i], expert_b_gate_up[i],
                  expert_w_down[i], expert_b_down[i], ln_ffn_g[i], ln_ffn_b[i], alpha)
    return xf.reshape(bsz, seq, d)
```

```python
import functools

import jax
import jax.numpy as jnp
from jax import lax
from jax.experimental import pallas as pl
from jax.experimental.pallas import tpu as pltpu

F32, BF16, I32 = jnp.float32, jnp.bfloat16, jnp.int32

HEAD_DIM = 128
GRID_W = 64
ROPE_THETA = 10000.0
TOP_K = 4
QK_NORM_EPS = 1e-6
RET_SCAN_CHUNK = 256
RET_GN_EPS = 1e-5
SWIGLU_LIMIT = 7.0
SWIGLU_ALPHA = 1.702
LN_EPS = 1e-5
LOG2_E = 1.4426950408889634

V7X_VMEM_LIMIT_BYTES = 56 * 1024 * 1024
MOE_ROWS = 256
MOE_GATE_UP_TILE = 1024

NT_DIMS = (((1,), (1,)), ((), ()))
TN_DIMS = (((0,), (0,)), ((), ()))


def _params(n_axes):
    return pltpu.CompilerParams(dimension_semantics=("arbitrary",) * n_axes,
                                vmem_limit_bytes=V7X_VMEM_LIMIT_BYTES)


def _tile(n, pref):
    t = min(n, pref)
    while n % t:
        t -= 128
    assert t > 0, (n, pref)
    return t


def _layer_norm(z, g, b):
    mu = jnp.mean(z, axis=-1, keepdims=True)
    zc = z - mu
    var = jnp.mean(zc * zc, axis=-1, keepdims=True)
    return zc * lax.rsqrt(var + LN_EPS) * g + b


def _rope_tables(seq, dim):
    rows = seq // GRID_W
    row_idx = jnp.repeat(jnp.arange(rows, dtype=F32), GRID_W)
    col_idx = jnp.tile(jnp.arange(GRID_W, dtype=F32), rows)
    quarter = dim // 4
    inv_freq = ROPE_THETA ** (-jnp.arange(quarter, dtype=F32) / quarter)
    ang_r = row_idx[:, None] * inv_freq[None, :]
    ang_c = col_idx[:, None] * inv_freq[None, :]
    ang = jnp.concatenate([ang_r, ang_r, ang_c, ang_c], axis=-1)
    cos, sin = jnp.cos(ang), jnp.sin(ang)
    first = (jnp.arange(dim) % (dim // 2)) < quarter
    return cos, jnp.where(first, -sin, 0.0), jnp.where(first, 0.0, sin)


def _rope(x, cos, sin_up, sin_dn):
    d = x.shape[-1]
    q = d // 4
    return x * cos + pltpu.roll(x, d - q, 1) * sin_up + pltpu.roll(x, q, 1) * sin_dn


def _matmul_kernel(x_ref, w_ref, o_ref):
    o_ref[...] = jnp.dot(x_ref[...].astype(BF16), w_ref[...],
                         preferred_element_type=F32).astype(o_ref.dtype)


def _matmul(x, w, *, tm=512, tn=1024):
    m, k = x.shape
    n = w.shape[1]
    tm, tn = _tile(m, tm), _tile(n, tn)
    return pl.pallas_call(
        _matmul_kernel,
        grid=(n // tn, m // tm),
        in_specs=[pl.BlockSpec((tm, k), lambda j, i: (i, 0)),
                  pl.BlockSpec((k, tn), lambda j, i: (0, j))],
        out_specs=pl.BlockSpec((tm, tn), lambda j, i: (i, j)),
        out_shape=jax.ShapeDtypeStruct((m, n), BF16),
        compiler_params=_params(2),
        name="dense_matmul",
    )(x, w)


def _matmul_rope_kernel(x_ref, w_ref, cos_ref, up_ref, dn_ref, o_ref, *, n_q_tiles, n_rope_tiles, k_scale):
    j = pl.program_id(0)
    acc = jnp.dot(x_ref[...].astype(BF16), w_ref[...], preferred_element_type=F32)

    @pl.when(j >= n_rope_tiles)
    def _():
        o_ref[...] = acc.astype(o_ref.dtype)

    @pl.when(j < n_rope_tiles)
    def _():
        scale = jnp.where(j >= n_q_tiles, k_scale, 1.0)
        dk = cos_ref.shape[1]
        cos, up, dn = cos_ref[...], up_ref[...], dn_ref[...]
        for hh in range(acc.shape[1] // dk):
            cols = slice(hh * dk, (hh + 1) * dk)
            o_ref[:, cols] = (_rope(acc[:, cols], cos, up, dn) * scale).astype(o_ref.dtype)


def _matmul_rope(x, w, seq, n_heads, dk, *, tm=512, tn=1024):
    m, k = x.shape
    n = w.shape[1]
    nqk = n_heads * dk
    tm, tn = _tile(seq, tm), _tile(nqk, tn)
    assert tn % dk == 0 and n % tn == 0
    cos, up, dn = _rope_tables(seq, dk)
    tab_spec = pl.BlockSpec((tm, dk), lambda j, i: (i % (seq // tm), 0))
    return pl.pallas_call(
        functools.partial(_matmul_rope_kernel, n_q_tiles=nqk // tn, n_rope_tiles=2 * nqk // tn,
                          k_scale=dk ** -0.5),
        grid=(n // tn, m // tm),
        in_specs=[pl.BlockSpec((tm, k), lambda j, i: (i, 0)),
                  pl.BlockSpec((k, tn), lambda j, i: (0, j)),
                  tab_spec, tab_spec, tab_spec],
        out_specs=pl.BlockSpec((tm, tn), lambda j, i: (i, j)),
        out_shape=jax.ShapeDtypeStruct((m, n), BF16),
        compiler_params=_params(2),
        name="dense_matmul_rope",
    )(x, w, cos, up, dn)


def _attn_kernel(q_ref, k_ref, v_ref, cq_ref, uq_ref, dq_ref, ck_ref, uk_ref, dk_ref,
                 qg_ref, kg_ref, o_ref, k_scr, v_scr, *, groups, scale):
    def norm_rope(x, gain, cos, up, dn):
        xn = x * lax.rsqrt(jnp.mean(x * x, axis=-1, keepdims=True) + QK_NORM_EPS) * gain
        return _rope(xn, cos, up, dn)

    @pl.when(pl.program_id(2) == 0)
    def _():
        k = k_ref[...].astype(F32)
        k_scr[...] = norm_rope(k, kg_ref[...], ck_ref[...], uk_ref[...], dk_ref[...]).astype(BF16)
        v_scr[:, :HEAD_DIM] = v_ref[...]
        v_scr[:, HEAD_DIM:] = jnp.ones((v_scr.shape[0], HEAD_DIM), BF16)

    cos, up, dn = cq_ref[...], uq_ref[...], dq_ref[...]
    for g in range(groups):
        cols = slice(g * HEAD_DIM, (g + 1) * HEAD_DIM)
        q = q_ref[:, cols].astype(F32)
        qr = (norm_rope(q, qg_ref[...], cos, up, dn) * (scale * LOG2_E)).astype(BF16)
        s = lax.dot_general(qr, k_scr[...], NT_DIMS, preferred_element_type=F32)
        p = jnp.exp2(s - jnp.max(s, axis=-1, keepdims=True))
        o = jnp.dot(p.astype(BF16), v_scr[...], preferred_element_type=F32)
        o_ref[:, cols] = (o[:, :HEAD_DIM] / o[:, HEAD_DIM:HEAD_DIM + 1]).astype(o_ref.dtype)


def _attention(h, q_gain, k_gain, bsz, seq, n_heads, n_kv, *, tq=256):
    t = h.shape[0]
    groups = n_heads // n_kv
    tq = _tile(seq, tq)
    nq = seq // tq
    cos, up, dn = _rope_tables(seq, HEAD_DIM)
    gw = groups * HEAD_DIM
    q_spec = pl.BlockSpec((tq, gw), lambda b, kv, i: (b * nq + i, kv))
    k_spec = pl.BlockSpec((seq, HEAD_DIM), lambda b, kv, i: (b, n_heads + kv))
    v_spec = pl.BlockSpec((seq, HEAD_DIM), lambda b, kv, i: (b, n_heads + n_kv + kv))
    tq_spec = pl.BlockSpec((tq, HEAD_DIM), lambda b, kv, i: (i, 0))
    tk_spec = pl.BlockSpec((seq, HEAD_DIM), lambda b, kv, i: (0, 0))
    gain_spec = pl.BlockSpec((1, HEAD_DIM), lambda b, kv, i: (0, 0))
    return pl.pallas_call(
        functools.partial(_attn_kernel, groups=groups, scale=HEAD_DIM ** -0.5),
        grid=(bsz, n_kv, nq),
        in_specs=[q_spec, k_spec, v_spec, tq_spec, tq_spec, tq_spec, tk_spec, tk_spec, tk_spec,
                  gain_spec, gain_spec],
        out_specs=pl.BlockSpec((tq, gw), lambda b, kv, i: (b * nq + i, kv)),
        out_shape=jax.ShapeDtypeStruct((t, n_heads * HEAD_DIM), BF16),
        scratch_shapes=[pltpu.VMEM((seq, HEAD_DIM), BF16), pltpu.VMEM((seq, 2 * HEAD_DIM), BF16)],
        compiler_params=_params(3),
        name="gqa_attention",
    )(h, h, h, cos, up, dn, cos, up, dn, q_gain.reshape(1, HEAD_DIM), k_gain.reshape(1, HEAD_DIM))


def _ret_kernel(lg_ref, q_ref, k_ref, v_ref, g_ref, o_ref, state, o_acc, *, n_sub):
    d = pl.program_id(2)
    sc = pl.program_id(3)
    n_sc = pl.num_programs(3)
    c = RET_SCAN_CHUNK
    span = n_sub * c

    @pl.when(sc == 0)
    def _():
        state[...] = jnp.zeros_like(state)

    lg = lg_ref[0, 0][:1, :1]
    sgn = 1 - 2 * d
    dist = (lax.broadcasted_iota(I32, (c, c), 0) - lax.broadcasted_iota(I32, (c, c), 1)) * sgn
    intra = jnp.where(dist >= 0, jnp.exp(lg * jnp.maximum(dist, 0).astype(F32)), 0.0)
    pos = lax.broadcasted_iota(I32, (c, 1), 0)
    q_dec = jnp.exp(lg * jnp.where(d == 0, pos + 1, c - pos).astype(F32))
    k_dec = jnp.exp(lg * jnp.where(d == 0, c - 1 - pos, pos).astype(F32))
    chunk_dec = jnp.exp(lg * float(c))
    sci = sc + d * (n_sc - 1 - 2 * sc)

    for j in range(n_sub):
        cj = j + d * (n_sub - 1 - 2 * j)
        r0 = pl.multiple_of(cj * c, c)
        rows = pl.ds(r0, c)
        q = q_ref[rows, :]
        k = k_ref[rows, :]
        v = v_ref[rows, :]
        s = lax.dot_general(q, k, NT_DIMS, preferred_element_type=F32) * intra
        st = state[...]
        o = (jnp.dot(s.astype(BF16), v, preferred_element_type=F32)
             + q_dec * jnp.dot(q, st.astype(BF16), preferred_element_type=F32))
        kd = (k.astype(F32) * k_dec).astype(BF16)
        state[...] = st * chunk_dec + lax.dot_general(kd, v, TN_DIMS, preferred_element_type=F32)
        acc_rows = pl.ds(pl.multiple_of(sci * span + r0, c), c)

        @pl.when(d == 0)
        def _():
            o_acc[acc_rows, :] = o

        @pl.when(d == 1)
        def _():
            ot = o_acc[acc_rows, :] + o
            mu = jnp.mean(ot, axis=-1, keepdims=True)
            oc = ot - mu
            var = jnp.mean(oc * oc, axis=-1, keepdims=True)
            on = oc * lax.rsqrt(var + RET_GN_EPS)
            gate = g_ref[rows, :].astype(F32)
            o_ref[rows, :] = (gate / (1.0 + jnp.exp(-gate)) * on).astype(o_ref.dtype)


def _retention(h, lg_fwd, lg_bwd, bsz, seq, n_heads, dk, dv, *, span=1024):
    t = h.shape[0]
    span = _tile(seq, span)
    n_sub = span // RET_SCAN_CHUNK
    n_sc = seq // span
    lg = jnp.broadcast_to(jnp.stack([lg_fwd, lg_bwd])[:, :, None, None], (2, n_heads, 8, 128)).astype(F32)
    k_off = n_heads
    v_off = 2 * n_heads * dk // dv
    g_off = v_off + n_heads

    def blk(b, d, s):
        return b * n_sc + s + d * (n_sc - 1 - 2 * s)

    def held(b, d, s):
        return b * n_sc + jnp.where(d == 0, n_sc - 1, n_sc - 1 - s)

    return pl.pallas_call(
        functools.partial(_ret_kernel, n_sub=n_sub),
        grid=(bsz, n_heads, 2, n_sc),
        in_specs=[pl.BlockSpec((1, 1, 8, 128), lambda b, hh, d, s: (d, hh, 0, 0)),
                  pl.BlockSpec((span, dk), lambda b, hh, d, s: (blk(b, d, s), hh)),
                  pl.BlockSpec((span, dk), lambda b, hh, d, s: (blk(b, d, s), k_off + hh)),
                  pl.BlockSpec((span, dv), lambda b, hh, d, s: (blk(b, d, s), v_off + hh)),
                  pl.BlockSpec((span, dv), lambda b, hh, d, s: (held(b, d, s), g_off + hh))],
        out_specs=pl.BlockSpec((span, dv), lambda b, hh, d, s: (held(b, d, s), hh)),
        out_shape=jax.ShapeDtypeStruct((t, n_heads * dv), BF16),
        scratch_shapes=[pltpu.VMEM((dk, dv), F32), pltpu.VMEM((seq, dv), F32)],
        compiler_params=_params(4),
        name="retention",
    )(lg, h, h, h, h)


def _pack_bf16_pairs(y):
    half = y.shape[1] // 2
    bits = lax.bitcast_convert_type(y.astype(BF16).astype(F32), I32)
    return (bits[:, half:] & jnp.int32(-65536)) | lax.shift_right_logical(bits[:, :half], 16)


def _unpack_bf16_pairs(w):
    lo = lax.bitcast_convert_type(lax.shift_left(w, 16), F32).astype(BF16)
    hi = lax.bitcast_convert_type(w & jnp.int32(-65536), F32).astype(BF16)
    return lo, hi


def _proj_route_kernel(a_ref, w_ref, res_ref, g_ref, b_ref, rwh_ref, rwl_ref, rb_ref,
                       x_ref, xp_ref, idx_ref, gate_ref, rank_ref, cnt_ref, carry, *, alpha):
    @pl.when(pl.program_id(0) == 0)
    def _():
        carry[...] = jnp.zeros_like(carry)

    acc = jnp.dot(a_ref[...], w_ref[...], preferred_element_type=F32)
    y = _layer_norm(alpha * res_ref[...] + acc, g_ref[...], b_ref[...])
    x_ref[...] = y
    xp_ref[...] = _pack_bf16_pairs(y)

    yh = y.astype(BF16)
    yl = (y - yh.astype(F32)).astype(BF16)
    logits = (lax.dot_general(rwh_ref[...], yh, NT_DIMS, preferred_element_type=F32)
              + lax.dot_general(rwl_ref[...], yh, NT_DIMS, preferred_element_type=F32)
              + lax.dot_general(rwh_ref[...], yl, NT_DIMS, preferred_element_type=F32)
              + rb_ref[...])
    n_exp, tm = logits.shape
    eidx = lax.broadcasted_iota(I32, logits.shape, 0).astype(F32)
    vals, idxs = [], []
    rest = logits
    for _ in range(TOP_K):
        m = jnp.max(rest, axis=0, keepdims=True)
        am = jnp.min(jnp.where(rest == m, eidx, float(n_exp)), axis=0, keepdims=True)
        vals.append(m)
        idxs.append(am)
        rest = jnp.where(eidx == am, -jnp.inf, rest)
    exps = [jnp.exp(v - vals[0]) for v in vals]
    den = exps[0] + exps[1] + exps[2] + exps[3]
    hots = [eidx == am for am in idxs]
    multi = sum(h.astype(F32) for h in hots)

    before = (lax.broadcasted_iota(I32, (tm, tm), 0) < lax.broadcasted_iota(I32, (tm, tm), 1))
    prefix = jnp.dot(multi.astype(BF16), before.astype(BF16), preferred_element_type=F32) + carry[:, :1]
    for k in range(TOP_K):
        idx_ref[k:k + 1, :] = idxs[k].astype(I32)
        gate_ref[k:k + 1, :] = exps[k] / den
        rank_ref[k:k + 1, :] = jnp.sum(jnp.where(hots[k], prefix, 0.0), axis=0, keepdims=True).astype(I32)
    carry[...] = carry[...] + jnp.sum(multi, axis=1, keepdims=True)
    cnt_ref[...] = carry[...].astype(I32)


def _proj_route(a, w, resid, ln_g, ln_b, router_w, router_b, alpha, *, tm=256):
    t, k = a.shape
    d = w.shape[1]
    n_exp = router_w.shape[1]
    tm = _tile(t, tm)
    rwt = router_w.T
    rwh = rwt.astype(BF16)
    rwl = (rwt - rwh.astype(F32)).astype(BF16)
    row = lambda i: (i, 0)
    fixed = lambda i: (0, 0)
    col = lambda i: (0, i)
    outs = pl.pallas_call(
        functools.partial(_proj_route_kernel, alpha=alpha),
        grid=(t // tm,),
        in_specs=[pl.BlockSpec((tm, k), row),
                  pl.BlockSpec((k, d), fixed, pipeline_mode=pl.Buffered(1)),
                  pl.BlockSpec((tm, d), row),
                  pl.BlockSpec((1, d), fixed), pl.BlockSpec((1, d), fixed),
                  pl.BlockSpec((n_exp, d), fixed), pl.BlockSpec((n_exp, d), fixed),
                  pl.BlockSpec((n_exp, 1), fixed)],
        out_specs=[pl.BlockSpec((tm, d), row), pl.BlockSpec((tm, d // 2), row),
                   pl.BlockSpec((TOP_K, tm), col), pl.BlockSpec((TOP_K, tm), col),
                   pl.BlockSpec((TOP_K, tm), col), pl.BlockSpec((n_exp, 128), fixed)],
        out_shape=[jax.ShapeDtypeStruct((t, d), F32), jax.ShapeDtypeStruct((t, d // 2), I32),
                   jax.ShapeDtypeStruct((TOP_K, t), I32), jax.ShapeDtypeStruct((TOP_K, t), F32),
                   jax.ShapeDtypeStruct((TOP_K, t), I32), jax.ShapeDtypeStruct((n_exp, 128), I32)],
        scratch_shapes=[pltpu.VMEM((n_exp, 128), F32)],
        compiler_params=_params(1),
        name="proj_ln_route",
    )(a, w, resid, ln_g.reshape(1, d), ln_b.reshape(1, d), rwh, rwl, router_b.reshape(n_exp, 1))
    return outs


def _dispatch_kernel(dest_ref, cnt_ref, start_ref, nu_ref, xp_ref, xs_hbm, zblk, sem, *, n_tok_steps, n_exp):
    i = pl.program_id(0)
    tm = xp_ref.shape[0]
    n_blocks = xs_hbm.shape[0] // MOE_ROWS

    def row_copy(src, dst_row):
        return pltpu.make_async_copy(src, xs_hbm.at[pl.ds(dst_row, 1)], sem)

    def block_copy(b):
        return pltpu.make_async_copy(zblk, xs_hbm.at[pl.ds(pl.multiple_of(b * MOE_ROWS, MOE_ROWS), MOE_ROWS)], sem)

    @pl.when(i < n_tok_steps)
    def _():
        def issue(t, carry):
            for k in range(TOP_K):
                row_copy(xp_ref.at[pl.ds(t, 1)], dest_ref[k, t]).start(priority=k % 2)
            return carry

        lax.fori_loop(0, tm, issue, 0)
        for k in range(TOP_K):
            pltpu.make_async_copy(xp_ref, xs_hbm.at[pl.ds(0, tm)], sem).wait()

    @pl.when(i == n_tok_steps)
    def _():
        zblk[...] = jnp.zeros_like(zblk)
        zrow = zblk.at[pl.ds(0, 1)]

        def per_expert(e, carry):
            cnt = cnt_ref[e]
            n_pad = lax.rem(MOE_ROWS - lax.rem(cnt, MOE_ROWS), MOE_ROWS)
            first = start_ref[e] + cnt

            def issue(r, c2):
                row_copy(zrow, first + r).start()
                return c2

            def drain(r, c2):
                row_copy(zrow, 0).wait()
                return c2

            lax.fori_loop(0, n_pad, issue, 0)
            lax.fori_loop(0, n_pad, drain, 0)
            return carry

        lax.fori_loop(0, n_exp, per_expert, 0)

        def tail_issue(b, carry):
            block_copy(b).start()
            return carry

        def tail_drain(b, carry):
            block_copy(0).wait()
            return carry

        lax.fori_loop(nu_ref[0], n_blocks, tail_issue, 0)
        lax.fori_loop(nu_ref[0], n_blocks, tail_drain, 0)


def _dispatch(dest, counts, starts, n_used, xp, n_rows, *, tm=256):
    t, w = xp.shape
    tm = _tile(t, tm)
    n_steps = t // tm
    n_exp = counts.shape[0]
    smem = functools.partial(pl.BlockSpec, memory_space=pltpu.SMEM)
    return pl.pallas_call(
        functools.partial(_dispatch_kernel, n_tok_steps=n_steps, n_exp=n_exp),
        grid=(n_steps + 1,),
        in_specs=[smem((TOP_K, tm), lambda i: (0, jnp.minimum(i, n_steps - 1))),
                  smem(), smem(), smem(),
                  pl.BlockSpec((tm, w), lambda i: (jnp.minimum(i, n_steps - 1), 0))],
        out_specs=pl.BlockSpec(memory_space=pl.ANY),
        out_shape=jax.ShapeDtypeStruct((n_rows, w), I32),
        scratch_shapes=[pltpu.VMEM((MOE_ROWS, w), I32), pltpu.SemaphoreType.DMA(())],
        compiler_params=_params(1),
        name="moe_dispatch",
    )(dest, counts, starts, n_used, xp)


def _stream_expert_blocks(start_ref, nblk_ref, nu_ref, src_hbm, dst_hbm, dst_cols, inbuf, outbuf,
                          insem, outsem, prepare, compute):
    e = pl.program_id(1)
    nb = nblk_ref[e]
    base = start_ref[e]

    def rows(r):
        return pl.ds(pl.multiple_of(base + r * MOE_ROWS, MOE_ROWS), MOE_ROWS)

    def load(r, slot):
        return pltpu.make_async_copy(src_hbm.at[rows(r)], inbuf.at[slot], insem.at[slot])

    def store(r, slot):
        return pltpu.make_async_copy(outbuf.at[slot], dst_hbm.at[rows(r), dst_cols], outsem.at[slot])

    @pl.when(nb > 0)
    def _():
        load(0, 0).start()
        prepare()

        def body(r, carry):
            slot = lax.rem(r, 2)

            @pl.when(r + 1 < nb)
            def _():
                load(r + 1, 1 - slot).start()

            load(r, slot).wait()

            @pl.when(r >= 2)
            def _():
                store(r - 2, slot).wait()

            compute(inbuf.at[slot], outbuf.at[slot])
            store(r, slot).start()
            return carry

        lax.fori_loop(0, nb, body, 0)

        @pl.when(nb >= 2)
        def _():
            store(nb - 2, lax.rem(nb, 2)).wait()

        store(nb - 1, lax.rem(nb + 1, 2)).wait()

    @pl.when(e == pl.num_programs(1) - 1)
    def _():
        outbuf[0] = jnp.zeros(outbuf.shape[1:], outbuf.dtype)

        def tail(b):
            blk = pl.ds(pl.multiple_of(b * MOE_ROWS, MOE_ROWS), MOE_ROWS)
            return pltpu.make_async_copy(outbuf.at[0], dst_hbm.at[blk, dst_cols], outsem.at[0])

        def issue(b, carry):
            tail(b).start()
            return carry

        def drain(b, carry):
            tail(b).wait()
            return carry

        n_blocks = dst_hbm.shape[0] // MOE_ROWS
        lax.fori_loop(nu_ref[0], n_blocks, issue, 0)
        lax.fori_loop(nu_ref[0], n_blocks, drain, 0)


def _gate_up_kernel(start_ref, nblk_ref, nu_ref, w_ref, b_ref, xs_hbm, h_hbm, wbf, xbuf, hbuf, xsem, hsem):
    tq = hbuf.shape[2]
    cols = pl.ds(pl.multiple_of(pl.program_id(0) * tq, tq), tq)

    def prepare():
        wbf[...] = w_ref[0, 0].astype(BF16)

    def compute(x_ref, h_ref):
        lo, hi = _unpack_bf16_pairs(x_ref[...])
        half = lo.shape[1]
        gu = (jnp.dot(lo, wbf[:half, :], preferred_element_type=F32)
              + jnp.dot(hi, wbf[half:, :], preferred_element_type=F32) + b_ref[0, 0])
        even = (lax.broadcasted_iota(I32, (gu.shape[0], 128), 1) & 1) == 0
        for c in range(tq // 128):
            a = gu[:, c * 128:(c + 1) * 128]
            b = gu[:, tq + c * 128:tq + (c + 1) * 128]
            gate = jnp.where(even, a, pltpu.roll(b, 1, 1))
            up = jnp.where(even, pltpu.roll(a, 127, 1), b)
            hg = jnp.minimum(gate, SWIGLU_LIMIT)
            hu = jnp.clip(up, -SWIGLU_LIMIT, SWIGLU_LIMIT)
            act = (hu + 1.0) * (hg / (1.0 + jnp.exp(-SWIGLU_ALPHA * hg)))
            h_ref[:, c * 128:(c + 1) * 128] = act.astype(h_ref.dtype)

    _stream_expert_blocks(start_ref, nblk_ref, nu_ref, xs_hbm, h_hbm, cols, xbuf, hbuf, xsem, hsem,
                          prepare, compute)


def _expert_call(kernel_fn, layer, starts, nblk, n_used, w, b, src, out_shape, tn, out_w, scratch, name):
    n_exp, d_in, n_out = w.shape[1:]
    smem = pl.BlockSpec(memory_space=pltpu.SMEM)
    hbm = pl.BlockSpec(memory_space=pl.ANY)
    wsel = lambda j, e: (layer, e, 0, j)
    return pl.pallas_call(
        kernel_fn,
        grid=(n_out // tn, n_exp),
        in_specs=[smem, smem, smem,
                  pl.BlockSpec((1, 1, d_in, tn), wsel), pl.BlockSpec((1, 1, 1, tn), wsel), hbm],
        out_specs=hbm,
        out_shape=out_shape,
        scratch_shapes=scratch + [pltpu.VMEM((2, MOE_ROWS, src.shape[1]), src.dtype),
                                  pltpu.VMEM((2, MOE_ROWS, out_w), out_shape.dtype),
                                  pltpu.SemaphoreType.DMA((2,)), pltpu.SemaphoreType.DMA((2,))],
        compiler_params=_params(2),
        name=name,
    )(starts, nblk, n_used, w, b.reshape(b.shape[0], n_exp, 1, n_out), src)


def _gate_up(xs, w, b, layer, starts, nblk, n_used, *, tn):
    d, f2 = w.shape[2:]
    return _expert_call(_gate_up_kernel, layer, starts, nblk, n_used, w, b, xs,
                        jax.ShapeDtypeStruct((xs.shape[0], f2 // 2), BF16), tn, tn // 2,
                        [pltpu.VMEM((d, tn), BF16)], "moe_gate_up")


def _down_kernel(start_ref, nblk_ref, nu_ref, w_ref, b_ref, h_hbm, y_hbm, wbf, stage, hbuf, ybuf,
                 hsem, ysem, *, group):
    tn = ybuf.shape[2]
    cols = pl.ds(pl.multiple_of(pl.program_id(0) * tn, tn), tn)

    def prepare():
        half = group // 2
        for g0 in range(0, wbf.shape[0], group):
            for c in range(tn // 128):
                lanes = slice(c * 128, (c + 1) * 128)
                stage[c, pl.ds(0, half, stride=2), :] = w_ref[0, 0, g0:g0 + half, lanes]
                stage[c, pl.ds(1, half, stride=2), :] = w_ref[0, 0, g0 + half:g0 + group, lanes]
                wbf[g0:g0 + group, lanes] = stage[c].astype(BF16)

    def compute(h_ref, y_ref):
        y_ref[...] = jnp.dot(h_ref[...], wbf[...], preferred_element_type=F32) + b_ref[0, 0]

    _stream_expert_blocks(start_ref, nblk_ref, nu_ref, h_hbm, y_hbm, cols, hbuf, ybuf, hsem, ysem,
                          prepare, compute)


def _down(h, w, b, layer, starts, nblk, n_used, *, group, tn=1024):
    f, d = w.shape[2:]
    tn = _tile(d, tn)
    return _expert_call(functools.partial(_down_kernel, group=group), layer, starts, nblk, n_used, w, b, h,
                        jax.ShapeDtypeStruct((h.shape[0], d), F32), tn, tn,
                        [pltpu.VMEM((f, tn), BF16), pltpu.VMEM((tn // 128, group, 128), F32)], "moe_down")


def _combine_kernel(dest_ref, next_ref, gate_ref, x_ref, g_ref, b_ref, y_hbm, o_ref, buf, sems, *, alpha):
    i = pl.program_id(0)
    tm = x_ref.shape[0]
    slot = lax.rem(i, 2)

    def gather(d_ref, s):
        def issue(t, carry):
            for k in range(TOP_K):
                pltpu.make_async_copy(y_hbm.at[pl.ds(d_ref[k, t], 1)], buf.at[s, k, pl.ds(t, 1)],
                                      sems.at[s]).start(priority=k % 2)
            return carry

        lax.fori_loop(0, tm, issue, 0)

    @pl.when(i == 0)
    def _():
        gather(dest_ref, 0)

    @pl.when(i + 1 < pl.num_programs(0))
    def _():
        gather(next_ref, 1 - slot)

    for k in range(TOP_K):
        pltpu.make_async_copy(y_hbm.at[pl.ds(0, tm)], buf.at[slot, k], sems.at[slot]).wait()
    gates = gate_ref[...]
    ffn = gates[:, 0:1] * buf[slot, 0]
    for k in range(1, TOP_K):
        ffn = ffn + gates[:, k:k + 1] * buf[slot, k]
    o_ref[...] = _layer_norm(alpha * x_ref[...] + ffn, g_ref[...], b_ref[...])


def _combine(dest, gates, x, y, ln_g, ln_b, alpha, *, tm=128):
    t, d = x.shape
    tm = _tile(t, tm)
    n_steps = t // tm
    row = lambda i: (i, 0)
    fixed = lambda i: (0, 0)
    return pl.pallas_call(
        functools.partial(_combine_kernel, alpha=alpha),
        grid=(n_steps,),
        in_specs=[pl.BlockSpec((TOP_K, tm), lambda i: (0, i), memory_space=pltpu.SMEM),
                  pl.BlockSpec((TOP_K, tm), lambda i: (0, jnp.minimum(i + 1, n_steps - 1)),
                               memory_space=pltpu.SMEM),
                  pl.BlockSpec((tm, TOP_K), row), pl.BlockSpec((tm, d), row),
                  pl.BlockSpec((1, d), fixed), pl.BlockSpec((1, d), fixed),
                  pl.BlockSpec(memory_space=pl.ANY)],
        out_specs=pl.BlockSpec((tm, d), row),
        out_shape=jax.ShapeDtypeStruct((t, d), F32),
        scratch_shapes=[pltpu.VMEM((2, TOP_K, tm, d), F32), pltpu.SemaphoreType.DMA((2,))],
        compiler_params=_params(1),
        name="moe_combine_ln",
    )(dest, dest, gates, x, ln_g.reshape(1, d), ln_b.reshape(1, d), y)


def _moe(x, xp, idx_t, gate_t, rank_t, counts2d, layer, w_gate_up, b_gate_up, w_down, b_down, ln_g, ln_b, alpha):
    t, d = x.shape
    n_exp, f = w_down.shape[1], w_down.shape[2]
    n_blocks = t * TOP_K // MOE_ROWS + n_exp
    n_rows = n_blocks * MOE_ROWS

    counts = counts2d[:, 0]
    nblk = (counts + MOE_ROWS - 1) // MOE_ROWS
    pad_end = jnp.cumsum(nblk) * MOE_ROWS
    starts = (pad_end - nblk * MOE_ROWS).astype(I32)
    nu = (pad_end[-1:] // MOE_ROWS).astype(I32)
    experts = jnp.arange(n_exp, dtype=I32)
    dest = jnp.sum(jnp.where(idx_t[:, :, None] == experts, starts, 0), axis=-1) + rank_t

    xs = _dispatch(dest, counts, starts, nu, xp, n_rows)
    tn = _tile(2 * f, MOE_GATE_UP_TILE)
    h = _gate_up(xs, w_gate_up, b_gate_up, layer, starts, nblk, nu, tn=tn)
    y = _down(h, w_down, b_down, layer, starts, nblk, nu, group=tn // 2)
    return _combine(dest, gate_t.T, x, y, ln_g, ln_b, alpha)


def kernel(x, attn_w_in, attn_q_gain, attn_k_gain, attn_w_out, ret_w_in, ret_decay_fwd, ret_decay_bwd,
           ret_w_out, ln_mix_g, ln_mix_b, router_w, router_b, expert_w_gate_up, expert_b_gate_up,
           expert_w_down, expert_b_down, ln_ffn_g, ln_ffn_b):
    bsz, seq, d = x.shape
    depth = ln_mix_g.shape[0]
    alpha = (2 * depth) ** 0.25
    nq = attn_w_out.shape[1]
    n_heads = nq // HEAD_DIM
    n_kv = (attn_w_in.shape[2] - nq) // (2 * HEAD_DIM)
    ret_heads = ret_decay_fwd.shape[1]
    dv = ret_w_out.shape[1] // ret_heads
    dk = (ret_w_in.shape[2] - 2 * ret_heads * dv) // (2 * ret_heads)

    xf = x.reshape(bsz * seq, d)
    for i in range(depth):
        j = i // 2
        if i % 2 == 0:
            h = _matmul(xf, attn_w_in[j].astype(BF16))
            mix = _attention(h, attn_q_gain[j], attn_k_gain[j], bsz, seq, n_heads, n_kv)
            w_out = attn_w_out[j]
        else:
            h = _matmul_rope(xf, ret_w_in[j].astype(BF16), seq, ret_heads, dk)
            lg_f = jnp.log1p(-jnp.exp(ret_decay_fwd[j].astype(F32)))
            lg_b = jnp.log1p(-jnp.exp(ret_decay_bwd[j].astype(F32)))
            mix = _retention(h, lg_f, lg_b, bsz, seq, ret_heads, dk, dv)
            w_out = ret_w_out[j]
        x1, xp, idx_t, gate_t, rank_t, counts = _proj_route(
            mix, w_out.astype(BF16), xf, ln_mix_g[i], ln_mix_b[i], router_w[i], router_b[i], alpha)
        xf = _moe(x1, xp, idx_t, gate_t, rank_t, counts, i, expert_w_gate_up, expert_b_gate_up,
                  expert_w_down, expert_b_down, ln_ffn_g[i], ln_ffn_b[i], alpha)
    return xf.reshape(bsz, seq, d)
```

```python
import functools

import jax
import jax.numpy as jnp
from jax import lax
from jax.experimental import pallas as pl
from jax.experimental.pallas import tpu as pltpu

F32, BF16, I32 = jnp.float32, jnp.bfloat16, jnp.int32

HEAD_DIM = 128
GRID_W = 64
ROPE_THETA = 10000.0
TOP_K = 4
QK_NORM_EPS = 1e-6
RET_SCAN_CHUNK = 256
RET_GN_EPS = 1e-5
SWIGLU_LIMIT = 7.0
SWIGLU_ALPHA = 1.702
LN_EPS = 1e-5
LOG2_E = 1.4426950408889634

V7X_VMEM_LIMIT_BYTES = 56 * 1024 * 1024
MOE_ROWS = 256
MOE_GATE_UP_TILE = 1024
MOE_IN_BUFS = 3
XFER_PRIORITY = 1

NT_DIMS = (((1,), (1,)), ((), ()))
TN_DIMS = (((0,), (0,)), ((), ()))


def _params(n_axes):
    return pltpu.CompilerParams(dimension_semantics=("arbitrary",) * n_axes,
                                vmem_limit_bytes=V7X_VMEM_LIMIT_BYTES)


def _tile(n, pref):
    t = min(n, pref)
    while n % t:
        t -= 128
    assert t > 0, (n, pref)
    return t


def _layer_norm(z, g, b):
    mu = jnp.mean(z, axis=-1, keepdims=True)
    zc = z - mu
    var = jnp.mean(zc * zc, axis=-1, keepdims=True)
    return zc * lax.rsqrt(var + LN_EPS) * g + b


def _rope_tables(seq, dim):
    rows = seq // GRID_W
    row_idx = jnp.repeat(jnp.arange(rows, dtype=F32), GRID_W)
    col_idx = jnp.tile(jnp.arange(GRID_W, dtype=F32), rows)
    quarter = dim // 4
    inv_freq = ROPE_THETA ** (-jnp.arange(quarter, dtype=F32) / quarter)
    ang_r = row_idx[:, None] * inv_freq[None, :]
    ang_c = col_idx[:, None] * inv_freq[None, :]
    ang = jnp.concatenate([ang_r, ang_r, ang_c, ang_c], axis=-1)
    cos, sin = jnp.cos(ang), jnp.sin(ang)
    first = (jnp.arange(dim) % (dim // 2)) < quarter
    return cos, jnp.where(first, -sin, 0.0), jnp.where(first, 0.0, sin)


def _rope(x, cos, sin_up, sin_dn):
    d = x.shape[-1]
    q = d // 4
    return x * cos + pltpu.roll(x, d - q, 1) * sin_up + pltpu.roll(x, q, 1) * sin_dn


def _matmul_kernel(x_ref, w_ref, o_ref):
    o_ref[...] = jnp.dot(x_ref[...].astype(BF16), w_ref[...],
                         preferred_element_type=F32).astype(o_ref.dtype)


def _matmul(x, w, *, tm=512, tn=1024):
    m, k = x.shape
    n = w.shape[1]
    tm, tn = _tile(m, tm), _tile(n, tn)
    return pl.pallas_call(
        _matmul_kernel,
        grid=(n // tn, m // tm),
        in_specs=[pl.BlockSpec((tm, k), lambda j, i: (i, 0)),
                  pl.BlockSpec((k, tn), lambda j, i: (0, j))],
        out_specs=pl.BlockSpec((tm, tn), lambda j, i: (i, j)),
        out_shape=jax.ShapeDtypeStruct((m, n), BF16),
        compiler_params=_params(2),
        name="dense_matmul",
    )(x, w)


def _matmul_rope_kernel(x_ref, w_ref, cos_ref, up_ref, dn_ref, o_ref, *, n_q_tiles, n_rope_tiles, k_scale):
    j = pl.program_id(0)
    acc = jnp.dot(x_ref[...].astype(BF16), w_ref[...], preferred_element_type=F32)

    @pl.when(j >= n_rope_tiles)
    def _():
        o_ref[...] = acc.astype(o_ref.dtype)

    @pl.when(j < n_rope_tiles)
    def _():
        scale = jnp.where(j >= n_q_tiles, k_scale, 1.0)
        dk = cos_ref.shape[1]
        cos, up, dn = cos_ref[...], up_ref[...], dn_ref[...]
        for hh in range(acc.shape[1] // dk):
            cols = slice(hh * dk, (hh + 1) * dk)
            o_ref[:, cols] = (_rope(acc[:, cols], cos, up, dn) * scale).astype(o_ref.dtype)


def _matmul_rope(x, w, seq, n_heads, dk, *, tm=512, tn=1024):
    m, k = x.shape
    n = w.shape[1]
    nqk = n_heads * dk
    tm, tn = _tile(seq, tm), _tile(nqk, tn)
    assert tn % dk == 0 and n % tn == 0
    cos, up, dn = _rope_tables(seq, dk)
    tab_spec = pl.BlockSpec((tm, dk), lambda j, i: (i % (seq // tm), 0))
    return pl.pallas_call(
        functools.partial(_matmul_rope_kernel, n_q_tiles=nqk // tn, n_rope_tiles=2 * nqk // tn,
                          k_scale=dk ** -0.5),
        grid=(n // tn, m // tm),
        in_specs=[pl.BlockSpec((tm, k), lambda j, i: (i, 0)),
                  pl.BlockSpec((k, tn), lambda j, i: (0, j)),
                  tab_spec, tab_spec, tab_spec],
        out_specs=pl.BlockSpec((tm, tn), lambda j, i: (i, j)),
        out_shape=jax.ShapeDtypeStruct((m, n), BF16),
        compiler_params=_params(2),
        name="dense_matmul_rope",
    )(x, w, cos, up, dn)


def _attn_kernel(q_ref, k_ref, v_ref, cq_ref, uq_ref, dq_ref, ck_ref, uk_ref, dk_ref,
                 qg_ref, kg_ref, o_ref, k_scr, v_scr, *, groups, scale):
    def norm_rope(x, gain, cos, up, dn):
        xn = x * lax.rsqrt(jnp.mean(x * x, axis=-1, keepdims=True) + QK_NORM_EPS) * gain
        return _rope(xn, cos, up, dn)

    @pl.when(pl.program_id(2) == 0)
    def _():
        k = k_ref[...].astype(F32)
        k_scr[...] = norm_rope(k, kg_ref[...], ck_ref[...], uk_ref[...], dk_ref[...]).astype(BF16)
        v_scr[:, :HEAD_DIM] = v_ref[...]
        v_scr[:, HEAD_DIM:] = jnp.ones((v_scr.shape[0], HEAD_DIM), BF16)

    cos, up, dn = cq_ref[...], uq_ref[...], dq_ref[...]
    for g in range(groups):
        cols = slice(g * HEAD_DIM, (g + 1) * HEAD_DIM)
        q = q_ref[:, cols].astype(F32)
        qr = (norm_rope(q, qg_ref[...], cos, up, dn) * (scale * LOG2_E)).astype(BF16)
        s = lax.dot_general(qr, k_scr[...], NT_DIMS, preferred_element_type=F32)
        p = jnp.exp2(s - jnp.max(s, axis=-1, keepdims=True))
        o = jnp.dot(p.astype(BF16), v_scr[...], preferred_element_type=F32)
        o_ref[:, cols] = (o[:, :HEAD_DIM] / o[:, HEAD_DIM:HEAD_DIM + 1]).astype(o_ref.dtype)


def _attention(h, q_gain, k_gain, bsz, seq, n_heads, n_kv, *, tq=256):
    t = h.shape[0]
    groups = n_heads // n_kv
    tq = _tile(seq, tq)
    nq = seq // tq
    cos, up, dn = _rope_tables(seq, HEAD_DIM)
    gw = groups * HEAD_DIM
    q_spec = pl.BlockSpec((tq, gw), lambda b, kv, i: (b * nq + i, kv))
    k_spec = pl.BlockSpec((seq, HEAD_DIM), lambda b, kv, i: (b, n_heads + kv))
    v_spec = pl.BlockSpec((seq, HEAD_DIM), lambda b, kv, i: (b, n_heads + n_kv + kv))
    tq_spec = pl.BlockSpec((tq, HEAD_DIM), lambda b, kv, i: (i, 0))
    tk_spec = pl.BlockSpec((seq, HEAD_DIM), lambda b, kv, i: (0, 0))
    gain_spec = pl.BlockSpec((1, HEAD_DIM), lambda b, kv, i: (0, 0))
    return pl.pallas_call(
        functools.partial(_attn_kernel, groups=groups, scale=HEAD_DIM ** -0.5),
        grid=(bsz, n_kv, nq),
        in_specs=[q_spec, k_spec, v_spec, tq_spec, tq_spec, tq_spec, tk_spec, tk_spec, tk_spec,
                  gain_spec, gain_spec],
        out_specs=pl.BlockSpec((tq, gw), lambda b, kv, i: (b * nq + i, kv)),
        out_shape=jax.ShapeDtypeStruct((t, n_heads * HEAD_DIM), BF16),
        scratch_shapes=[pltpu.VMEM((seq, HEAD_DIM), BF16), pltpu.VMEM((seq, 2 * HEAD_DIM), BF16)],
        compiler_params=_params(3),
        name="gqa_attention",
    )(h, h, h, cos, up, dn, cos, up, dn, q_gain.reshape(1, HEAD_DIM), k_gain.reshape(1, HEAD_DIM))


def _ret_kernel(lg_ref, q_ref, k_ref, v_ref, g_ref, o_ref, state, o_acc, *, n_sub):
    d = pl.program_id(2)
    sc = pl.program_id(3)
    n_sc = pl.num_programs(3)
    c = RET_SCAN_CHUNK
    span = n_sub * c

    @pl.when(sc == 0)
    def _():
        state[...] = jnp.zeros_like(state)

    lg = lg_ref[0, 0][:1, :1]
    sgn = 1 - 2 * d
    dist = (lax.broadcasted_iota(I32, (c, c), 0) - lax.broadcasted_iota(I32, (c, c), 1)) * sgn
    intra = jnp.where(dist >= 0, jnp.exp(lg * jnp.maximum(dist, 0).astype(F32)), 0.0)
    pos = lax.broadcasted_iota(I32, (c, 1), 0)
    q_dec = jnp.exp(lg * jnp.where(d == 0, pos + 1, c - pos).astype(F32))
    k_dec = jnp.exp(lg * jnp.where(d == 0, c - 1 - pos, pos).astype(F32))
    chunk_dec = jnp.exp(lg * float(c))
    sci = sc + d * (n_sc - 1 - 2 * sc)

    for j in range(n_sub):
        cj = j + d * (n_sub - 1 - 2 * j)
        r0 = pl.multiple_of(cj * c, c)
        rows = pl.ds(r0, c)
        q = q_ref[rows, :]
        k = k_ref[rows, :]
        v = v_ref[rows, :]
        s = lax.dot_general(q, k, NT_DIMS, preferred_element_type=F32) * intra
        st = state[...]
        o = (jnp.dot(s.astype(BF16), v, preferred_element_type=F32)
             + q_dec * jnp.dot(q, st.astype(BF16), preferred_element_type=F32))
        kd = (k.astype(F32) * k_dec).astype(BF16)
        state[...] = st * chunk_dec + lax.dot_general(kd, v, TN_DIMS, preferred_element_type=F32)
        acc_rows = pl.ds(pl.multiple_of(sci * span + r0, c), c)

        @pl.when(d == 0)
        def _():
            o_acc[acc_rows, :] = o

        @pl.when(d == 1)
        def _():
            ot = o_acc[acc_rows, :] + o
            mu = jnp.mean(ot, axis=-1, keepdims=True)
            oc = ot - mu
            var = jnp.mean(oc * oc, axis=-1, keepdims=True)
            on = oc * lax.rsqrt(var + RET_GN_EPS)
            gate = g_ref[rows, :].astype(F32)
            o_ref[rows, :] = (gate / (1.0 + jnp.exp(-gate)) * on).astype(o_ref.dtype)


def _retention(h, lg_fwd, lg_bwd, bsz, seq, n_heads, dk, dv, *, span=1024):
    t = h.shape[0]
    span = _tile(seq, span)
    n_sub = span // RET_SCAN_CHUNK
    n_sc = seq // span
    lg = jnp.broadcast_to(jnp.stack([lg_fwd, lg_bwd])[:, :, None, None], (2, n_heads, 8, 128)).astype(F32)
    k_off = n_heads
    v_off = 2 * n_heads * dk // dv
    g_off = v_off + n_heads

    def blk(b, d, s):
        return b * n_sc + s + d * (n_sc - 1 - 2 * s)

    def held(b, d, s):
        return b * n_sc + jnp.where(d == 0, n_sc - 1, n_sc - 1 - s)

    return pl.pallas_call(
        functools.partial(_ret_kernel, n_sub=n_sub),
        grid=(bsz, n_heads, 2, n_sc),
        in_specs=[pl.BlockSpec((1, 1, 8, 128), lambda b, hh, d, s: (d, hh, 0, 0)),
                  pl.BlockSpec((span, dk), lambda b, hh, d, s: (blk(b, d, s), hh)),
                  pl.BlockSpec((span, dk), lambda b, hh, d, s: (blk(b, d, s), k_off + hh)),
                  pl.BlockSpec((span, dv), lambda b, hh, d, s: (blk(b, d, s), v_off + hh)),
                  pl.BlockSpec((span, dv), lambda b, hh, d, s: (held(b, d, s), g_off + hh))],
        out_specs=pl.BlockSpec((span, dv), lambda b, hh, d, s: (held(b, d, s), hh)),
        out_shape=jax.ShapeDtypeStruct((t, n_heads * dv), BF16),
        scratch_shapes=[pltpu.VMEM((dk, dv), F32), pltpu.VMEM((seq, dv), F32)],
        compiler_params=_params(4),
        name="retention",
    )(lg, h, h, h, h)


def _pack_bf16_pairs(y):
    half = y.shape[1] // 2
    bits = lax.bitcast_convert_type(y.astype(BF16).astype(F32), I32)
    return (bits[:, half:] & jnp.int32(-65536)) | lax.shift_right_logical(bits[:, :half], 16)


def _unpack_bf16_pairs(w):
    lo = lax.bitcast_convert_type(lax.shift_left(w, 16), F32).astype(BF16)
    hi = lax.bitcast_convert_type(w & jnp.int32(-65536), F32).astype(BF16)
    return lo, hi


def _proj_route_kernel(a_ref, w_ref, res_ref, g_ref, b_ref, rwh_ref, rwl_ref, rb_ref,
                       x_ref, xp_ref, idx_ref, gate_ref, rank_ref, cnt_ref, carry, *, alpha):
    @pl.when(pl.program_id(0) == 0)
    def _():
        carry[...] = jnp.zeros_like(carry)

    acc = jnp.dot(a_ref[...], w_ref[...], preferred_element_type=F32)
    y = _layer_norm(alpha * res_ref[...] + acc, g_ref[...], b_ref[...])
    x_ref[...] = y
    xp_ref[...] = _pack_bf16_pairs(y)

    yh = y.astype(BF16)
    yl = (y - yh.astype(F32)).astype(BF16)
    logits = (lax.dot_general(rwh_ref[...], yh, NT_DIMS, preferred_element_type=F32)
              + lax.dot_general(rwl_ref[...], yh, NT_DIMS, preferred_element_type=F32)
              + lax.dot_general(rwh_ref[...], yl, NT_DIMS, preferred_element_type=F32)
              + rb_ref[...])
    n_exp, tm = logits.shape
    eidx = lax.broadcasted_iota(I32, logits.shape, 0).astype(F32)
    vals, idxs = [], []
    rest = logits
    for _ in range(TOP_K):
        m = jnp.max(rest, axis=0, keepdims=True)
        am = jnp.min(jnp.where(rest == m, eidx, float(n_exp)), axis=0, keepdims=True)
        vals.append(m)
        idxs.append(am)
        rest = jnp.where(eidx == am, -jnp.inf, rest)
    exps = [jnp.exp(v - vals[0]) for v in vals]
    den = exps[0] + exps[1] + exps[2] + exps[3]
    hots = [eidx == am for am in idxs]
    multi = sum(h.astype(F32) for h in hots)

    before = (lax.broadcasted_iota(I32, (tm, tm), 0) < lax.broadcasted_iota(I32, (tm, tm), 1))
    prefix = jnp.dot(multi.astype(BF16), before.astype(BF16), preferred_element_type=F32) + carry[:, :1]
    for k in range(TOP_K):
        idx_ref[k:k + 1, :] = idxs[k].astype(I32)
        gate_ref[k:k + 1, :] = exps[k] / den
        rank_ref[k:k + 1, :] = jnp.sum(jnp.where(hots[k], prefix, 0.0), axis=0, keepdims=True).astype(I32)
    carry[...] = carry[...] + jnp.sum(multi, axis=1, keepdims=True)
    cnt_ref[...] = carry[...].astype(I32)


def _proj_route(a, w, resid, ln_g, ln_b, router_w, router_b, alpha, *, tm=256):
    t, k = a.shape
    d = w.shape[1]
    n_exp = router_w.shape[1]
    tm = _tile(t, tm)
    rwt = router_w.T
    rwh = rwt.astype(BF16)
    rwl = (rwt - rwh.astype(F32)).astype(BF16)
    row = lambda i: (i, 0)
    fixed = lambda i: (0, 0)
    col = lambda i: (0, i)
    outs = pl.pallas_call(
        functools.partial(_proj_route_kernel, alpha=alpha),
        grid=(t // tm,),
        in_specs=[pl.BlockSpec((tm, k), row),
                  pl.BlockSpec((k, d), fixed, pipeline_mode=pl.Buffered(1)),
                  pl.BlockSpec((tm, d), row),
                  pl.BlockSpec((1, d), fixed), pl.BlockSpec((1, d), fixed),
                  pl.BlockSpec((n_exp, d), fixed), pl.BlockSpec((n_exp, d), fixed),
                  pl.BlockSpec((n_exp, 1), fixed)],
        out_specs=[pl.BlockSpec((tm, d), row), pl.BlockSpec((tm, d // 2), row),
                   pl.BlockSpec((TOP_K, tm), col), pl.BlockSpec((TOP_K, tm), col),
                   pl.BlockSpec((TOP_K, tm), col), pl.BlockSpec((n_exp, 128), fixed)],
        out_shape=[jax.ShapeDtypeStruct((t, d), F32), jax.ShapeDtypeStruct((t, d // 2), I32),
                   jax.ShapeDtypeStruct((TOP_K, t), I32), jax.ShapeDtypeStruct((TOP_K, t), F32),
                   jax.ShapeDtypeStruct((TOP_K, t), I32), jax.ShapeDtypeStruct((n_exp, 128), I32)],
        scratch_shapes=[pltpu.VMEM((n_exp, 128), F32)],
        compiler_params=_params(1),
        name="proj_ln_route",
    )(a, w, resid, ln_g.reshape(1, d), ln_b.reshape(1, d), rwh, rwl, router_b.reshape(n_exp, 1))
    return outs


def _dispatch_kernel(dest_ref, cnt_ref, start_ref, nu_ref, xp_ref, xs_hbm, zblk, sem, *, n_tok_steps, n_exp):
    i = pl.program_id(0)
    tm = xp_ref.shape[0]
    n_blocks = xs_hbm.shape[0] // MOE_ROWS

    def row_copy(src, dst_row):
        return pltpu.make_async_copy(src, xs_hbm.at[pl.ds(dst_row, 1)], sem)

    def block_copy(b):
        return pltpu.make_async_copy(zblk, xs_hbm.at[pl.ds(pl.multiple_of(b * MOE_ROWS, MOE_ROWS), MOE_ROWS)], sem)

    @pl.when(i < n_tok_steps)
    def _():
        def issue(t, carry):
            for k in range(TOP_K):
                row_copy(xp_ref.at[pl.ds(t, 1)], dest_ref[k, t]).start(priority=k % 2)
            return carry

        lax.fori_loop(0, tm, issue, 0)
        for k in range(TOP_K):
            pltpu.make_async_copy(xp_ref, xs_hbm.at[pl.ds(0, tm)], sem).wait()

    @pl.when(i == n_tok_steps)
    def _():
        zblk[...] = jnp.zeros_like(zblk)
        zrow = zblk.at[pl.ds(0, 1)]

        def per_expert(e, carry):
            cnt = cnt_ref[e]
            n_pad = lax.rem(MOE_ROWS - lax.rem(cnt, MOE_ROWS), MOE_ROWS)
            first = start_ref[e] + cnt

            def issue(r, c2):
                row_copy(zrow, first + r).start()
                return c2

            def drain(r, c2):
                row_copy(zrow, 0).wait()
                return c2

            lax.fori_loop(0, n_pad, issue, 0)
            lax.fori_loop(0, n_pad, drain, 0)
            return carry

        lax.fori_loop(0, n_exp, per_expert, 0)

        def tail_issue(b, carry):
            block_copy(b).start()
            return carry

        def tail_drain(b, carry):
            block_copy(0).wait()
            return carry

        lax.fori_loop(nu_ref[0], n_blocks, tail_issue, 0)
        lax.fori_loop(nu_ref[0], n_blocks, tail_drain, 0)


def _dispatch(dest, counts, starts, n_used, xp, n_rows, *, tm=256):
    t, w = xp.shape
    tm = _tile(t, tm)
    n_steps = t // tm
    n_exp = counts.shape[0]
    smem = functools.partial(pl.BlockSpec, memory_space=pltpu.SMEM)
    return pl.pallas_call(
        functools.partial(_dispatch_kernel, n_tok_steps=n_steps, n_exp=n_exp),
        grid=(n_steps + 1,),
        in_specs=[smem((TOP_K, tm), lambda i: (0, jnp.minimum(i, n_steps - 1))),
                  smem(), smem(), smem(),
                  pl.BlockSpec((tm, w), lambda i: (jnp.minimum(i, n_steps - 1), 0))],
        out_specs=pl.BlockSpec(memory_space=pl.ANY),
        out_shape=jax.ShapeDtypeStruct((n_rows, w), I32),
        scratch_shapes=[pltpu.VMEM((MOE_ROWS, w), I32), pltpu.SemaphoreType.DMA(())],
        compiler_params=_params(1),
        name="moe_dispatch",
    )(dest, counts, starts, n_used, xp)


def _stream_expert_blocks(start_ref, nblk_ref, nu_ref, src_hbm, dst_hbm, dst_cols, inbuf, midbuf, outbuf,
                          insem, outsem, prepare, stage_a, stage_b=None):
    j, e = pl.program_id(0), pl.program_id(1)
    n_j, n_e = pl.num_programs(0), pl.num_programs(1)
    nb = nblk_ref[e]
    base = start_ref[e]

    def load_of(first_row, r, slot):
        rows = pl.ds(pl.multiple_of(first_row + r * MOE_ROWS, MOE_ROWS), MOE_ROWS)
        return pltpu.make_async_copy(src_hbm.at[rows], inbuf.at[slot], insem.at[slot])

    def load(r):
        return load_of(base, r, lax.rem(r, MOE_IN_BUFS))

    def store(r):
        slot = lax.rem(r, 2)
        rows = pl.ds(pl.multiple_of(base + r * MOE_ROWS, MOE_ROWS), MOE_ROWS)
        return pltpu.make_async_copy(outbuf.at[slot], dst_hbm.at[rows, dst_cols], outsem.at[slot])

    def take(r):
        @pl.when(r + MOE_IN_BUFS - 1 < nb)
        def _():
            load(r + MOE_IN_BUFS - 1).start(priority=XFER_PRIORITY)

        load(r).wait()

    def free_out(r):
        @pl.when(r >= 2)
        def _():
            store(r - 2).wait()

    @pl.when(nb > 0)
    def _():
        @pl.when((j == 0) & (e == 0))
        def _():
            load(0).start(priority=XFER_PRIORITY)

        for r in range(1, MOE_IN_BUFS - 1):
            @pl.when(r < nb)
            def _():
                load(r).start(priority=XFER_PRIORITY)

        prepare()

        if stage_b is None:
            def body(r, carry):
                take(r)
                free_out(r)
                stage_a(inbuf.at[lax.rem(r, MOE_IN_BUFS)], outbuf.at[lax.rem(r, 2)])
                store(r).start(priority=XFER_PRIORITY)
                return carry

            lax.fori_loop(0, nb, body, 0)
        else:
            take(0)
            stage_a(inbuf.at[0], midbuf)

            def body(r, carry):
                take(r)
                free_out(r - 1)
                stage_b(midbuf, outbuf.at[lax.rem(r - 1, 2)])
                stage_a(inbuf.at[lax.rem(r, MOE_IN_BUFS)], midbuf)
                store(r - 1).start(priority=XFER_PRIORITY)
                return carry

            lax.fori_loop(1, nb, body, 0)
            free_out(nb - 1)
            stage_b(midbuf, outbuf.at[lax.rem(nb - 1, 2)])
            store(nb - 1).start(priority=XFER_PRIORITY)

        @pl.when(nb >= 2)
        def _():
            store(nb - 2).wait()

        store(nb - 1).wait()

    e_next = jnp.where(e + 1 < n_e, e + 1, 0)
    is_last_step = (j == n_j - 1) & (e == n_e - 1)

    @pl.when(jnp.logical_not(is_last_step) & (nblk_ref[e_next] > 0))
    def _():
        load_of(start_ref[e_next], 0, 0).start(priority=XFER_PRIORITY)

    @pl.when(e == n_e - 1)
    def _():
        outbuf[0] = jnp.zeros(outbuf.shape[1:], outbuf.dtype)

        def tail(b):
            blk = pl.ds(pl.multiple_of(b * MOE_ROWS, MOE_ROWS), MOE_ROWS)
            return pltpu.make_async_copy(outbuf.at[0], dst_hbm.at[blk, dst_cols], outsem.at[0])

        def issue(b, carry):
            tail(b).start()
            return carry

        def drain(b, carry):
            tail(b).wait()
            return carry

        n_blocks = dst_hbm.shape[0] // MOE_ROWS
        lax.fori_loop(nu_ref[0], n_blocks, issue, 0)
        lax.fori_loop(nu_ref[0], n_blocks, drain, 0)


def _gate_up_kernel(start_ref, nblk_ref, nu_ref, w_ref, b_ref, xs_hbm, h_hbm, wbf, gubuf, xbuf, hbuf,
                    xsem, hsem):
    tq = hbuf.shape[2]
    cols = pl.ds(pl.multiple_of(pl.program_id(0) * tq, tq), tq)

    def prepare():
        wbf[...] = w_ref[0, 0].astype(BF16)

    def project(x_ref, gu_ref):
        lo, hi = _unpack_bf16_pairs(x_ref[...])
        half = lo.shape[1]
        gu_ref[...] = (jnp.dot(lo, wbf[:half, :], preferred_element_type=F32)
                       + jnp.dot(hi, wbf[half:, :], preferred_element_type=F32) + b_ref[0, 0])

    def activate(gu_ref, h_ref):
        even = (lax.broadcasted_iota(I32, (gu_ref.shape[0], 128), 1) & 1) == 0
        for c in range(tq // 128):
            a = gu_ref[:, c * 128:(c + 1) * 128]
            b = gu_ref[:, tq + c * 128:tq + (c + 1) * 128]
            gate = jnp.where(even, a, pltpu.roll(b, 1, 1))
            up = jnp.where(even, pltpu.roll(a, 127, 1), b)
            hg = jnp.minimum(gate, SWIGLU_LIMIT)
            hu = jnp.clip(up, -SWIGLU_LIMIT, SWIGLU_LIMIT)
            act = (hu + 1.0) * (hg / (1.0 + jnp.exp(-SWIGLU_ALPHA * hg)))
            h_ref[:, c * 128:(c + 1) * 128] = act.astype(h_ref.dtype)

    _stream_expert_blocks(start_ref, nblk_ref, nu_ref, xs_hbm, h_hbm, cols, xbuf, gubuf, hbuf, xsem, hsem,
                          prepare, project, activate)


def _expert_call(kernel_fn, layer, starts, nblk, n_used, w, b, src, out_shape, tn, out_w, scratch, name):
    n_exp, d_in, n_out = w.shape[1:]
    smem = pl.BlockSpec(memory_space=pltpu.SMEM)
    hbm = pl.BlockSpec(memory_space=pl.ANY)
    wsel = lambda j, e: (layer, e, 0, j)
    return pl.pallas_call(
        kernel_fn,
        grid=(n_out // tn, n_exp),
        in_specs=[smem, smem, smem,
                  pl.BlockSpec((1, 1, d_in, tn), wsel), pl.BlockSpec((1, 1, 1, tn), wsel), hbm],
        out_specs=hbm,
        out_shape=out_shape,
        scratch_shapes=scratch + [pltpu.VMEM((MOE_IN_BUFS, MOE_ROWS, src.shape[1]), src.dtype),
                                  pltpu.VMEM((2, MOE_ROWS, out_w), out_shape.dtype),
                                  pltpu.SemaphoreType.DMA((MOE_IN_BUFS,)), pltpu.SemaphoreType.DMA((2,))],
        compiler_params=_params(2),
        name=name,
    )(starts, nblk, n_used, w, b.reshape(b.shape[0], n_exp, 1, n_out), src)


def _gate_up(xs, w, b, layer, starts, nblk, n_used, *, tn):
    d, f2 = w.shape[2:]
    return _expert_call(_gate_up_kernel, layer, starts, nblk, n_used, w, b, xs,
                        jax.ShapeDtypeStruct((xs.shape[0], f2 // 2), BF16), tn, tn // 2,
                        [pltpu.VMEM((d, tn), BF16), pltpu.VMEM((MOE_ROWS, tn), F32)], "moe_gate_up")


def _down_kernel(start_ref, nblk_ref, nu_ref, w_ref, b_ref, h_hbm, y_hbm, wbf, stage, hbuf, ybuf,
                 hsem, ysem, *, group):
    tn = wbf.shape[1]
    cols = pl.ds(pl.multiple_of(pl.program_id(0) * (tn // 2), tn // 2), tn // 2)

    def prepare():
        half = group // 2
        for g0 in range(0, wbf.shape[0], group):
            for c in range(tn // 128):
                lanes = slice(c * 128, (c + 1) * 128)
                stage[c, pl.ds(0, half, stride=2), :] = w_ref[0, 0, g0:g0 + half, lanes]
                stage[c, pl.ds(1, half, stride=2), :] = w_ref[0, 0, g0 + half:g0 + group, lanes]
                wbf[g0:g0 + group, lanes] = stage[c].astype(BF16)

    def compute(h_ref, y_ref):
        y = jnp.dot(h_ref[...], wbf[...], preferred_element_type=F32) + b_ref[0, 0]
        y_ref[...] = _pack_bf16_pairs(y)

    _stream_expert_blocks(start_ref, nblk_ref, nu_ref, h_hbm, y_hbm, cols, hbuf, None, ybuf, hsem, ysem,
                          prepare, compute)


def _down(h, w, b, layer, starts, nblk, n_used, *, group, tn):
    f, d = w.shape[2:]
    return _expert_call(functools.partial(_down_kernel, group=group), layer, starts, nblk, n_used, w, b, h,
                        jax.ShapeDtypeStruct((h.shape[0], d // 2), I32), tn, tn // 2,
                        [pltpu.VMEM((f, tn), BF16), pltpu.VMEM((tn // 128, group, 128), F32)], "moe_down")


def _combine_kernel(dest_ref, next_ref, gate_ref, x_ref, g_ref, b_ref, y_hbm, o_ref, buf, sems, *,
                    alpha, pair_tile):
    i = pl.program_id(0)
    tm = x_ref.shape[0]
    slot = lax.rem(i, 2)

    def gather(d_ref, s):
        def issue(t, carry):
            for k in range(TOP_K):
                pltpu.make_async_copy(y_hbm.at[pl.ds(d_ref[k, t], 1)], buf.at[s, k, pl.ds(t, 1)],
                                      sems.at[s]).start(priority=k % 2)
            return carry

        lax.fori_loop(0, tm, issue, 0)

    @pl.when(i == 0)
    def _():
        gather(dest_ref, 0)

    @pl.when(i + 1 < pl.num_programs(0))
    def _():
        gather(next_ref, 1 - slot)

    for k in range(TOP_K):
        pltpu.make_async_copy(y_hbm.at[pl.ds(0, tm)], buf.at[slot, k], sems.at[slot]).wait()
    gates = gate_ref[...]
    lo = hi = None
    for k in range(TOP_K):
        w = buf[slot, k]
        g = gates[:, k:k + 1]
        lo_k = g * lax.bitcast_convert_type(lax.shift_left(w, 16), F32)
        hi_k = g * lax.bitcast_convert_type(w & jnp.int32(-65536), F32)
        lo = lo_k if lo is None else lo + lo_k
        hi = hi_k if hi is None else hi + hi_k
    half = pair_tile // 2
    parts = []
    for c0 in range(0, lo.shape[1], half):
        parts += [lo[:, c0:c0 + half], hi[:, c0:c0 + half]]
    ffn = jnp.concatenate(parts, axis=1)
    o_ref[...] = _layer_norm(alpha * x_ref[...] + ffn, g_ref[...], b_ref[...])


def _combine(dest, gates, x, y, ln_g, ln_b, alpha, *, pair_tile, tm=256):
    t, d = x.shape
    tm = _tile(t, tm)
    n_steps = t // tm
    row = lambda i: (i, 0)
    fixed = lambda i: (0, 0)
    return pl.pallas_call(
        functools.partial(_combine_kernel, alpha=alpha, pair_tile=pair_tile),
        grid=(n_steps,),
        in_specs=[pl.BlockSpec((TOP_K, tm), lambda i: (0, i), memory_space=pltpu.SMEM),
                  pl.BlockSpec((TOP_K, tm), lambda i: (0, jnp.minimum(i + 1, n_steps - 1)),
                               memory_space=pltpu.SMEM),
                  pl.BlockSpec((tm, TOP_K), row), pl.BlockSpec((tm, d), row),
                  pl.BlockSpec((1, d), fixed), pl.BlockSpec((1, d), fixed),
                  pl.BlockSpec(memory_space=pl.ANY)],
        out_specs=pl.BlockSpec((tm, d), row),
        out_shape=jax.ShapeDtypeStruct((t, d), F32),
        scratch_shapes=[pltpu.VMEM((2, TOP_K, tm, d // 2), I32), pltpu.SemaphoreType.DMA((2,))],
        compiler_params=_params(1),
        name="moe_combine_ln",
    )(dest, dest, gates, x, ln_g.reshape(1, d), ln_b.reshape(1, d), y)


def _moe(x, xp, idx_t, gate_t, rank_t, counts2d, layer, w_gate_up, b_gate_up, w_down, b_down, ln_g, ln_b, alpha):
    t, d = x.shape
    n_exp, f = w_down.shape[1], w_down.shape[2]
    n_blocks = t * TOP_K // MOE_ROWS + n_exp
    n_rows = n_blocks * MOE_ROWS

    counts = counts2d[:, 0]
    nblk = (counts + MOE_ROWS - 1) // MOE_ROWS
    pad_end = jnp.cumsum(nblk) * MOE_ROWS
    starts = (pad_end - nblk * MOE_ROWS).astype(I32)
    nu = (pad_end[-1:] // MOE_ROWS).astype(I32)
    experts = jnp.arange(n_exp, dtype=I32)
    dest = jnp.sum(jnp.where(idx_t[:, :, None] == experts, starts, 0), axis=-1) + rank_t

    xs = _dispatch(dest, counts, starts, nu, xp, n_rows)
    tn = _tile(2 * f, MOE_GATE_UP_TILE)
    h = _gate_up(xs, w_gate_up, b_gate_up, layer, starts, nblk, nu, tn=tn)
    tn_down = _tile(d, 1024)
    y = _down(h, w_down, b_down, layer, starts, nblk, nu, group=tn // 2, tn=tn_down)
    return _combine(dest, gate_t.T, x, y, ln_g, ln_b, alpha, pair_tile=tn_down)


def kernel(x, attn_w_in, attn_q_gain, attn_k_gain, attn_w_out, ret_w_in, ret_decay_fwd, ret_decay_bwd,
           ret_w_out, ln_mix_g, ln_mix_b, router_w, router_b, expert_w_gate_up, expert_b_gate_up,
           expert_w_down, expert_b_down, ln_ffn_g, ln_ffn_b):
    bsz, seq, d = x.shape
    depth = ln_mix_g.shape[0]
    alpha = (2 * depth) ** 0.25
    nq = attn_w_out.shape[1]
    n_heads = nq // HEAD_DIM
    n_kv = (attn_w_in.shape[2] - nq) // (2 * HEAD_DIM)
    ret_heads = ret_decay_fwd.shape[1]
    dv = ret_w_out.shape[1] // ret_heads
    dk = (ret_w_in.shape[2] - 2 * ret_heads * dv) // (2 * ret_heads)

    xf = x.reshape(bsz * seq, d)
    for i in range(depth):
        j = i // 2
        if i % 2 == 0:
            h = _matmul(xf, attn_w_in[j].astype(BF16))
            mix = _attention(h, attn_q_gain[j], attn_k_gain[j], bsz, seq, n_heads, n_kv)
            w_out = attn_w_out[j]
        else:
            h = _matmul_rope(xf, ret_w_in[j].astype(BF16), seq, ret_heads, dk)
            lg_f = jnp.log1p(-jnp.exp(ret_decay_fwd[j].astype(F32)))
            lg_b = jnp.log1p(-jnp.exp(ret_decay_bwd[j].astype(F32)))
            mix = _retention(h, lg_f, lg_b, bsz, seq, ret_heads, dk, dv)
            w_out = ret_w_out[j]
        x1, xp, idx_t, gate_t, rank_t, counts = _proj_route(
            mix, w_out.astype(BF16), xf, ln_mix_g[i], ln_mix_b[i], router_w[i], router_b[i], alpha)
        xf = _moe(x1, xp, idx_t, gate_t, rank_t, counts, i, expert_w_gate_up, expert_b_gate_up,
                  expert_w_down, expert_b_down, ln_ffn_g[i], ln_ffn_b[i], alpha)
    return xf.reshape(bsz, seq, d)
```

```python
import functools

import jax
import jax.numpy as jnp
from jax import lax
from jax.experimental import pallas as pl
from jax.experimental.pallas import tpu as pltpu

F32, BF16, I32 = jnp.float32, jnp.bfloat16, jnp.int32

HEAD_DIM = 128
GRID_W = 64
ROPE_THETA = 10000.0
TOP_K = 4
QK_NORM_EPS = 1e-6
RET_SCAN_CHUNK = 256
RET_GN_EPS = 1e-5
SWIGLU_LIMIT = 7.0
SWIGLU_ALPHA = 1.702
LN_EPS = 1e-5
LOG2_E = 1.4426950408889634

V7X_VMEM_LIMIT_BYTES = 56 * 1024 * 1024
MOE_ROWS = 256
MOE_GATE_UP_TILE = 1024
MOE_WEIGHT_PIECES = 16
MOE_IN_BUFS = 3
XFER_PRIORITY = 1

NT_DIMS = (((1,), (1,)), ((), ()))
TN_DIMS = (((0,), (0,)), ((), ()))


def _params(n_axes):
    return pltpu.CompilerParams(dimension_semantics=("arbitrary",) * n_axes,
                                vmem_limit_bytes=V7X_VMEM_LIMIT_BYTES)


def _tile(n, pref):
    t = min(n, pref)
    while n % t:
        t -= 128
    assert t > 0, (n, pref)
    return t


def _layer_norm(z, g, b):
    mu = jnp.mean(z, axis=-1, keepdims=True)
    zc = z - mu
    var = jnp.mean(zc * zc, axis=-1, keepdims=True)
    return zc * lax.rsqrt(var + LN_EPS) * g + b


def _rope_tables(seq, dim):
    rows = seq // GRID_W
    row_idx = jnp.repeat(jnp.arange(rows, dtype=F32), GRID_W)
    col_idx = jnp.tile(jnp.arange(GRID_W, dtype=F32), rows)
    quarter = dim // 4
    inv_freq = ROPE_THETA ** (-jnp.arange(quarter, dtype=F32) / quarter)
    ang_r = row_idx[:, None] * inv_freq[None, :]
    ang_c = col_idx[:, None] * inv_freq[None, :]
    ang = jnp.concatenate([ang_r, ang_r, ang_c, ang_c], axis=-1)
    cos, sin = jnp.cos(ang), jnp.sin(ang)
    first = (jnp.arange(dim) % (dim // 2)) < quarter
    return cos, jnp.where(first, -sin, 0.0), jnp.where(first, 0.0, sin)


def _rope(x, cos, sin_up, sin_dn):
    d = x.shape[-1]
    q = d // 4
    return x * cos + pltpu.roll(x, d - q, 1) * sin_up + pltpu.roll(x, q, 1) * sin_dn


def _matmul_kernel(x_ref, w_ref, o_ref):
    o_ref[...] = jnp.dot(x_ref[...].astype(BF16), w_ref[...],
                         preferred_element_type=F32).astype(o_ref.dtype)


def _matmul(x, w, *, tm=512, tn=1024):
    m, k = x.shape
    n = w.shape[1]
    tm, tn = _tile(m, tm), _tile(n, tn)
    return pl.pallas_call(
        _matmul_kernel,
        grid=(n // tn, m // tm),
        in_specs=[pl.BlockSpec((tm, k), lambda j, i: (i, 0)),
                  pl.BlockSpec((k, tn), lambda j, i: (0, j))],
        out_specs=pl.BlockSpec((tm, tn), lambda j, i: (i, j)),
        out_shape=jax.ShapeDtypeStruct((m, n), BF16),
        compiler_params=_params(2),
        name="dense_matmul",
    )(x, w)


def _matmul_rope_kernel(x_ref, w_ref, cos_ref, up_ref, dn_ref, o_ref, *, n_q_tiles, n_rope_tiles, k_scale):
    j = pl.program_id(0)
    acc = jnp.dot(x_ref[...].astype(BF16), w_ref[...], preferred_element_type=F32)

    @pl.when(j >= n_rope_tiles)
    def _():
        o_ref[...] = acc.astype(o_ref.dtype)

    @pl.when(j < n_rope_tiles)
    def _():
        scale = jnp.where(j >= n_q_tiles, k_scale, 1.0)
        dk = cos_ref.shape[1]
        cos, up, dn = cos_ref[...], up_ref[...], dn_ref[...]
        for hh in range(acc.shape[1] // dk):
            cols = slice(hh * dk, (hh + 1) * dk)
            o_ref[:, cols] = (_rope(acc[:, cols], cos, up, dn) * scale).astype(o_ref.dtype)


def _matmul_rope(x, w, seq, n_heads, dk, *, tm=512, tn=1024):
    m, k = x.shape
    n = w.shape[1]
    nqk = n_heads * dk
    tm, tn = _tile(seq, tm), _tile(nqk, tn)
    assert tn % dk == 0 and n % tn == 0
    cos, up, dn = _rope_tables(seq, dk)
    tab_spec = pl.BlockSpec((tm, dk), lambda j, i: (i % (seq // tm), 0))
    return pl.pallas_call(
        functools.partial(_matmul_rope_kernel, n_q_tiles=nqk // tn, n_rope_tiles=2 * nqk // tn,
                          k_scale=dk ** -0.5),
        grid=(n // tn, m // tm),
        in_specs=[pl.BlockSpec((tm, k), lambda j, i: (i, 0)),
                  pl.BlockSpec((k, tn), lambda j, i: (0, j)),
                  tab_spec, tab_spec, tab_spec],
        out_specs=pl.BlockSpec((tm, tn), lambda j, i: (i, j)),
        out_shape=jax.ShapeDtypeStruct((m, n), BF16),
        compiler_params=_params(2),
        name="dense_matmul_rope",
    )(x, w, cos, up, dn)


def _attn_kernel(q_ref, k_ref, v_ref, cq_ref, uq_ref, dq_ref, ck_ref, uk_ref, dk_ref,
                 qg_ref, kg_ref, o_ref, k_scr, v_scr, *, groups, scale):
    def norm_rope(x, gain, cos, up, dn):
        xn = x * lax.rsqrt(jnp.mean(x * x, axis=-1, keepdims=True) + QK_NORM_EPS) * gain
        return _rope(xn, cos, up, dn)

    @pl.when(pl.program_id(2) == 0)
    def _():
        k = k_ref[...].astype(F32)
        k_scr[...] = norm_rope(k, kg_ref[...], ck_ref[...], uk_ref[...], dk_ref[...]).astype(BF16)
        v_scr[:, :HEAD_DIM] = v_ref[...]
        v_scr[:, HEAD_DIM:] = jnp.ones((v_scr.shape[0], HEAD_DIM), BF16)

    cos, up, dn = cq_ref[...], uq_ref[...], dq_ref[...]
    for g in range(groups):
        cols = slice(g * HEAD_DIM, (g + 1) * HEAD_DIM)
        q = q_ref[:, cols].astype(F32)
        qr = (norm_rope(q, qg_ref[...], cos, up, dn) * (scale * LOG2_E)).astype(BF16)
        s = lax.dot_general(qr, k_scr[...], NT_DIMS, preferred_element_type=F32)
        p = jnp.exp2(s - jnp.max(s, axis=-1, keepdims=True))
        o = jnp.dot(p.astype(BF16), v_scr[...], preferred_element_type=F32)
        o_ref[:, cols] = (o[:, :HEAD_DIM] / o[:, HEAD_DIM:HEAD_DIM + 1]).astype(o_ref.dtype)


def _attention(h, q_gain, k_gain, bsz, seq, n_heads, n_kv, *, tq=256):
    t = h.shape[0]
    groups = n_heads // n_kv
    tq = _tile(seq, tq)
    nq = seq // tq
    cos, up, dn = _rope_tables(seq, HEAD_DIM)
    gw = groups * HEAD_DIM
    q_spec = pl.BlockSpec((tq, gw), lambda b, kv, i: (b * nq + i, kv))
    k_spec = pl.BlockSpec((seq, HEAD_DIM), lambda b, kv, i: (b, n_heads + kv))
    v_spec = pl.BlockSpec((seq, HEAD_DIM), lambda b, kv, i: (b, n_heads + n_kv + kv))
    tq_spec = pl.BlockSpec((tq, HEAD_DIM), lambda b, kv, i: (i, 0))
    tk_spec = pl.BlockSpec((seq, HEAD_DIM), lambda b, kv, i: (0, 0))
    gain_spec = pl.BlockSpec((1, HEAD_DIM), lambda b, kv, i: (0, 0))
    return pl.pallas_call(
        functools.partial(_attn_kernel, groups=groups, scale=HEAD_DIM ** -0.5),
        grid=(bsz, n_kv, nq),
        in_specs=[q_spec, k_spec, v_spec, tq_spec, tq_spec, tq_spec, tk_spec, tk_spec, tk_spec,
                  gain_spec, gain_spec],
        out_specs=pl.BlockSpec((tq, gw), lambda b, kv, i: (b * nq + i, kv)),
        out_shape=jax.ShapeDtypeStruct((t, n_heads * HEAD_DIM), BF16),
        scratch_shapes=[pltpu.VMEM((seq, HEAD_DIM), BF16), pltpu.VMEM((seq, 2 * HEAD_DIM), BF16)],
        compiler_params=_params(3),
        name="gqa_attention",
    )(h, h, h, cos, up, dn, cos, up, dn, q_gain.reshape(1, HEAD_DIM), k_gain.reshape(1, HEAD_DIM))


def _ret_kernel(lg_ref, q_ref, k_ref, v_ref, g_ref, o_ref, state, o_acc, *, n_sub):
    d = pl.program_id(2)
    sc = pl.program_id(3)
    n_sc = pl.num_programs(3)
    c = RET_SCAN_CHUNK
    span = n_sub * c

    @pl.when(sc == 0)
    def _():
        state[...] = jnp.zeros_like(state)

    lg = lg_ref[0, 0][:1, :1]
    sgn = 1 - 2 * d
    dist = (lax.broadcasted_iota(I32, (c, c), 0) - lax.broadcasted_iota(I32, (c, c), 1)) * sgn
    intra = jnp.where(dist >= 0, jnp.exp(lg * jnp.maximum(dist, 0).astype(F32)), 0.0)
    pos = lax.broadcasted_iota(I32, (c, 1), 0)
    q_dec = jnp.exp(lg * jnp.where(d == 0, pos + 1, c - pos).astype(F32))
    k_dec = jnp.exp(lg * jnp.where(d == 0, c - 1 - pos, pos).astype(F32))
    chunk_dec = jnp.exp(lg * float(c))
    sci = sc + d * (n_sc - 1 - 2 * sc)

    for j in range(n_sub):
        cj = j + d * (n_sub - 1 - 2 * j)
        r0 = pl.multiple_of(cj * c, c)
        rows = pl.ds(r0, c)
        q = q_ref[rows, :]
        k = k_ref[rows, :]
        v = v_ref[rows, :]
        s = lax.dot_general(q, k, NT_DIMS, preferred_element_type=F32) * intra
        st = state[...]
        o = (jnp.dot(s.astype(BF16), v, preferred_element_type=F32)
             + q_dec * jnp.dot(q, st.astype(BF16), preferred_element_type=F32))
        kd = (k.astype(F32) * k_dec).astype(BF16)
        state[...] = st * chunk_dec + lax.dot_general(kd, v, TN_DIMS, preferred_element_type=F32)
        acc_rows = pl.ds(pl.multiple_of(sci * span + r0, c), c)

        @pl.when(d == 0)
        def _():
            o_acc[acc_rows, :] = o

        @pl.when(d == 1)
        def _():
            ot = o_acc[acc_rows, :] + o
            mu = jnp.mean(ot, axis=-1, keepdims=True)
            oc = ot - mu
            var = jnp.mean(oc * oc, axis=-1, keepdims=True)
            on = oc * lax.rsqrt(var + RET_GN_EPS)
            gate = g_ref[rows, :].astype(F32)
            o_ref[rows, :] = (gate / (1.0 + jnp.exp(-gate)) * on).astype(o_ref.dtype)


def _retention(h, lg_fwd, lg_bwd, bsz, seq, n_heads, dk, dv, *, span=1024):
    t = h.shape[0]
    span = _tile(seq, span)
    n_sub = span // RET_SCAN_CHUNK
    n_sc = seq // span
    lg = jnp.broadcast_to(jnp.stack([lg_fwd, lg_bwd])[:, :, None, None], (2, n_heads, 8, 128)).astype(F32)
    k_off = n_heads
    v_off = 2 * n_heads * dk // dv
    g_off = v_off + n_heads

    def blk(b, d, s):
        return b * n_sc + s + d * (n_sc - 1 - 2 * s)

    def held(b, d, s):
        return b * n_sc + jnp.where(d == 0, n_sc - 1, n_sc - 1 - s)

    return pl.pallas_call(
        functools.partial(_ret_kernel, n_sub=n_sub),
        grid=(bsz, n_heads, 2, n_sc),
        in_specs=[pl.BlockSpec((1, 1, 8, 128), lambda b, hh, d, s: (d, hh, 0, 0)),
                  pl.BlockSpec((span, dk), lambda b, hh, d, s: (blk(b, d, s), hh)),
                  pl.BlockSpec((span, dk), lambda b, hh, d, s: (blk(b, d, s), k_off + hh)),
                  pl.BlockSpec((span, dv), lambda b, hh, d, s: (blk(b, d, s), v_off + hh)),
                  pl.BlockSpec((span, dv), lambda b, hh, d, s: (held(b, d, s), g_off + hh))],
        out_specs=pl.BlockSpec((span, dv), lambda b, hh, d, s: (held(b, d, s), hh)),
        out_shape=jax.ShapeDtypeStruct((t, n_heads * dv), BF16),
        scratch_shapes=[pltpu.VMEM((dk, dv), F32), pltpu.VMEM((seq, dv), F32)],
        compiler_params=_params(4),
        name="retention",
    )(lg, h, h, h, h)


def _pack_bf16_pairs(y):
    half = y.shape[1] // 2
    bits = lax.bitcast_convert_type(y.astype(BF16).astype(F32), I32)
    return (bits[:, half:] & jnp.int32(-65536)) | lax.shift_right_logical(bits[:, :half], 16)


def _unpack_bf16_pairs(w):
    lo = lax.bitcast_convert_type(lax.shift_left(w, 16), F32).astype(BF16)
    hi = lax.bitcast_convert_type(w & jnp.int32(-65536), F32).astype(BF16)
    return lo, hi


def _proj_route_kernel(a_ref, w_ref, res_ref, g_ref, b_ref, rwh_ref, rwl_ref, rb_ref,
                       x_ref, xp_ref, idx_ref, gate_ref, rank_ref, cnt_ref, carry, *, alpha):
    @pl.when(pl.program_id(0) == 0)
    def _():
        carry[...] = jnp.zeros_like(carry)

    acc = jnp.dot(a_ref[...], w_ref[...], preferred_element_type=F32)
    y = _layer_norm(alpha * res_ref[...] + acc, g_ref[...], b_ref[...])
    x_ref[...] = y
    xp_ref[...] = _pack_bf16_pairs(y)

    yh = y.astype(BF16)
    yl = (y - yh.astype(F32)).astype(BF16)
    logits = (lax.dot_general(rwh_ref[...], yh, NT_DIMS, preferred_element_type=F32)
              + lax.dot_general(rwl_ref[...], yh, NT_DIMS, preferred_element_type=F32)
              + lax.dot_general(rwh_ref[...], yl, NT_DIMS, preferred_element_type=F32)
              + rb_ref[...])
    n_exp, tm = logits.shape
    eidx = lax.broadcasted_iota(I32, logits.shape, 0).astype(F32)
    vals, idxs = [], []
    rest = logits
    for _ in range(TOP_K):
        m = jnp.max(rest, axis=0, keepdims=True)
        am = jnp.min(jnp.where(rest == m, eidx, float(n_exp)), axis=0, keepdims=True)
        vals.append(m)
        idxs.append(am)
        rest = jnp.where(eidx == am, -jnp.inf, rest)
    exps = [jnp.exp(v - vals[0]) for v in vals]
    den = exps[0] + exps[1] + exps[2] + exps[3]
    hots = [eidx == am for am in idxs]
    multi = sum(h.astype(F32) for h in hots)

    before = (lax.broadcasted_iota(I32, (tm, tm), 0) < lax.broadcasted_iota(I32, (tm, tm), 1))
    prefix = jnp.dot(multi.astype(BF16), before.astype(BF16), preferred_element_type=F32) + carry[:, :1]
    for k in range(TOP_K):
        idx_ref[k:k + 1, :] = idxs[k].astype(I32)
        gate_ref[k:k + 1, :] = exps[k] / den
        rank_ref[k:k + 1, :] = jnp.sum(jnp.where(hots[k], prefix, 0.0), axis=0, keepdims=True).astype(I32)
    carry[...] = carry[...] + jnp.sum(multi, axis=1, keepdims=True)
    cnt_ref[...] = carry[...].astype(I32)


def _proj_route(a, w, resid, ln_g, ln_b, router_w, router_b, alpha, *, tm=256):
    t, k = a.shape
    d = w.shape[1]
    n_exp = router_w.shape[1]
    tm = _tile(t, tm)
    rwt = router_w.T
    rwh = rwt.astype(BF16)
    rwl = (rwt - rwh.astype(F32)).astype(BF16)
    row = lambda i: (i, 0)
    fixed = lambda i: (0, 0)
    col = lambda i: (0, i)
    outs = pl.pallas_call(
        functools.partial(_proj_route_kernel, alpha=alpha),
        grid=(t // tm,),
        in_specs=[pl.BlockSpec((tm, k), row),
                  pl.BlockSpec((k, d), fixed, pipeline_mode=pl.Buffered(1)),
                  pl.BlockSpec((tm, d), row),
                  pl.BlockSpec((1, d), fixed), pl.BlockSpec((1, d), fixed),
                  pl.BlockSpec((n_exp, d), fixed), pl.BlockSpec((n_exp, d), fixed),
                  pl.BlockSpec((n_exp, 1), fixed)],
        out_specs=[pl.BlockSpec((tm, d), row), pl.BlockSpec((tm, d // 2), row),
                   pl.BlockSpec((TOP_K, tm), col), pl.BlockSpec((TOP_K, tm), col),
                   pl.BlockSpec((TOP_K, tm), col), pl.BlockSpec((n_exp, 128), fixed)],
        out_shape=[jax.ShapeDtypeStruct((t, d), F32), jax.ShapeDtypeStruct((t, d // 2), I32),
                   jax.ShapeDtypeStruct((TOP_K, t), I32), jax.ShapeDtypeStruct((TOP_K, t), F32),
                   jax.ShapeDtypeStruct((TOP_K, t), I32), jax.ShapeDtypeStruct((n_exp, 128), I32)],
        scratch_shapes=[pltpu.VMEM((n_exp, 128), F32)],
        compiler_params=_params(1),
        name="proj_ln_route",
    )(a, w, resid, ln_g.reshape(1, d), ln_b.reshape(1, d), rwh, rwl, router_b.reshape(n_exp, 1))
    return outs


def _dispatch_kernel(dest_ref, cnt_ref, start_ref, nu_ref, xp_ref, xs_hbm, zblk, sem, *, n_tok_steps, n_exp):
    i = pl.program_id(0)
    tm = xp_ref.shape[0]
    n_blocks = xs_hbm.shape[0] // MOE_ROWS

    def row_copy(src, dst_row):
        return pltpu.make_async_copy(src, xs_hbm.at[pl.ds(dst_row, 1)], sem)

    def block_copy(b):
        return pltpu.make_async_copy(zblk, xs_hbm.at[pl.ds(pl.multiple_of(b * MOE_ROWS, MOE_ROWS), MOE_ROWS)], sem)

    @pl.when(i < n_tok_steps)
    def _():
        def issue(t, carry):
            for k in range(TOP_K):
                row_copy(xp_ref.at[pl.ds(t, 1)], dest_ref[k, t]).start(priority=k % 2)
            return carry

        lax.fori_loop(0, tm, issue, 0)
        for k in range(TOP_K):
            pltpu.make_async_copy(xp_ref, xs_hbm.at[pl.ds(0, tm)], sem).wait()

    @pl.when(i == n_tok_steps)
    def _():
        zblk[...] = jnp.zeros_like(zblk)
        zrow = zblk.at[pl.ds(0, 1)]

        def per_expert(e, carry):
            cnt = cnt_ref[e]
            n_pad = lax.rem(MOE_ROWS - lax.rem(cnt, MOE_ROWS), MOE_ROWS)
            first = start_ref[e] + cnt

            def issue(r, c2):
                row_copy(zrow, first + r).start()
                return c2

            def drain(r, c2):
                row_copy(zrow, 0).wait()
                return c2

            lax.fori_loop(0, n_pad, issue, 0)
            lax.fori_loop(0, n_pad, drain, 0)
            return carry

        lax.fori_loop(0, n_exp, per_expert, 0)

        def tail_issue(b, carry):
            block_copy(b).start()
            return carry

        def tail_drain(b, carry):
            block_copy(0).wait()
            return carry

        lax.fori_loop(nu_ref[0], n_blocks, tail_issue, 0)
        lax.fori_loop(nu_ref[0], n_blocks, tail_drain, 0)


def _dispatch(dest, counts, starts, n_used, xp, n_rows, *, tm=256):
    t, w = xp.shape
    tm = _tile(t, tm)
    n_steps = t // tm
    n_exp = counts.shape[0]
    smem = functools.partial(pl.BlockSpec, memory_space=pltpu.SMEM)
    return pl.pallas_call(
        functools.partial(_dispatch_kernel, n_tok_steps=n_steps, n_exp=n_exp),
        grid=(n_steps + 1,),
        in_specs=[smem((TOP_K, tm), lambda i: (0, jnp.minimum(i, n_steps - 1))),
                  smem(), smem(), smem(),
                  pl.BlockSpec((tm, w), lambda i: (jnp.minimum(i, n_steps - 1), 0))],
        out_specs=pl.BlockSpec(memory_space=pl.ANY),
        out_shape=jax.ShapeDtypeStruct((n_rows, w), I32),
        scratch_shapes=[pltpu.VMEM((MOE_ROWS, w), I32), pltpu.SemaphoreType.DMA(())],
        compiler_params=_params(1),
        name="moe_dispatch",
    )(dest, counts, starts, n_used, xp)


def _stream_expert_blocks(layer, start_ref, nblk_ref, nu_ref, w_hbm, wraw, wsem, src_hbm, dst_hbm, dst_cols,
                          inbuf, midbuf, outbuf, insem, outsem, prepare, stage_a, stage_b=None):
    j, e = pl.program_id(0), pl.program_id(1)
    n_j, n_e = pl.num_programs(0), pl.num_programs(1)
    nb = nblk_ref[e]
    base = start_ref[e]
    e_next = jnp.where(e + 1 < n_e, e + 1, 0)
    j_next = jnp.where(e + 1 < n_e, j, j + 1)
    is_last_step = (j == n_j - 1) & (e == n_e - 1)
    wslot = lax.rem(j * n_e + e, 2)
    d_in, tn = wraw.shape[1:]
    piece = d_in // MOE_WEIGHT_PIECES

    def fetch_weights(jj, ee, slot):
        cols = pl.ds(pl.multiple_of(jj * tn, tn), tn)
        for p in range(MOE_WEIGHT_PIECES):
            rows = pl.ds(p * piece, piece)
            pltpu.make_async_copy(w_hbm.at[layer, ee, rows, cols], wraw.at[slot, rows], wsem.at[slot]).start()

    @pl.when((j == 0) & (e == 0) & (nb > 0))
    def _():
        fetch_weights(j, e, wslot)

    @pl.when(jnp.logical_not(is_last_step) & (nblk_ref[e_next] > 0))
    def _():
        fetch_weights(j_next, e_next, 1 - wslot)

    def load_of(first_row, r, slot):
        rows = pl.ds(pl.multiple_of(first_row + r * MOE_ROWS, MOE_ROWS), MOE_ROWS)
        return pltpu.make_async_copy(src_hbm.at[rows], inbuf.at[slot], insem.at[slot])

    def load(r):
        return load_of(base, r, lax.rem(r, MOE_IN_BUFS))

    def store(r):
        slot = lax.rem(r, 2)
        rows = pl.ds(pl.multiple_of(base + r * MOE_ROWS, MOE_ROWS), MOE_ROWS)
        return pltpu.make_async_copy(outbuf.at[slot], dst_hbm.at[rows, dst_cols], outsem.at[slot])

    def take(r):
        @pl.when(r + MOE_IN_BUFS - 1 < nb)
        def _():
            load(r + MOE_IN_BUFS - 1).start(priority=XFER_PRIORITY)

        load(r).wait()

    def free_out(r):
        @pl.when(r >= 2)
        def _():
            store(r - 2).wait()

    @pl.when(nb > 0)
    def _():
        @pl.when((j == 0) & (e == 0))
        def _():
            load(0).start(priority=XFER_PRIORITY)

        for r in range(1, MOE_IN_BUFS - 1):
            @pl.when(r < nb)
            def _():
                load(r).start(priority=XFER_PRIORITY)

        pltpu.make_async_copy(w_hbm.at[layer, 0, :, pl.ds(0, tn)], wraw.at[wslot], wsem.at[wslot]).wait()
        prepare(wraw.at[wslot])

        if stage_b is None:
            def body(r, carry):
                take(r)
                free_out(r)
                stage_a(inbuf.at[lax.rem(r, MOE_IN_BUFS)], outbuf.at[lax.rem(r, 2)])
                store(r).start(priority=XFER_PRIORITY)
                return carry

            lax.fori_loop(0, nb, body, 0)
        else:
            take(0)
            stage_a(inbuf.at[0], midbuf)

            def body(r, carry):
                take(r)
                free_out(r - 1)
                stage_b(midbuf, outbuf.at[lax.rem(r - 1, 2)])
                stage_a(inbuf.at[lax.rem(r, MOE_IN_BUFS)], midbuf)
                store(r - 1).start(priority=XFER_PRIORITY)
                return carry

            lax.fori_loop(1, nb, body, 0)
            free_out(nb - 1)
            stage_b(midbuf, outbuf.at[lax.rem(nb - 1, 2)])
            store(nb - 1).start(priority=XFER_PRIORITY)

        @pl.when(nb >= 2)
        def _():
            store(nb - 2).wait()

        store(nb - 1).wait()

    @pl.when(jnp.logical_not(is_last_step) & (nblk_ref[e_next] > 0))
    def _():
        load_of(start_ref[e_next], 0, 0).start(priority=XFER_PRIORITY)

    @pl.when(e == n_e - 1)
    def _():
        outbuf[0] = jnp.zeros(outbuf.shape[1:], outbuf.dtype)

        def tail(b):
            blk = pl.ds(pl.multiple_of(b * MOE_ROWS, MOE_ROWS), MOE_ROWS)
            return pltpu.make_async_copy(outbuf.at[0], dst_hbm.at[blk, dst_cols], outsem.at[0])

        def issue(b, carry):
            tail(b).start()
            return carry

        def drain(b, carry):
            tail(b).wait()
            return carry

        n_blocks = dst_hbm.shape[0] // MOE_ROWS
        lax.fori_loop(nu_ref[0], n_blocks, issue, 0)
        lax.fori_loop(nu_ref[0], n_blocks, drain, 0)


def _gate_up_kernel(start_ref, nblk_ref, nu_ref, w_hbm, b_ref, xs_hbm, h_hbm, wbf, gubuf, wraw, wsem,
                    xbuf, hbuf, xsem, hsem, *, layer):
    tq = hbuf.shape[2]
    cols = pl.ds(pl.multiple_of(pl.program_id(0) * tq, tq), tq)

    def prepare(w_ref):
        wbf[...] = w_ref[...].astype(BF16)

    def project(x_ref, gu_ref):
        lo, hi = _unpack_bf16_pairs(x_ref[...])
        half = lo.shape[1]
        gu_ref[...] = (jnp.dot(lo, wbf[:half, :], preferred_element_type=F32)
                       + jnp.dot(hi, wbf[half:, :], preferred_element_type=F32) + b_ref[0, 0])

    def activate(gu_ref, h_ref):
        even = (lax.broadcasted_iota(I32, (gu_ref.shape[0], 128), 1) & 1) == 0
        for c in range(tq // 128):
            a = gu_ref[:, c * 128:(c + 1) * 128]
            b = gu_ref[:, tq + c * 128:tq + (c + 1) * 128]
            gate = jnp.where(even, a, pltpu.roll(b, 1, 1))
            up = jnp.where(even, pltpu.roll(a, 127, 1), b)
            hg = jnp.minimum(gate, SWIGLU_LIMIT)
            hu = jnp.clip(up, -SWIGLU_LIMIT, SWIGLU_LIMIT)
            act = (hu + 1.0) * (hg / (1.0 + jnp.exp(-SWIGLU_ALPHA * hg)))
            h_ref[:, c * 128:(c + 1) * 128] = act.astype(h_ref.dtype)

    _stream_expert_blocks(layer, start_ref, nblk_ref, nu_ref, w_hbm, wraw, wsem, xs_hbm, h_hbm, cols,
                          xbuf, gubuf, hbuf, xsem, hsem, prepare, project, activate)


def _expert_call(kernel_fn, layer, starts, nblk, n_used, w, b, src, out_shape, tn, out_w, scratch, name):
    n_exp, d_in, n_out = w.shape[1:]
    assert d_in % (8 * MOE_WEIGHT_PIECES) == 0
    smem = pl.BlockSpec(memory_space=pltpu.SMEM)
    hbm = pl.BlockSpec(memory_space=pl.ANY)
    return pl.pallas_call(
        functools.partial(kernel_fn, layer=layer),
        grid=(n_out // tn, n_exp),
        in_specs=[smem, smem, smem, hbm,
                  pl.BlockSpec((1, 1, 1, tn), lambda j, e: (layer, e, 0, j)), hbm],
        out_specs=hbm,
        out_shape=out_shape,
        scratch_shapes=scratch + [pltpu.VMEM((2, d_in, tn), F32), pltpu.SemaphoreType.DMA((2,)),
                                  pltpu.VMEM((MOE_IN_BUFS, MOE_ROWS, src.shape[1]), src.dtype),
                                  pltpu.VMEM((2, MOE_ROWS, out_w), out_shape.dtype),
                                  pltpu.SemaphoreType.DMA((MOE_IN_BUFS,)), pltpu.SemaphoreType.DMA((2,))],
        compiler_params=_params(2),
        name=name,
    )(starts, nblk, n_used, w, b.reshape(b.shape[0], n_exp, 1, n_out), src)


def _gate_up(xs, w, b, layer, starts, nblk, n_used, *, tn):
    d, f2 = w.shape[2:]
    return _expert_call(_gate_up_kernel, layer, starts, nblk, n_used, w, b, xs,
                        jax.ShapeDtypeStruct((xs.shape[0], f2 // 2), BF16), tn, tn // 2,
                        [pltpu.VMEM((d, tn), BF16), pltpu.VMEM((MOE_ROWS, tn), F32)], "moe_gate_up")


def _down_kernel(start_ref, nblk_ref, nu_ref, w_hbm, b_ref, h_hbm, y_hbm, wbf, stage, wraw, wsem,
                 hbuf, ybuf, hsem, ysem, *, group, layer):
    tn = wbf.shape[1]
    cols = pl.ds(pl.multiple_of(pl.program_id(0) * (tn // 2), tn // 2), tn // 2)

    def prepare(w_ref):
        half = group // 2
        for g0 in range(0, wbf.shape[0], group):
            for c in range(tn // 128):
                lanes = slice(c * 128, (c + 1) * 128)
                stage[c, pl.ds(0, half, stride=2), :] = w_ref[g0:g0 + half, lanes]
                stage[c, pl.ds(1, half, stride=2), :] = w_ref[g0 + half:g0 + group, lanes]
                wbf[g0:g0 + group, lanes] = stage[c].astype(BF16)

    def compute(h_ref, y_ref):
        y = jnp.dot(h_ref[...], wbf[...], preferred_element_type=F32) + b_ref[0, 0]
        y_ref[...] = _pack_bf16_pairs(y)

    _stream_expert_blocks(layer, start_ref, nblk_ref, nu_ref, w_hbm, wraw, wsem, h_hbm, y_hbm, cols,
                          hbuf, None, ybuf, hsem, ysem, prepare, compute)


def _down(h, w, b, layer, starts, nblk, n_used, *, group, tn):
    f, d = w.shape[2:]
    return _expert_call(functools.partial(_down_kernel, group=group), layer, starts, nblk, n_used, w, b, h,
                        jax.ShapeDtypeStruct((h.shape[0], d // 2), I32), tn, tn // 2,
                        [pltpu.VMEM((f, tn), BF16), pltpu.VMEM((tn // 128, group, 128), F32)], "moe_down")


def _combine_kernel(dest_ref, next_ref, gate_ref, x_ref, g_ref, b_ref, y_hbm, o_ref, buf, sems, *,
                    alpha, pair_tile):
    i = pl.program_id(0)
    tm = x_ref.shape[0]
    slot = lax.rem(i, 2)

    def gather(d_ref, s):
        def issue(t, carry):
            for k in range(TOP_K):
                pltpu.make_async_copy(y_hbm.at[pl.ds(d_ref[k, t], 1)], buf.at[s, k, pl.ds(t, 1)],
                                      sems.at[s]).start(priority=k % 2)
            return carry

        lax.fori_loop(0, tm, issue, 0)

    @pl.when(i == 0)
    def _():
        gather(dest_ref, 0)

    @pl.when(i + 1 < pl.num_programs(0))
    def _():
        gather(next_ref, 1 - slot)

    for k in range(TOP_K):
        pltpu.make_async_copy(y_hbm.at[pl.ds(0, tm)], buf.at[slot, k], sems.at[slot]).wait()
    gates = gate_ref[...]
    lo = hi = None
    for k in range(TOP_K):
        w = buf[slot, k]
        g = gates[:, k:k + 1]
        lo_k = g * lax.bitcast_convert_type(lax.shift_left(w, 16), F32)
        hi_k = g * lax.bitcast_convert_type(w & jnp.int32(-65536), F32)
        lo = lo_k if lo is None else lo + lo_k
        hi = hi_k if hi is None else hi + hi_k
    half = pair_tile // 2
    parts = []
    for c0 in range(0, lo.shape[1], half):
        parts += [lo[:, c0:c0 + half], hi[:, c0:c0 + half]]
    ffn = jnp.concatenate(parts, axis=1)
    o_ref[...] = _layer_norm(alpha * x_ref[...] + ffn, g_ref[...], b_ref[...])


def _combine(dest, gates, x, y, ln_g, ln_b, alpha, *, pair_tile, tm=256):
    t, d = x.shape
    tm = _tile(t, tm)
    n_steps = t // tm
    row = lambda i: (i, 0)
    fixed = lambda i: (0, 0)
    return pl.pallas_call(
        functools.partial(_combine_kernel, alpha=alpha, pair_tile=pair_tile),
        grid=(n_steps,),
        in_specs=[pl.BlockSpec((TOP_K, tm), lambda i: (0, i), memory_space=pltpu.SMEM),
                  pl.BlockSpec((TOP_K, tm), lambda i: (0, jnp.minimum(i + 1, n_steps - 1)),
                               memory_space=pltpu.SMEM),
                  pl.BlockSpec((tm, TOP_K), row), pl.BlockSpec((tm, d), row),
                  pl.BlockSpec((1, d), fixed), pl.BlockSpec((1, d), fixed),
                  pl.BlockSpec(memory_space=pl.ANY)],
        out_specs=pl.BlockSpec((tm, d), row),
        out_shape=jax.ShapeDtypeStruct((t, d), F32),
        scratch_shapes=[pltpu.VMEM((2, TOP_K, tm, d // 2), I32), pltpu.SemaphoreType.DMA((2,))],
        compiler_params=_params(1),
        name="moe_combine_ln",
    )(dest, dest, gates, x, ln_g.reshape(1, d), ln_b.reshape(1, d), y)


def _moe(x, xp, idx_t, gate_t, rank_t, counts2d, layer, w_gate_up, b_gate_up, w_down, b_down, ln_g, ln_b, alpha):
    t, d = x.shape
    n_exp, f = w_down.shape[1], w_down.shape[2]
    n_blocks = t * TOP_K // MOE_ROWS + n_exp
    n_rows = n_blocks * MOE_ROWS

    counts = counts2d[:, 0]
    nblk = (counts + MOE_ROWS - 1) // MOE_ROWS
    pad_end = jnp.cumsum(nblk) * MOE_ROWS
    starts = (pad_end - nblk * MOE_ROWS).astype(I32)
    nu = (pad_end[-1:] // MOE_ROWS).astype(I32)
    experts = jnp.arange(n_exp, dtype=I32)
    dest = jnp.sum(jnp.where(idx_t[:, :, None] == experts, starts, 0), axis=-1) + rank_t

    xs = _dispatch(dest, counts, starts, nu, xp, n_rows)
    tn = _tile(2 * f, MOE_GATE_UP_TILE)
    h = _gate_up(xs, w_gate_up, b_gate_up, layer, starts, nblk, nu, tn=tn)
    tn_down = _tile(d, 1024)
    y = _down(h, w_down, b_down, layer, starts, nblk, nu, group=tn // 2, tn=tn_down)
    return _combine(dest, gate_t.T, x, y, ln_g, ln_b, alpha, pair_tile=tn_down)


def kernel(x, attn_w_in, attn_q_gain, attn_k_gain, attn_w_out, ret_w_in, ret_decay_fwd, ret_decay_bwd,
           ret_w_out, ln_mix_g, ln_mix_b, router_w, router_b, expert_w_gate_up, expert_b_gate_up,
           expert_w_down, expert_b_down, ln_ffn_g, ln_ffn_b):
    bsz, seq, d = x.shape
    depth = ln_mix_g.shape[0]
    alpha = (2 * depth) ** 0.25
    nq = attn_w_out.shape[1]
    n_heads = nq // HEAD_DIM
    n_kv = (attn_w_in.shape[2] - nq) // (2 * HEAD_DIM)
    ret_heads = ret_decay_fwd.shape[1]
    dv = ret_w_out.shape[1] // ret_heads
    dk = (ret_w_in.shape[2] - 2 * ret_heads * dv) // (2 * ret_heads)

    xf = x.reshape(bsz * seq, d)
    for i in range(depth):
        j = i // 2
        if i % 2 == 0:
            h = _matmul(xf, attn_w_in[j].astype(BF16))
            mix = _attention(h, attn_q_gain[j], attn_k_gain[j], bsz, seq, n_heads, n_kv)
            w_out = attn_w_out[j]
        else:
            h = _matmul_rope(xf, ret_w_in[j].astype(BF16), seq, ret_heads, dk)
            lg_f = jnp.log1p(-jnp.exp(ret_decay_fwd[j].astype(F32)))
            lg_b = jnp.log1p(-jnp.exp(ret_decay_bwd[j].astype(F32)))
            mix = _retention(h, lg_f, lg_b, bsz, seq, ret_heads, dk, dv)
            w_out = ret_w_out[j]
        x1, xp, idx_t, gate_t, rank_t, counts = _proj_route(
            mix, w_out.astype(BF16), xf, ln_mix_g[i], ln_mix_b[i], router_w[i], router_b[i], alpha)
        xf = _moe(x1, xp, idx_t, gate_t, rank_t, counts, i, expert_w_gate_up, expert_b_gate_up,
                  expert_w_down, expert_b_down, ln_ffn_g[i], ln_ffn_b[i], alpha)
    return xf.reshape(bsz, seq, d)
```

```python
import functools

import jax
import jax.numpy as jnp
from jax import lax
from jax.experimental import pallas as pl
from jax.experimental.pallas import tpu as pltpu

F32, BF16, I32 = jnp.float32, jnp.bfloat16, jnp.int32

HEAD_DIM = 128
GRID_W = 64
ROPE_THETA = 10000.0
TOP_K = 4
QK_NORM_EPS = 1e-6
RET_SCAN_CHUNK = 256
RET_GN_EPS = 1e-5
SWIGLU_LIMIT = 7.0
SWIGLU_ALPHA = 1.702
LN_EPS = 1e-5
LOG2_E = 1.4426950408889634

V7X_VMEM_LIMIT_BYTES = 56 * 1024 * 1024
MOE_ROWS = 256
MOE_GATE_UP_TILE = 1024
MOE_PAIR = 2
MOE_WEIGHT_PIECES = 16
MOE_IN_BUFS = 3
XFER_PRIORITY = 1

NT_DIMS = (((1,), (1,)), ((), ()))
TN_DIMS = (((0,), (0,)), ((), ()))


def _params(n_axes):
    return pltpu.CompilerParams(dimension_semantics=("arbitrary",) * n_axes,
                                vmem_limit_bytes=V7X_VMEM_LIMIT_BYTES)


def _tile(n, pref):
    t = min(n, pref)
    while n % t:
        t -= 128
    assert t > 0, (n, pref)
    return t


def _layer_norm(z, g, b):
    mu = jnp.mean(z, axis=-1, keepdims=True)
    zc = z - mu
    var = jnp.mean(zc * zc, axis=-1, keepdims=True)
    return zc * lax.rsqrt(var + LN_EPS) * g + b


def _rope_tables(seq, dim):
    rows = seq // GRID_W
    row_idx = jnp.repeat(jnp.arange(rows, dtype=F32), GRID_W)
    col_idx = jnp.tile(jnp.arange(GRID_W, dtype=F32), rows)
    quarter = dim // 4
    inv_freq = ROPE_THETA ** (-jnp.arange(quarter, dtype=F32) / quarter)
    ang_r = row_idx[:, None] * inv_freq[None, :]
    ang_c = col_idx[:, None] * inv_freq[None, :]
    ang = jnp.concatenate([ang_r, ang_r, ang_c, ang_c], axis=-1)
    cos, sin = jnp.cos(ang), jnp.sin(ang)
    first = (jnp.arange(dim) % (dim // 2)) < quarter
    return cos, jnp.where(first, -sin, 0.0), jnp.where(first, 0.0, sin)


def _rope(x, cos, sin_up, sin_dn):
    d = x.shape[-1]
    q = d // 4
    return x * cos + pltpu.roll(x, d - q, 1) * sin_up + pltpu.roll(x, q, 1) * sin_dn


def _matmul_kernel(x_ref, w_ref, o_ref):
    o_ref[...] = jnp.dot(x_ref[...].astype(BF16), w_ref[...],
                         preferred_element_type=F32).astype(o_ref.dtype)


def _matmul(x, w, *, tm=512, tn=1024):
    m, k = x.shape
    n = w.shape[1]
    tm, tn = _tile(m, tm), _tile(n, tn)
    return pl.pallas_call(
        _matmul_kernel,
        grid=(n // tn, m // tm),
        in_specs=[pl.BlockSpec((tm, k), lambda j, i: (i, 0)),
                  pl.BlockSpec((k, tn), lambda j, i: (0, j))],
        out_specs=pl.BlockSpec((tm, tn), lambda j, i: (i, j)),
        out_shape=jax.ShapeDtypeStruct((m, n), BF16),
        compiler_params=_params(2),
        name="dense_matmul",
    )(x, w)


def _matmul_rope_kernel(x_ref, w_ref, cos_ref, up_ref, dn_ref, o_ref, *, n_q_tiles, n_rope_tiles, k_scale):
    j = pl.program_id(0)
    acc = jnp.dot(x_ref[...].astype(BF16), w_ref[...], preferred_element_type=F32)

    @pl.when(j >= n_rope_tiles)
    def _():
        o_ref[...] = acc.astype(o_ref.dtype)

    @pl.when(j < n_rope_tiles)
    def _():
        scale = jnp.where(j >= n_q_tiles, k_scale, 1.0)
        dk = cos_ref.shape[1]
        cos, up, dn = cos_ref[...], up_ref[...], dn_ref[...]
        for hh in range(acc.shape[1] // dk):
            cols = slice(hh * dk, (hh + 1) * dk)
            o_ref[:, cols] = (_rope(acc[:, cols], cos, up, dn) * scale).astype(o_ref.dtype)


def _matmul_rope(x, w, seq, n_heads, dk, *, tm=512, tn=1024):
    m, k = x.shape
    n = w.shape[1]
    nqk = n_heads * dk
    tm, tn = _tile(seq, tm), _tile(nqk, tn)
    assert tn % dk == 0 and n % tn == 0
    cos, up, dn = _rope_tables(seq, dk)
    tab_spec = pl.BlockSpec((tm, dk), lambda j, i: (i % (seq // tm), 0))
    return pl.pallas_call(
        functools.partial(_matmul_rope_kernel, n_q_tiles=nqk // tn, n_rope_tiles=2 * nqk // tn,
                          k_scale=dk ** -0.5),
        grid=(n // tn, m // tm),
        in_specs=[pl.BlockSpec((tm, k), lambda j, i: (i, 0)),
                  pl.BlockSpec((k, tn), lambda j, i: (0, j)),
                  tab_spec, tab_spec, tab_spec],
        out_specs=pl.BlockSpec((tm, tn), lambda j, i: (i, j)),
        out_shape=jax.ShapeDtypeStruct((m, n), BF16),
        compiler_params=_params(2),
        name="dense_matmul_rope",
    )(x, w, cos, up, dn)


def _attn_kernel(q_ref, k_ref, v_ref, cq_ref, uq_ref, dq_ref, ck_ref, uk_ref, dk_ref,
                 qg_ref, kg_ref, o_ref, k_scr, v_scr, *, groups, scale):
    def norm_rope(x, gain, cos, up, dn):
        xn = x * lax.rsqrt(jnp.mean(x * x, axis=-1, keepdims=True) + QK_NORM_EPS) * gain
        return _rope(xn, cos, up, dn)

    @pl.when(pl.program_id(2) == 0)
    def _():
        k = k_ref[...].astype(F32)
        k_scr[...] = norm_rope(k, kg_ref[...], ck_ref[...], uk_ref[...], dk_ref[...]).astype(BF16)
        v_scr[:, :HEAD_DIM] = v_ref[...]
        v_scr[:, HEAD_DIM:] = jnp.ones((v_scr.shape[0], HEAD_DIM), BF16)

    cos, up, dn = cq_ref[...], uq_ref[...], dq_ref[...]
    for g in range(groups):
        cols = slice(g * HEAD_DIM, (g + 1) * HEAD_DIM)
        q = q_ref[:, cols].astype(F32)
        qr = (norm_rope(q, qg_ref[...], cos, up, dn) * (scale * LOG2_E)).astype(BF16)
        s = lax.dot_general(qr, k_scr[...], NT_DIMS, preferred_element_type=F32)
        p = jnp.exp2(s - jnp.max(s, axis=-1, keepdims=True))
        o = jnp.dot(p.astype(BF16), v_scr[...], preferred_element_type=F32)
        o_ref[:, cols] = (o[:, :HEAD_DIM] / o[:, HEAD_DIM:HEAD_DIM + 1]).astype(o_ref.dtype)


def _attention(h, q_gain, k_gain, bsz, seq, n_heads, n_kv, *, tq=256):
    t = h.shape[0]
    groups = n_heads // n_kv
    tq = _tile(seq, tq)
    nq = seq // tq
    cos, up, dn = _rope_tables(seq, HEAD_DIM)
    gw = groups * HEAD_DIM
    q_spec = pl.BlockSpec((tq, gw), lambda b, kv, i: (b * nq + i, kv))
    k_spec = pl.BlockSpec((seq, HEAD_DIM), lambda b, kv, i: (b, n_heads + kv))
    v_spec = pl.BlockSpec((seq, HEAD_DIM), lambda b, kv, i: (b, n_heads + n_kv + kv))
    tq_spec = pl.BlockSpec((tq, HEAD_DIM), lambda b, kv, i: (i, 0))
    tk_spec = pl.BlockSpec((seq, HEAD_DIM), lambda b, kv, i: (0, 0))
    gain_spec = pl.BlockSpec((1, HEAD_DIM), lambda b, kv, i: (0, 0))
    return pl.pallas_call(
        functools.partial(_attn_kernel, groups=groups, scale=HEAD_DIM ** -0.5),
        grid=(bsz, n_kv, nq),
        in_specs=[q_spec, k_spec, v_spec, tq_spec, tq_spec, tq_spec, tk_spec, tk_spec, tk_spec,
                  gain_spec, gain_spec],
        out_specs=pl.BlockSpec((tq, gw), lambda b, kv, i: (b * nq + i, kv)),
        out_shape=jax.ShapeDtypeStruct((t, n_heads * HEAD_DIM), BF16),
        scratch_shapes=[pltpu.VMEM((seq, HEAD_DIM), BF16), pltpu.VMEM((seq, 2 * HEAD_DIM), BF16)],
        compiler_params=_params(3),
        name="gqa_attention",
    )(h, h, h, cos, up, dn, cos, up, dn, q_gain.reshape(1, HEAD_DIM), k_gain.reshape(1, HEAD_DIM))


def _ret_kernel(lg_ref, q_ref, k_ref, v_ref, g_ref, o_ref, state, o_acc, *, n_sub):
    d = pl.program_id(2)
    sc = pl.program_id(3)
    n_sc = pl.num_programs(3)
    c = RET_SCAN_CHUNK
    span = n_sub * c

    @pl.when(sc == 0)
    def _():
        state[...] = jnp.zeros_like(state)

    lg = lg_ref[0, 0][:1, :1]
    sgn = 1 - 2 * d
    dist = (lax.broadcasted_iota(I32, (c, c), 0) - lax.broadcasted_iota(I32, (c, c), 1)) * sgn
    intra = jnp.where(dist >= 0, jnp.exp(lg * jnp.maximum(dist, 0).astype(F32)), 0.0)
    pos = lax.broadcasted_iota(I32, (c, 1), 0)
    q_dec = jnp.exp(lg * jnp.where(d == 0, pos + 1, c - pos).astype(F32))
    k_dec = jnp.exp(lg * jnp.where(d == 0, c - 1 - pos, pos).astype(F32))
    chunk_dec = jnp.exp(lg * float(c))
    sci = sc + d * (n_sc - 1 - 2 * sc)

    for j in range(n_sub):
        cj = j + d * (n_sub - 1 - 2 * j)
        r0 = pl.multiple_of(cj * c, c)
        rows = pl.ds(r0, c)
        q = q_ref[rows, :]
        k = k_ref[rows, :]
        v = v_ref[rows, :]
        s = lax.dot_general(q, k, NT_DIMS, preferred_element_type=F32) * intra
        st = state[...]
        o = (jnp.dot(s.astype(BF16), v, preferred_element_type=F32)
             + q_dec * jnp.dot(q, st.astype(BF16), preferred_element_type=F32))
        kd = (k.astype(F32) * k_dec).astype(BF16)
        state[...] = st * chunk_dec + lax.dot_general(kd, v, TN_DIMS, preferred_element_type=F32)
        acc_rows = pl.ds(pl.multiple_of(sci * span + r0, c), c)

        @pl.when(d == 0)
        def _():
            o_acc[acc_rows, :] = o

        @pl.when(d == 1)
        def _():
            ot = o_acc[acc_rows, :] + o
            mu = jnp.mean(ot, axis=-1, keepdims=True)
            oc = ot - mu
            var = jnp.mean(oc * oc, axis=-1, keepdims=True)
            on = oc * lax.rsqrt(var + RET_GN_EPS)
            gate = g_ref[rows, :].astype(F32)
            o_ref[rows, :] = (gate / (1.0 + jnp.exp(-gate)) * on).astype(o_ref.dtype)


def _retention(h, lg_fwd, lg_bwd, bsz, seq, n_heads, dk, dv, *, span=1024):
    t = h.shape[0]
    span = _tile(seq, span)
    n_sub = span // RET_SCAN_CHUNK
    n_sc = seq // span
    lg = jnp.broadcast_to(jnp.stack([lg_fwd, lg_bwd])[:, :, None, None], (2, n_heads, 8, 128)).astype(F32)
    k_off = n_heads
    v_off = 2 * n_heads * dk // dv
    g_off = v_off + n_heads

    def blk(b, d, s):
        return b * n_sc + s + d * (n_sc - 1 - 2 * s)

    def held(b, d, s):
        return b * n_sc + jnp.where(d == 0, n_sc - 1, n_sc - 1 - s)

    return pl.pallas_call(
        functools.partial(_ret_kernel, n_sub=n_sub),
        grid=(bsz, n_heads, 2, n_sc),
        in_specs=[pl.BlockSpec((1, 1, 8, 128), lambda b, hh, d, s: (d, hh, 0, 0)),
                  pl.BlockSpec((span, dk), lambda b, hh, d, s: (blk(b, d, s), hh)),
                  pl.BlockSpec((span, dk), lambda b, hh, d, s: (blk(b, d, s), k_off + hh)),
                  pl.BlockSpec((span, dv), lambda b, hh, d, s: (blk(b, d, s), v_off + hh)),
                  pl.BlockSpec((span, dv), lambda b, hh, d, s: (held(b, d, s), g_off + hh))],
        out_specs=pl.BlockSpec((span, dv), lambda b, hh, d, s: (held(b, d, s), hh)),
        out_shape=jax.ShapeDtypeStruct((t, n_heads * dv), BF16),
        scratch_shapes=[pltpu.VMEM((dk, dv), F32), pltpu.VMEM((seq, dv), F32)],
        compiler_params=_params(4),
        name="retention",
    )(lg, h, h, h, h)


def _pack_bf16_pairs(y):
    half = y.shape[1] // 2
    bits = lax.bitcast_convert_type(y.astype(BF16).astype(F32), I32)
    return (bits[:, half:] & jnp.int32(-65536)) | lax.shift_right_logical(bits[:, :half], 16)


def _unpack_bf16_pairs(w):
    lo = lax.bitcast_convert_type(lax.shift_left(w, 16), F32).astype(BF16)
    hi = lax.bitcast_convert_type(w & jnp.int32(-65536), F32).astype(BF16)
    return lo, hi


def _proj_route_kernel(a_ref, w_ref, res_ref, g_ref, b_ref, rwh_ref, rwl_ref, rb_ref,
                       x_ref, xp_ref, idx_ref, gate_ref, rank_ref, cnt_ref, carry, *, alpha):
    @pl.when(pl.program_id(0) == 0)
    def _():
        carry[...] = jnp.zeros_like(carry)

    acc = jnp.dot(a_ref[...], w_ref[...], preferred_element_type=F32)
    y = _layer_norm(alpha * res_ref[...] + acc, g_ref[...], b_ref[...])
    x_ref[...] = y
    xp_ref[...] = _pack_bf16_pairs(y)

    yh = y.astype(BF16)
    yl = (y - yh.astype(F32)).astype(BF16)
    logits = (lax.dot_general(rwh_ref[...], yh, NT_DIMS, preferred_element_type=F32)
              + lax.dot_general(rwl_ref[...], yh, NT_DIMS, preferred_element_type=F32)
              + lax.dot_general(rwh_ref[...], yl, NT_DIMS, preferred_element_type=F32)
              + rb_ref[...])
    n_exp, tm = logits.shape
    eidx = lax.broadcasted_iota(I32, logits.shape, 0).astype(F32)
    vals, idxs = [], []
    rest = logits
    for _ in range(TOP_K):
        m = jnp.max(rest, axis=0, keepdims=True)
        am = jnp.min(jnp.where(rest == m, eidx, float(n_exp)), axis=0, keepdims=True)
        vals.append(m)
        idxs.append(am)
        rest = jnp.where(eidx == am, -jnp.inf, rest)
    exps = [jnp.exp(v - vals[0]) for v in vals]
    den = exps[0] + exps[1] + exps[2] + exps[3]
    hots = [eidx == am for am in idxs]
    multi = sum(h.astype(F32) for h in hots)

    before = (lax.broadcasted_iota(I32, (tm, tm), 0) < lax.broadcasted_iota(I32, (tm, tm), 1))
    prefix = jnp.dot(multi.astype(BF16), before.astype(BF16), preferred_element_type=F32) + carry[:, :1]
    for k in range(TOP_K):
        idx_ref[k:k + 1, :] = idxs[k].astype(I32)
        gate_ref[k:k + 1, :] = exps[k] / den
        rank_ref[k:k + 1, :] = jnp.sum(jnp.where(hots[k], prefix, 0.0), axis=0, keepdims=True).astype(I32)
    carry[...] = carry[...] + jnp.sum(multi, axis=1, keepdims=True)
    cnt_ref[...] = carry[...].astype(I32)


def _proj_route(a, w, resid, ln_g, ln_b, router_w, router_b, alpha, *, tm=256):
    t, k = a.shape
    d = w.shape[1]
    n_exp = router_w.shape[1]
    tm = _tile(t, tm)
    rwt = router_w.T
    rwh = rwt.astype(BF16)
    rwl = (rwt - rwh.astype(F32)).astype(BF16)
    row = lambda i: (i, 0)
    fixed = lambda i: (0, 0)
    col = lambda i: (0, i)
    outs = pl.pallas_call(
        functools.partial(_proj_route_kernel, alpha=alpha),
        grid=(t // tm,),
        in_specs=[pl.BlockSpec((tm, k), row),
                  pl.BlockSpec((k, d), fixed, pipeline_mode=pl.Buffered(1)),
                  pl.BlockSpec((tm, d), row),
                  pl.BlockSpec((1, d), fixed), pl.BlockSpec((1, d), fixed),
                  pl.BlockSpec((n_exp, d), fixed), pl.BlockSpec((n_exp, d), fixed),
                  pl.BlockSpec((n_exp, 1), fixed)],
        out_specs=[pl.BlockSpec((tm, d), row), pl.BlockSpec((tm, d // 2), row),
                   pl.BlockSpec((TOP_K, tm), col), pl.BlockSpec((TOP_K, tm), col),
                   pl.BlockSpec((TOP_K, tm), col), pl.BlockSpec((n_exp, 128), fixed)],
        out_shape=[jax.ShapeDtypeStruct((t, d), F32), jax.ShapeDtypeStruct((t, d // 2), I32),
                   jax.ShapeDtypeStruct((TOP_K, t), I32), jax.ShapeDtypeStruct((TOP_K, t), F32),
                   jax.ShapeDtypeStruct((TOP_K, t), I32), jax.ShapeDtypeStruct((n_exp, 128), I32)],
        scratch_shapes=[pltpu.VMEM((n_exp, 128), F32)],
        compiler_params=_params(1),
        name="proj_ln_route",
    )(a, w, resid, ln_g.reshape(1, d), ln_b.reshape(1, d), rwh, rwl, router_b.reshape(n_exp, 1))
    return outs


def _dispatch_kernel(dest_ref, cnt_ref, start_ref, nu_ref, xp_ref, xs_hbm, zblk, sem, *, n_tok_steps, n_exp):
    i = pl.program_id(0)
    tm = xp_ref.shape[0]
    n_blocks = xs_hbm.shape[0] // MOE_ROWS

    def row_copy(src, dst_row):
        return pltpu.make_async_copy(src, xs_hbm.at[pl.ds(dst_row, 1)], sem)

    def block_copy(b):
        return pltpu.make_async_copy(zblk, xs_hbm.at[pl.ds(pl.multiple_of(b * MOE_ROWS, MOE_ROWS), MOE_ROWS)], sem)

    @pl.when(i < n_tok_steps)
    def _():
        def issue(t, carry):
            for k in range(TOP_K):
                row_copy(xp_ref.at[pl.ds(t, 1)], dest_ref[k, t]).start(priority=k % 2)
            return carry

        lax.fori_loop(0, tm, issue, 0)
        for k in range(TOP_K):
            pltpu.make_async_copy(xp_ref, xs_hbm.at[pl.ds(0, tm)], sem).wait()

    @pl.when(i == n_tok_steps)
    def _():
        zblk[...] = jnp.zeros_like(zblk)
        zrow = zblk.at[pl.ds(0, 1)]

        def per_expert(e, carry):
            cnt = cnt_ref[e]
            n_pad = lax.rem(MOE_ROWS - lax.rem(cnt, MOE_ROWS), MOE_ROWS)
            first = start_ref[e] + cnt

            def issue(r, c2):
                row_copy(zrow, first + r).start()
                return c2

            def drain(r, c2):
                row_copy(zrow, 0).wait()
                return c2

            lax.fori_loop(0, n_pad, issue, 0)
            lax.fori_loop(0, n_pad, drain, 0)
            return carry

        lax.fori_loop(0, n_exp, per_expert, 0)

        def tail_issue(b, carry):
            block_copy(b).start()
            return carry

        def tail_drain(b, carry):
            block_copy(0).wait()
            return carry

        lax.fori_loop(nu_ref[0], n_blocks, tail_issue, 0)
        lax.fori_loop(nu_ref[0], n_blocks, tail_drain, 0)


def _dispatch(dest, counts, starts, n_used, xp, n_rows, *, tm=256):
    t, w = xp.shape
    tm = _tile(t, tm)
    n_steps = t // tm
    n_exp = counts.shape[0]
    smem = functools.partial(pl.BlockSpec, memory_space=pltpu.SMEM)
    return pl.pallas_call(
        functools.partial(_dispatch_kernel, n_tok_steps=n_steps, n_exp=n_exp),
        grid=(n_steps + 1,),
        in_specs=[smem((TOP_K, tm), lambda i: (0, jnp.minimum(i, n_steps - 1))),
                  smem(), smem(), smem(),
                  pl.BlockSpec((tm, w), lambda i: (jnp.minimum(i, n_steps - 1), 0))],
        out_specs=pl.BlockSpec(memory_space=pl.ANY),
        out_shape=jax.ShapeDtypeStruct((n_rows, w), I32),
        scratch_shapes=[pltpu.VMEM((MOE_ROWS, w), I32), pltpu.SemaphoreType.DMA(())],
        compiler_params=_params(1),
        name="moe_dispatch",
    )(dest, counts, starts, n_used, xp)


def _stream_expert_blocks(layer, start_ref, nblk_ref, nu_ref, w_hbm, wraw, wsem, src_hbm, dst_hbm, dst_cols,
                          inbuf, midbuf, outbuf, insem, outsem, prepare, stage_a, stage_b=None):
    j, e = pl.program_id(0), pl.program_id(1)
    n_j, n_e = pl.num_programs(0), pl.num_programs(1)
    nb = nblk_ref[e]
    base = start_ref[e]
    e_next = jnp.where(e + 1 < n_e, e + 1, 0)
    j_next = jnp.where(e + 1 < n_e, j, j + 1)
    is_last_step = (j == n_j - 1) & (e == n_e - 1)
    wslot = lax.rem(j * n_e + e, 2)
    d_in, tn = wraw.shape[1:]
    piece = d_in // MOE_WEIGHT_PIECES

    def fetch_weights(jj, ee, slot):
        cols = pl.ds(pl.multiple_of(jj * tn, tn), tn)
        for p in range(MOE_WEIGHT_PIECES):
            rows = pl.ds(p * piece, piece)
            pltpu.make_async_copy(w_hbm.at[layer, ee, rows, cols], wraw.at[slot, rows], wsem.at[slot]).start()

    @pl.when((j == 0) & (e == 0) & (nb > 0))
    def _():
        fetch_weights(j, e, wslot)

    @pl.when(jnp.logical_not(is_last_step) & (nblk_ref[e_next] > 0))
    def _():
        fetch_weights(j_next, e_next, 1 - wslot)

    step_rows = MOE_PAIR * MOE_ROWS
    n_it = lax.div(nb + MOE_PAIR - 1, MOE_PAIR)

    def load_of(first_row, r, slot):
        rows = pl.ds(pl.multiple_of(first_row + r * step_rows, MOE_ROWS), step_rows)
        return pltpu.make_async_copy(src_hbm.at[rows], inbuf.at[slot], insem.at[slot])

    def load(r):
        return load_of(base, r, lax.rem(r, MOE_IN_BUFS))

    def store_half(r, hh):
        slot = lax.rem(r, 2)
        rows = pl.ds(pl.multiple_of(base + r * step_rows + hh * MOE_ROWS, MOE_ROWS), MOE_ROWS)
        return pltpu.make_async_copy(outbuf.at[slot, pl.ds(hh * MOE_ROWS, MOE_ROWS)],
                                     dst_hbm.at[rows, dst_cols], outsem.at[slot])

    class _Stores:
        def __init__(self, r):
            self.r = r

        def _each(self, fn):
            fn(store_half(self.r, 0))
            for hh in range(1, MOE_PAIR):
                @pl.when(MOE_PAIR * self.r + hh < nb)
                def _():
                    fn(store_half(self.r, hh))

        def start(self, priority):
            self._each(lambda c: c.start(priority=priority))

        def wait(self):
            self._each(lambda c: c.wait())

    store = _Stores

    def take(r):
        @pl.when(r + MOE_IN_BUFS - 1 < n_it)
        def _():
            load(r + MOE_IN_BUFS - 1).start(priority=XFER_PRIORITY)

        load(r).wait()

    def free_out(r):
        @pl.when(r >= 2)
        def _():
            store(r - 2).wait()

    @pl.when(nb > 0)
    def _():
        @pl.when((j == 0) & (e == 0))
        def _():
            load(0).start(priority=XFER_PRIORITY)

        for r in range(1, MOE_IN_BUFS - 1):
            @pl.when(r < n_it)
            def _():
                load(r).start(priority=XFER_PRIORITY)

        pltpu.make_async_copy(w_hbm.at[layer, 0, :, pl.ds(0, tn)], wraw.at[wslot], wsem.at[wslot]).wait()
        prepare(wraw.at[wslot])

        if stage_b is None:
            def body(r, carry):
                take(r)
                free_out(r)
                stage_a(inbuf.at[lax.rem(r, MOE_IN_BUFS)], outbuf.at[lax.rem(r, 2)])
                store(r).start(priority=XFER_PRIORITY)
                return carry

            lax.fori_loop(0, n_it, body, 0)
        else:
            take(0)
            stage_a(inbuf.at[0], midbuf)

            def body(r, carry):
                take(r)
                free_out(r - 1)
                stage_b(midbuf, outbuf.at[lax.rem(r - 1, 2)])
                stage_a(inbuf.at[lax.rem(r, MOE_IN_BUFS)], midbuf)
                store(r - 1).start(priority=XFER_PRIORITY)
                return carry

            lax.fori_loop(1, n_it, body, 0)
            free_out(n_it - 1)
            stage_b(midbuf, outbuf.at[lax.rem(n_it - 1, 2)])
            store(n_it - 1).start(priority=XFER_PRIORITY)

        @pl.when(n_it >= 2)
        def _():
            store(n_it - 2).wait()

        store(n_it - 1).wait()

    @pl.when(jnp.logical_not(is_last_step) & (nblk_ref[e_next] > 0))
    def _():
        load_of(start_ref[e_next], 0, 0).start(priority=XFER_PRIORITY)

    @pl.when(e == n_e - 1)
    def _():
        outbuf[0] = jnp.zeros(outbuf.shape[1:], outbuf.dtype)

        def tail(b):
            blk = pl.ds(pl.multiple_of(b * MOE_ROWS, MOE_ROWS), MOE_ROWS)
            return pltpu.make_async_copy(outbuf.at[0, pl.ds(0, MOE_ROWS)], dst_hbm.at[blk, dst_cols],
                                         outsem.at[0])

        def issue(b, carry):
            tail(b).start()
            return carry

        def drain(b, carry):
            tail(b).wait()
            return carry

        n_blocks = dst_hbm.shape[0] // MOE_ROWS
        lax.fori_loop(nu_ref[0], n_blocks, issue, 0)
        lax.fori_loop(nu_ref[0], n_blocks, drain, 0)


def _gate_up_kernel(start_ref, nblk_ref, nu_ref, w_hbm, b_ref, xs_hbm, h_hbm, wbf, gubuf, wraw, wsem,
                    xbuf, hbuf, xsem, hsem, *, layer):
    tq = hbuf.shape[2]
    cols = pl.ds(pl.multiple_of(pl.program_id(0) * tq, tq), tq)

    def prepare(w_ref):
        wbf[...] = w_ref[...].astype(BF16)

    def project(x_ref, gu_ref):
        lo, hi = _unpack_bf16_pairs(x_ref[...])
        half = lo.shape[1]
        gu_ref[...] = (jnp.dot(lo, wbf[:half, :], preferred_element_type=F32)
                       + jnp.dot(hi, wbf[half:, :], preferred_element_type=F32) + b_ref[0, 0])

    def activate(gu_ref, h_ref):
        even = (lax.broadcasted_iota(I32, (gu_ref.shape[0], 128), 1) & 1) == 0
        for c in range(tq // 128):
            a = gu_ref[:, c * 128:(c + 1) * 128]
            b = gu_ref[:, tq + c * 128:tq + (c + 1) * 128]
            gate = jnp.where(even, a, pltpu.roll(b, 1, 1))
            up = jnp.where(even, pltpu.roll(a, 127, 1), b)
            hg = jnp.minimum(gate, SWIGLU_LIMIT)
            hu = jnp.clip(up, -SWIGLU_LIMIT, SWIGLU_LIMIT)
            act = (hu + 1.0) * (hg / (1.0 + jnp.exp(-SWIGLU_ALPHA * hg)))
            h_ref[:, c * 128:(c + 1) * 128] = act.astype(h_ref.dtype)

    _stream_expert_blocks(layer, start_ref, nblk_ref, nu_ref, w_hbm, wraw, wsem, xs_hbm, h_hbm, cols,
                          xbuf, gubuf, hbuf, xsem, hsem, prepare, project, activate)


def _expert_call(kernel_fn, layer, starts, nblk, n_used, w, b, src, out_shape, tn, out_w, scratch, name):
    n_exp, d_in, n_out = w.shape[1:]
    assert d_in % (8 * MOE_WEIGHT_PIECES) == 0
    smem = pl.BlockSpec(memory_space=pltpu.SMEM)
    hbm = pl.BlockSpec(memory_space=pl.ANY)
    return pl.pallas_call(
        functools.partial(kernel_fn, layer=layer),
        grid=(n_out // tn, n_exp),
        in_specs=[smem, smem, smem, hbm,
                  pl.BlockSpec((1, 1, 1, tn), lambda j, e: (layer, e, 0, j)), hbm],
        out_specs=hbm,
        out_shape=out_shape,
        scratch_shapes=scratch + [pltpu.VMEM((2, d_in, tn), F32), pltpu.SemaphoreType.DMA((2,)),
                                  pltpu.VMEM((MOE_IN_BUFS, MOE_PAIR * MOE_ROWS, src.shape[1]), src.dtype),
                                  pltpu.VMEM((2, MOE_PAIR * MOE_ROWS, out_w), out_shape.dtype),
                                  pltpu.SemaphoreType.DMA((MOE_IN_BUFS,)), pltpu.SemaphoreType.DMA((2,))],
        compiler_params=_params(2),
        name=name,
    )(starts, nblk, n_used, w, b.reshape(b.shape[0], n_exp, 1, n_out), src)


def _gate_up(xs, w, b, layer, starts, nblk, n_used, *, tn):
    d, f2 = w.shape[2:]
    return _expert_call(_gate_up_kernel, layer, starts, nblk, n_used, w, b, xs,
                        jax.ShapeDtypeStruct((xs.shape[0], f2 // 2), BF16), tn, tn // 2,
                        [pltpu.VMEM((d, tn), BF16), pltpu.VMEM((MOE_PAIR * MOE_ROWS, tn), F32)], "moe_gate_up")


def _down_kernel(start_ref, nblk_ref, nu_ref, w_hbm, b_ref, h_hbm, y_hbm, wbf, stage, wraw, wsem,
                 hbuf, ybuf, hsem, ysem, *, group, layer):
    tn = wbf.shape[1]
    cols = pl.ds(pl.multiple_of(pl.program_id(0) * (tn // 2), tn // 2), tn // 2)

    def prepare(w_ref):
        half = group // 2
        for g0 in range(0, wbf.shape[0], group):
            for c in range(tn // 128):
                lanes = slice(c * 128, (c + 1) * 128)
                stage[c, pl.ds(0, half, stride=2), :] = w_ref[g0:g0 + half, lanes]
                stage[c, pl.ds(1, half, stride=2), :] = w_ref[g0 + half:g0 + group, lanes]
                wbf[g0:g0 + group, lanes] = stage[c].astype(BF16)

    def compute(h_ref, y_ref):
        y = jnp.dot(h_ref[...], wbf[...], preferred_element_type=F32) + b_ref[0, 0]
        y_ref[...] = _pack_bf16_pairs(y)

    _stream_expert_blocks(layer, start_ref, nblk_ref, nu_ref, w_hbm, wraw, wsem, h_hbm, y_hbm, cols,
                          hbuf, None, ybuf, hsem, ysem, prepare, compute)


def _down(h, w, b, layer, starts, nblk, n_used, *, group, tn):
    f, d = w.shape[2:]
    return _expert_call(functools.partial(_down_kernel, group=group), layer, starts, nblk, n_used, w, b, h,
                        jax.ShapeDtypeStruct((h.shape[0], d // 2), I32), tn, tn // 2,
                        [pltpu.VMEM((f, tn), BF16), pltpu.VMEM((tn // 128, group, 128), F32)], "moe_down")


def _combine_kernel(dest_ref, next_ref, gate_ref, x_ref, g_ref, b_ref, y_hbm, o_ref, buf, sems, *,
                    alpha, pair_tile):
    i = pl.program_id(0)
    tm = x_ref.shape[0]
    slot = lax.rem(i, 2)

    def gather(d_ref, s):
        def issue(t, carry):
            for k in range(TOP_K):
                pltpu.make_async_copy(y_hbm.at[pl.ds(d_ref[k, t], 1)], buf.at[s, k, pl.ds(t, 1)],
                                      sems.at[s]).start(priority=k % 2)
            return carry

        lax.fori_loop(0, tm, issue, 0)

    @pl.when(i == 0)
    def _():
        gather(dest_ref, 0)

    @pl.when(i + 1 < pl.num_programs(0))
    def _():
        gather(next_ref, 1 - slot)

    for k in range(TOP_K):
        pltpu.make_async_copy(y_hbm.at[pl.ds(0, tm)], buf.at[slot, k], sems.at[slot]).wait()
    gates = gate_ref[...]
    lo = hi = None
    for k in range(TOP_K):
        w = buf[slot, k]
        g = gates[:, k:k + 1]
        lo_k = g * lax.bitcast_convert_type(lax.shift_left(w, 16), F32)
        hi_k = g * lax.bitcast_convert_type(w & jnp.int32(-65536), F32)
        lo = lo_k if lo is None else lo + lo_k
        hi = hi_k if hi is None else hi + hi_k
    half = pair_tile // 2
    parts = []
    for c0 in range(0, lo.shape[1], half):
        parts += [lo[:, c0:c0 + half], hi[:, c0:c0 + half]]
    ffn = jnp.concatenate(parts, axis=1)
    o_ref[...] = _layer_norm(alpha * x_ref[...] + ffn, g_ref[...], b_ref[...])


def _combine(dest, gates, x, y, ln_g, ln_b, alpha, *, pair_tile, tm=256):
    t, d = x.shape
    tm = _tile(t, tm)
    n_steps = t // tm
    row = lambda i: (i, 0)
    fixed = lambda i: (0, 0)
    return pl.pallas_call(
        functools.partial(_combine_kernel, alpha=alpha, pair_tile=pair_tile),
        grid=(n_steps,),
        in_specs=[pl.BlockSpec((TOP_K, tm), lambda i: (0, i), memory_space=pltpu.SMEM),
                  pl.BlockSpec((TOP_K, tm), lambda i: (0, jnp.minimum(i + 1, n_steps - 1)),
                               memory_space=pltpu.SMEM),
                  pl.BlockSpec((tm, TOP_K), row), pl.BlockSpec((tm, d), row),
                  pl.BlockSpec((1, d), fixed), pl.BlockSpec((1, d), fixed),
                  pl.BlockSpec(memory_space=pl.ANY)],
        out_specs=pl.BlockSpec((tm, d), row),
        out_shape=jax.ShapeDtypeStruct((t, d), F32),
        scratch_shapes=[pltpu.VMEM((2, TOP_K, tm, d // 2), I32), pltpu.SemaphoreType.DMA((2,))],
        compiler_params=_params(1),
        name="moe_combine_ln",
    )(dest, dest, gates, x, ln_g.reshape(1, d), ln_b.reshape(1, d), y)


def _moe(x, xp, idx_t, gate_t, rank_t, counts2d, layer, w_gate_up, b_gate_up, w_down, b_down, ln_g, ln_b, alpha):
    t, d = x.shape
    n_exp, f = w_down.shape[1], w_down.shape[2]
    n_blocks = t * TOP_K // MOE_ROWS + n_exp + (MOE_PAIR - 1)
    n_rows = n_blocks * MOE_ROWS

    counts = counts2d[:, 0]
    nblk = (counts + MOE_ROWS - 1) // MOE_ROWS
    pad_end = jnp.cumsum(nblk) * MOE_ROWS
    starts = (pad_end - nblk * MOE_ROWS).astype(I32)
    nu = (pad_end[-1:] // MOE_ROWS).astype(I32)
    experts = jnp.arange(n_exp, dtype=I32)
    dest = jnp.sum(jnp.where(idx_t[:, :, None] == experts, starts, 0), axis=-1) + rank_t

    xs = _dispatch(dest, counts, starts, nu, xp, n_rows)
    tn = _tile(2 * f, MOE_GATE_UP_TILE)
    h = _gate_up(xs, w_gate_up, b_gate_up, layer, starts, nblk, nu, tn=tn)
    tn_down = _tile(d, 1024)
    y = _down(h, w_down, b_down, layer, starts, nblk, nu, group=tn // 2, tn=tn_down)
    return _combine(dest, gate_t.T, x, y, ln_g, ln_b, alpha, pair_tile=tn_down)


def kernel(x, attn_w_in, attn_q_gain, attn_k_gain, attn_w_out, ret_w_in, ret_decay_fwd, ret_decay_bwd,
           ret_w_out, ln_mix_g, ln_mix_b, router_w, router_b, expert_w_gate_up, expert_b_gate_up,
           expert_w_down, expert_b_down, ln_ffn_g, ln_ffn_b):
    bsz, seq, d = x.shape
    depth = ln_mix_g.shape[0]
    alpha = (2 * depth) ** 0.25
    nq = attn_w_out.shape[1]
    n_heads = nq // HEAD_DIM
    n_kv = (attn_w_in.shape[2] - nq) // (2 * HEAD_DIM)
    ret_heads = ret_decay_fwd.shape[1]
    dv = ret_w_out.shape[1] // ret_heads
    dk = (ret_w_in.shape[2] - 2 * ret_heads * dv) // (2 * ret_heads)

    xf = x.reshape(bsz * seq, d)
    for i in range(depth):
        j = i // 2
        if i % 2 == 0:
            h = _matmul(xf, attn_w_in[j].astype(BF16))
            mix = _attention(h, attn_q_gain[j], attn_k_gain[j], bsz, seq, n_heads, n_kv)
            w_out = attn_w_out[j]
        else:
            h = _matmul_rope(xf, ret_w_in[j].astype(BF16), seq, ret_heads, dk)
            lg_f = jnp.log1p(-jnp.exp(ret_decay_fwd[j].astype(F32)))
            lg_b = jnp.log1p(-jnp.exp(ret_decay_bwd[j].astype(F32)))
            mix = _retention(h, lg_f, lg_b, bsz, seq, ret_heads, dk, dv)
            w_out = ret_w_out[j]
        x1, xp, idx_t, gate_t, rank_t, counts = _proj_route(
            mix, w_out.astype(BF16), xf, ln_mix_g[i], ln_mix_b[i], router_w[i], router_b[i], alpha)
        xf = _moe(x1, xp, idx_t, gate_t, rank_t, counts, i, expert_w_gate_up, expert_b_gate_up,
                  expert_w_down, expert_b_down, ln_ffn_g[i], ln_ffn_b[i], alpha)
    return xf.reshape(bsz, seq, d)
```

```python
import functools

import jax
import jax.numpy as jnp
from jax import lax
from jax.experimental import pallas as pl
from jax.experimental.pallas import tpu as pltpu

F32, BF16, I32 = jnp.float32, jnp.bfloat16, jnp.int32

HEAD_DIM = 128
GRID_W = 64
ROPE_THETA = 10000.0
TOP_K = 4
QK_NORM_EPS = 1e-6
RET_SCAN_CHUNK = 256
RET_GN_EPS = 1e-5
SWIGLU_LIMIT = 7.0
SWIGLU_ALPHA = 1.702
LN_EPS = 1e-5
LOG2_E = 1.4426950408889634

V7X_VMEM_LIMIT_BYTES = 56 * 1024 * 1024
MOE_ROWS = 256
MOE_GATE_UP_TILE = 1024
MOE_PAIR = 2
MOE_LOAD_PIECES = 8
MOE_WEIGHT_PIECES = 16
MOE_IN_BUFS = 3
XFER_PRIORITY = 1

NT_DIMS = (((1,), (1,)), ((), ()))
TN_DIMS = (((0,), (0,)), ((), ()))


def _params(n_axes):
    return pltpu.CompilerParams(dimension_semantics=("arbitrary",) * n_axes,
                                vmem_limit_bytes=V7X_VMEM_LIMIT_BYTES)


def _tile(n, pref):
    t = min(n, pref)
    while n % t:
        t -= 128
    assert t > 0, (n, pref)
    return t


def _layer_norm(z, g, b):
    mu = jnp.mean(z, axis=-1, keepdims=True)
    zc = z - mu
    var = jnp.mean(zc * zc, axis=-1, keepdims=True)
    return zc * lax.rsqrt(var + LN_EPS) * g + b


def _rope_tables(seq, dim):
    rows = seq // GRID_W
    row_idx = jnp.repeat(jnp.arange(rows, dtype=F32), GRID_W)
    col_idx = jnp.tile(jnp.arange(GRID_W, dtype=F32), rows)
    quarter = dim // 4
    inv_freq = ROPE_THETA ** (-jnp.arange(quarter, dtype=F32) / quarter)
    ang_r = row_idx[:, None] * inv_freq[None, :]
    ang_c = col_idx[:, None] * inv_freq[None, :]
    ang = jnp.concatenate([ang_r, ang_r, ang_c, ang_c], axis=-1)
    cos, sin = jnp.cos(ang), jnp.sin(ang)
    first = (jnp.arange(dim) % (dim // 2)) < quarter
    return cos, jnp.where(first, -sin, 0.0), jnp.where(first, 0.0, sin)


def _rope(x, cos, sin_up, sin_dn):
    d = x.shape[-1]
    q = d // 4
    if 2 * q == 128:
        sin = sin_up + sin_dn
        parts = [pltpu.roll(x[:, c:c + 128], q, 1) * sin[:, c:c + 128] for c in range(0, d, 128)]
        return x * cos + jnp.concatenate(parts, axis=1)
    return x * cos + pltpu.roll(x, d - q, 1) * sin_up + pltpu.roll(x, q, 1) * sin_dn


def _matmul_kernel(x_ref, w_ref, o_ref):
    o_ref[...] = jnp.dot(x_ref[...].astype(BF16), w_ref[...],
                         preferred_element_type=F32).astype(o_ref.dtype)


def _matmul(x, w, *, tm=512, tn=1024):
    m, k = x.shape
    n = w.shape[1]
    tm, tn = _tile(m, tm), _tile(n, tn)
    return pl.pallas_call(
        _matmul_kernel,
        grid=(n // tn, m // tm),
        in_specs=[pl.BlockSpec((tm, k), lambda j, i: (i, 0)),
                  pl.BlockSpec((k, tn), lambda j, i: (0, j))],
        out_specs=pl.BlockSpec((tm, tn), lambda j, i: (i, j)),
        out_shape=jax.ShapeDtypeStruct((m, n), BF16),
        compiler_params=_params(2),
        name="dense_matmul",
    )(x, w)


def _matmul_rope_kernel(x_ref, w_ref, cos_ref, up_ref, dn_ref, o_ref, *, n_q_tiles, n_rope_tiles, k_scale):
    j = pl.program_id(0)
    acc = jnp.dot(x_ref[...].astype(BF16), w_ref[...], preferred_element_type=F32)

    @pl.when(j >= n_rope_tiles)
    def _():
        o_ref[...] = acc.astype(o_ref.dtype)

    @pl.when(j < n_rope_tiles)
    def _():
        scale = jnp.where(j >= n_q_tiles, k_scale, 1.0)
        dk = cos_ref.shape[1]
        cos, up, dn = cos_ref[...], up_ref[...], dn_ref[...]
        for hh in range(acc.shape[1] // dk):
            cols = slice(hh * dk, (hh + 1) * dk)
            o_ref[:, cols] = (_rope(acc[:, cols], cos, up, dn) * scale).astype(o_ref.dtype)


def _matmul_rope(x, w, seq, n_heads, dk, *, tm=512, tn=1024):
    m, k = x.shape
    n = w.shape[1]
    nqk = n_heads * dk
    tm, tn = _tile(seq, tm), _tile(nqk, tn)
    assert tn % dk == 0 and n % tn == 0
    cos, up, dn = _rope_tables(seq, dk)
    tab_spec = pl.BlockSpec((tm, dk), lambda j, i: (i % (seq // tm), 0))
    return pl.pallas_call(
        functools.partial(_matmul_rope_kernel, n_q_tiles=nqk // tn, n_rope_tiles=2 * nqk // tn,
                          k_scale=dk ** -0.5),
        grid=(n // tn, m // tm),
        in_specs=[pl.BlockSpec((tm, k), lambda j, i: (i, 0)),
                  pl.BlockSpec((k, tn), lambda j, i: (0, j)),
                  tab_spec, tab_spec, tab_spec],
        out_specs=pl.BlockSpec((tm, tn), lambda j, i: (i, j)),
        out_shape=jax.ShapeDtypeStruct((m, n), BF16),
        compiler_params=_params(2),
        name="dense_matmul_rope",
    )(x, w, cos, up, dn)


def _attn_kernel(q_ref, k_ref, v_ref, cq_ref, uq_ref, dq_ref, ck_ref, uk_ref, dk_ref,
                 qg_ref, kg_ref, o_ref, k_scr, v_scr, *, groups, scale):
    def norm_rope(x, gain, cos, up, dn):
        xn = x * lax.rsqrt(jnp.mean(x * x, axis=-1, keepdims=True) + QK_NORM_EPS) * gain
        return _rope(xn, cos, up, dn)

    @pl.when(pl.program_id(2) == 0)
    def _():
        k = k_ref[...].astype(F32)
        k_scr[...] = norm_rope(k, kg_ref[...], ck_ref[...], uk_ref[...], dk_ref[...]).astype(BF16)
        v_scr[:, :HEAD_DIM] = v_ref[...]
        v_scr[:, HEAD_DIM:] = jnp.ones((v_scr.shape[0], HEAD_DIM), BF16)

    cos, up, dn = cq_ref[...], uq_ref[...], dq_ref[...]
    for g in range(groups):
        cols = slice(g * HEAD_DIM, (g + 1) * HEAD_DIM)
        q = q_ref[:, cols].astype(F32)
        qr = (norm_rope(q, qg_ref[...], cos, up, dn) * (scale * LOG2_E)).astype(BF16)
        s = lax.dot_general(qr, k_scr[...], NT_DIMS, preferred_element_type=F32)
        p = jnp.exp2(s - jnp.max(s, axis=-1, keepdims=True))
        o = jnp.dot(p.astype(BF16), v_scr[...], preferred_element_type=F32)
        o_ref[:, cols] = (o[:, :HEAD_DIM] / o[:, HEAD_DIM:HEAD_DIM + 1]).astype(o_ref.dtype)


def _attention(h, q_gain, k_gain, bsz, seq, n_heads, n_kv, *, tq=256):
    t = h.shape[0]
    groups = n_heads // n_kv
    tq = _tile(seq, tq)
    nq = seq // tq
    cos, up, dn = _rope_tables(seq, HEAD_DIM)
    gw = groups * HEAD_DIM
    q_spec = pl.BlockSpec((tq, gw), lambda b, kv, i: (b * nq + i, kv))
    k_spec = pl.BlockSpec((seq, HEAD_DIM), lambda b, kv, i: (b, n_heads + kv))
    v_spec = pl.BlockSpec((seq, HEAD_DIM), lambda b, kv, i: (b, n_heads + n_kv + kv))
    tq_spec = pl.BlockSpec((tq, HEAD_DIM), lambda b, kv, i: (i, 0))
    tk_spec = pl.BlockSpec((seq, HEAD_DIM), lambda b, kv, i: (0, 0))
    gain_spec = pl.BlockSpec((1, HEAD_DIM), lambda b, kv, i: (0, 0))
    return pl.pallas_call(
        functools.partial(_attn_kernel, groups=groups, scale=HEAD_DIM ** -0.5),
        grid=(bsz, n_kv, nq),
        in_specs=[q_spec, k_spec, v_spec, tq_spec, tq_spec, tq_spec, tk_spec, tk_spec, tk_spec,
                  gain_spec, gain_spec],
        out_specs=pl.BlockSpec((tq, gw), lambda b, kv, i: (b * nq + i, kv)),
        out_shape=jax.ShapeDtypeStruct((t, n_heads * HEAD_DIM), BF16),
        scratch_shapes=[pltpu.VMEM((seq, HEAD_DIM), BF16), pltpu.VMEM((seq, 2 * HEAD_DIM), BF16)],
        compiler_params=_params(3),
        name="gqa_attention",
    )(h, h, h, cos, up, dn, cos, up, dn, q_gain.reshape(1, HEAD_DIM), k_gain.reshape(1, HEAD_DIM))


def _ret_kernel(lg_ref, q_ref, k_ref, v_ref, g_ref, o_ref, state, o_acc, *, n_sub):
    d = pl.program_id(2)
    sc = pl.program_id(3)
    n_sc = pl.num_programs(3)
    c = RET_SCAN_CHUNK
    span = n_sub * c

    @pl.when(sc == 0)
    def _():
        state[...] = jnp.zeros_like(state)

    lg = lg_ref[0, 0][:1, :1]
    sgn = 1 - 2 * d
    dist = (lax.broadcasted_iota(I32, (c, c), 0) - lax.broadcasted_iota(I32, (c, c), 1)) * sgn
    intra = jnp.where(dist >= 0, jnp.exp(lg * jnp.maximum(dist, 0).astype(F32)), 0.0)
    pos = lax.broadcasted_iota(I32, (c, 1), 0)
    q_dec = jnp.exp(lg * jnp.where(d == 0, pos + 1, c - pos).astype(F32))
    k_dec = jnp.exp(lg * jnp.where(d == 0, c - 1 - pos, pos).astype(F32))
    chunk_dec = jnp.exp(lg * float(c))
    sci = sc + d * (n_sc - 1 - 2 * sc)

    for j in range(n_sub):
        cj = j + d * (n_sub - 1 - 2 * j)
        r0 = pl.multiple_of(cj * c, c)
        rows = pl.ds(r0, c)
        q = q_ref[rows, :]
        k = k_ref[rows, :]
        v = v_ref[rows, :]
        s = lax.dot_general(q, k, NT_DIMS, preferred_element_type=F32) * intra
        st = state[...]
        o = (jnp.dot(s.astype(BF16), v, preferred_element_type=F32)
             + q_dec * jnp.dot(q, st.astype(BF16), preferred_element_type=F32))
        kd = (k.astype(F32) * k_dec).astype(BF16)
        state[...] = st * chunk_dec + lax.dot_general(kd, v, TN_DIMS, preferred_element_type=F32)
        acc_rows = pl.ds(pl.multiple_of(sci * span + r0, c), c)

        @pl.when(d == 0)
        def _():
            o_acc[acc_rows, :] = o

        @pl.when(d == 1)
        def _():
            ot = o_acc[acc_rows, :] + o
            mu = jnp.mean(ot, axis=-1, keepdims=True)
            oc = ot - mu
            var = jnp.mean(oc * oc, axis=-1, keepdims=True)
            on = oc * lax.rsqrt(var + RET_GN_EPS)
            gate = g_ref[rows, :].astype(F32)
            o_ref[rows, :] = (gate / (1.0 + jnp.exp(-gate)) * on).astype(o_ref.dtype)


def _retention(h, lg_fwd, lg_bwd, bsz, seq, n_heads, dk, dv, *, span=1024):
    t = h.shape[0]
    span = _tile(seq, span)
    n_sub = span // RET_SCAN_CHUNK
    n_sc = seq // span
    lg = jnp.broadcast_to(jnp.stack([lg_fwd, lg_bwd])[:, :, None, None], (2, n_heads, 8, 128)).astype(F32)
    k_off = n_heads
    v_off = 2 * n_heads * dk // dv
    g_off = v_off + n_heads

    def blk(b, d, s):
        return b * n_sc + s + d * (n_sc - 1 - 2 * s)

    def held(b, d, s):
        return b * n_sc + jnp.where(d == 0, n_sc - 1, n_sc - 1 - s)

    return pl.pallas_call(
        functools.partial(_ret_kernel, n_sub=n_sub),
        grid=(bsz, n_heads, 2, n_sc),
        in_specs=[pl.BlockSpec((1, 1, 8, 128), lambda b, hh, d, s: (d, hh, 0, 0)),
                  pl.BlockSpec((span, dk), lambda b, hh, d, s: (blk(b, d, s), hh)),
                  pl.BlockSpec((span, dk), lambda b, hh, d, s: (blk(b, d, s), k_off + hh)),
                  pl.BlockSpec((span, dv), lambda b, hh, d, s: (blk(b, d, s), v_off + hh)),
                  pl.BlockSpec((span, dv), lambda b, hh, d, s: (held(b, d, s), g_off + hh))],
        out_specs=pl.BlockSpec((span, dv), lambda b, hh, d, s: (held(b, d, s), hh)),
        out_shape=jax.ShapeDtypeStruct((t, n_heads * dv), BF16),
        scratch_shapes=[pltpu.VMEM((dk, dv), F32), pltpu.VMEM((seq, dv), F32)],
        compiler_params=_params(4),
        name="retention",
    )(lg, h, h, h, h)


def _pack_bf16_pairs(y):
    half = y.shape[1] // 2
    bits = lax.bitcast_convert_type(y.astype(BF16).astype(F32), I32)
    return (bits[:, half:] & jnp.int32(-65536)) | lax.shift_right_logical(bits[:, :half], 16)


def _unpack_bf16_pairs(w):
    lo = lax.bitcast_convert_type(lax.shift_left(w, 16), F32).astype(BF16)
    hi = lax.bitcast_convert_type(w & jnp.int32(-65536), F32).astype(BF16)
    return lo, hi


def _proj_route_kernel(a_ref, w_ref, res_ref, g_ref, b_ref, rwh_ref, rwl_ref, rb_ref,
                       x_ref, xp_ref, idx_ref, gate_ref, rank_ref, cnt_ref, carry, *, alpha):
    @pl.when(pl.program_id(0) == 0)
    def _():
        carry[...] = jnp.zeros_like(carry)

    acc = jnp.dot(a_ref[...], w_ref[...], preferred_element_type=F32)
    y = _layer_norm(alpha * res_ref[...] + acc, g_ref[...], b_ref[...])
    x_ref[...] = y
    xp_ref[...] = _pack_bf16_pairs(y)

    yh = y.astype(BF16)
    yl = (y - yh.astype(F32)).astype(BF16)
    logits = (lax.dot_general(rwh_ref[...], yh, NT_DIMS, preferred_element_type=F32)
              + lax.dot_general(rwl_ref[...], yh, NT_DIMS, preferred_element_type=F32)
              + lax.dot_general(rwh_ref[...], yl, NT_DIMS, preferred_element_type=F32)
              + rb_ref[...])
    n_exp, tm = logits.shape
    eidx = lax.broadcasted_iota(I32, logits.shape, 0).astype(F32)
    vals, idxs = [], []
    rest = logits
    for _ in range(TOP_K):
        m = jnp.max(rest, axis=0, keepdims=True)
        am = jnp.min(jnp.where(rest == m, eidx, float(n_exp)), axis=0, keepdims=True)
        vals.append(m)
        idxs.append(am)
        rest = jnp.where(eidx == am, -jnp.inf, rest)
    exps = [jnp.exp(v - vals[0]) for v in vals]
    den = exps[0] + exps[1] + exps[2] + exps[3]
    hots = [eidx == am for am in idxs]
    multi = sum(h.astype(F32) for h in hots)

    before = (lax.broadcasted_iota(I32, (tm, tm), 0) < lax.broadcasted_iota(I32, (tm, tm), 1))
    prefix = jnp.dot(multi.astype(BF16), before.astype(BF16), preferred_element_type=F32) + carry[:, :1]
    for k in range(TOP_K):
        idx_ref[k:k + 1, :] = idxs[k].astype(I32)
        gate_ref[k:k + 1, :] = exps[k] / den
        rank_ref[k:k + 1, :] = jnp.sum(jnp.where(hots[k], prefix, 0.0), axis=0, keepdims=True).astype(I32)
    carry[...] = carry[...] + jnp.sum(multi, axis=1, keepdims=True)
    cnt_ref[...] = carry[...].astype(I32)


def _proj_route(a, w, resid, ln_g, ln_b, router_w, router_b, alpha, *, tm=256):
    t, k = a.shape
    d = w.shape[1]
    n_exp = router_w.shape[1]
    tm = _tile(t, tm)
    rwt = router_w.T
    rwh = rwt.astype(BF16)
    rwl = (rwt - rwh.astype(F32)).astype(BF16)
    row = lambda i: (i, 0)
    fixed = lambda i: (0, 0)
    col = lambda i: (0, i)
    outs = pl.pallas_call(
        functools.partial(_proj_route_kernel, alpha=alpha),
        grid=(t // tm,),
        in_specs=[pl.BlockSpec((tm, k), row),
                  pl.BlockSpec((k, d), fixed, pipeline_mode=pl.Buffered(1)),
                  pl.BlockSpec((tm, d), row),
                  pl.BlockSpec((1, d), fixed), pl.BlockSpec((1, d), fixed),
                  pl.BlockSpec((n_exp, d), fixed), pl.BlockSpec((n_exp, d), fixed),
                  pl.BlockSpec((n_exp, 1), fixed)],
        out_specs=[pl.BlockSpec((tm, d), row), pl.BlockSpec((tm, d // 2), row),
                   pl.BlockSpec((TOP_K, tm), col), pl.BlockSpec((TOP_K, tm), col),
                   pl.BlockSpec((TOP_K, tm), col), pl.BlockSpec((n_exp, 128), fixed)],
        out_shape=[jax.ShapeDtypeStruct((t, d), F32), jax.ShapeDtypeStruct((t, d // 2), I32),
                   jax.ShapeDtypeStruct((TOP_K, t), I32), jax.ShapeDtypeStruct((TOP_K, t), F32),
                   jax.ShapeDtypeStruct((TOP_K, t), I32), jax.ShapeDtypeStruct((n_exp, 128), I32)],
        scratch_shapes=[pltpu.VMEM((n_exp, 128), F32)],
        compiler_params=_params(1),
        name="proj_ln_route",
    )(a, w, resid, ln_g.reshape(1, d), ln_b.reshape(1, d), rwh, rwl, router_b.reshape(n_exp, 1))
    return outs


def _dispatch_kernel(dest_ref, cnt_ref, start_ref, nu_ref, xp_ref, xs_hbm, zblk, sem, *, n_tok_steps, n_exp):
    i = pl.program_id(0)
    tm = xp_ref.shape[0]
    n_blocks = xs_hbm.shape[0] // MOE_ROWS

    def row_copy(src, dst_row):
        return pltpu.make_async_copy(src, xs_hbm.at[pl.ds(dst_row, 1)], sem)

    def block_copy(b):
        return pltpu.make_async_copy(zblk, xs_hbm.at[pl.ds(pl.multiple_of(b * MOE_ROWS, MOE_ROWS), MOE_ROWS)], sem)

    @pl.when(i < n_tok_steps)
    def _():
        def issue(t, carry):
            for k in range(TOP_K):
                row_copy(xp_ref.at[pl.ds(t, 1)], dest_ref[k, t]).start(priority=k % 2)
            return carry

        lax.fori_loop(0, tm, issue, 0)
        for k in range(TOP_K):
            pltpu.make_async_copy(xp_ref, xs_hbm.at[pl.ds(0, tm)], sem).wait()

    @pl.when(i == n_tok_steps)
    def _():
        zblk[...] = jnp.zeros_like(zblk)
        zrow = zblk.at[pl.ds(0, 1)]

        def per_expert(e, carry):
            cnt = cnt_ref[e]
            n_pad = lax.rem(MOE_ROWS - lax.rem(cnt, MOE_ROWS), MOE_ROWS)
            first = start_ref[e] + cnt

            def issue(r, c2):
                row_copy(zrow, first + r).start()
                return c2

            def drain(r, c2):
                row_copy(zrow, 0).wait()
                return c2

            lax.fori_loop(0, n_pad, issue, 0)
            lax.fori_loop(0, n_pad, drain, 0)
            return carry

        lax.fori_loop(0, n_exp, per_expert, 0)

        def tail_issue(b, carry):
            block_copy(b).start()
            return carry

        def tail_drain(b, carry):
            block_copy(0).wait()
            return carry

        lax.fori_loop(nu_ref[0], n_blocks, tail_issue, 0)
        lax.fori_loop(nu_ref[0], n_blocks, tail_drain, 0)


def _dispatch(dest, counts, starts, n_used, xp, n_rows, *, tm=256):
    t, w = xp.shape
    tm = _tile(t, tm)
    n_steps = t // tm
    n_exp = counts.shape[0]
    smem = functools.partial(pl.BlockSpec, memory_space=pltpu.SMEM)
    return pl.pallas_call(
        functools.partial(_dispatch_kernel, n_tok_steps=n_steps, n_exp=n_exp),
        grid=(n_steps + 1,),
        in_specs=[smem((TOP_K, tm), lambda i: (0, jnp.minimum(i, n_steps - 1))),
                  smem(), smem(), smem(),
                  pl.BlockSpec((tm, w), lambda i: (jnp.minimum(i, n_steps - 1), 0))],
        out_specs=pl.BlockSpec(memory_space=pl.ANY),
        out_shape=jax.ShapeDtypeStruct((n_rows, w), I32),
        scratch_shapes=[pltpu.VMEM((MOE_ROWS, w), I32), pltpu.SemaphoreType.DMA(())],
        compiler_params=_params(1),
        name="moe_dispatch",
    )(dest, counts, starts, n_used, xp)


def _stream_expert_blocks(layer, start_ref, nblk_ref, nu_ref, w_hbm, wraw, wsem, src_hbm, dst_hbm, dst_cols,
                          inbuf, midbuf, outbuf, insem, outsem, prepare, stage_a, stage_b=None):
    j, e = pl.program_id(0), pl.program_id(1)
    n_j, n_e = pl.num_programs(0), pl.num_programs(1)
    nb = nblk_ref[e]
    base = start_ref[e]
    e_next = jnp.where(e + 1 < n_e, e + 1, 0)
    j_next = jnp.where(e + 1 < n_e, j, j + 1)
    is_last_step = (j == n_j - 1) & (e == n_e - 1)
    wslot = lax.rem(j * n_e + e, 2)
    d_in, tn = wraw.shape[1:]
    piece = d_in // MOE_WEIGHT_PIECES

    def fetch_weights(jj, ee, slot):
        cols = pl.ds(pl.multiple_of(jj * tn, tn), tn)
        for p in range(MOE_WEIGHT_PIECES):
            rows = pl.ds(p * piece, piece)
            pltpu.make_async_copy(w_hbm.at[layer, ee, rows, cols], wraw.at[slot, rows], wsem.at[slot]).start()

    @pl.when((j == 0) & (e == 0) & (nb > 0))
    def _():
        fetch_weights(j, e, wslot)

    @pl.when(jnp.logical_not(is_last_step) & (nblk_ref[e_next] > 0))
    def _():
        fetch_weights(j_next, e_next, 1 - wslot)

    step_rows = MOE_PAIR * MOE_ROWS
    n_it = lax.div(nb + MOE_PAIR - 1, MOE_PAIR)

    class _Load:
        def __init__(self, first_row, r, slot):
            self.row0, self.slot = first_row + r * step_rows, slot

        def start(self, priority):
            piece = step_rows // MOE_LOAD_PIECES
            for p in range(MOE_LOAD_PIECES):
                rows = pl.ds(pl.multiple_of(self.row0 + p * piece, piece), piece)
                pltpu.make_async_copy(src_hbm.at[rows], inbuf.at[self.slot, pl.ds(p * piece, piece)],
                                      insem.at[self.slot]).start(priority=(priority + p) % 2)

        def wait(self):
            rows = pl.ds(pl.multiple_of(self.row0, MOE_ROWS), step_rows)
            pltpu.make_async_copy(src_hbm.at[rows], inbuf.at[self.slot], insem.at[self.slot]).wait()

    load_of = _Load

    def load(r):
        return load_of(base, r, lax.rem(r, MOE_IN_BUFS))

    def store_half(r, hh):
        slot = lax.rem(r, 2)
        rows = pl.ds(pl.multiple_of(base + r * step_rows + hh * MOE_ROWS, MOE_ROWS), MOE_ROWS)
        return pltpu.make_async_copy(outbuf.at[slot, pl.ds(hh * MOE_ROWS, MOE_ROWS)],
                                     dst_hbm.at[rows, dst_cols], outsem.at[slot])

    class _Stores:
        def __init__(self, r):
            self.r = r

        def _each(self, fn):
            fn(store_half(self.r, 0))
            for hh in range(1, MOE_PAIR):
                @pl.when(MOE_PAIR * self.r + hh < nb)
                def _():
                    fn(store_half(self.r, hh))

        def start(self, priority):
            self._each(lambda c: c.start(priority=priority))

        def wait(self):
            self._each(lambda c: c.wait())

    store = _Stores

    def take(r):
        @pl.when(r + MOE_IN_BUFS - 1 < n_it)
        def _():
            load(r + MOE_IN_BUFS - 1).start(priority=XFER_PRIORITY)

        load(r).wait()

    def free_out(r):
        @pl.when(r >= 2)
        def _():
            store(r - 2).wait()

    @pl.when(nb > 0)
    def _():
        @pl.when((j == 0) & (e == 0))
        def _():
            load(0).start(priority=XFER_PRIORITY)

        for r in range(1, MOE_IN_BUFS - 1):
            @pl.when(r < n_it)
            def _():
                load(r).start(priority=XFER_PRIORITY)

        pltpu.make_async_copy(w_hbm.at[layer, 0, :, pl.ds(0, tn)], wraw.at[wslot], wsem.at[wslot]).wait()
        prepare(wraw.at[wslot])

        if stage_b is None:
            def body(r, carry):
                take(r)
                free_out(r)
                stage_a(inbuf.at[lax.rem(r, MOE_IN_BUFS)], outbuf.at[lax.rem(r, 2)])
                store(r).start(priority=XFER_PRIORITY)
                return carry

            lax.fori_loop(0, n_it, body, 0)
        else:
            take(0)
            stage_a(inbuf.at[0], midbuf)

            def body(r, carry):
                take(r)
                free_out(r - 1)
                stage_b(midbuf, outbuf.at[lax.rem(r - 1, 2)])
                stage_a(inbuf.at[lax.rem(r, MOE_IN_BUFS)], midbuf)
                store(r - 1).start(priority=XFER_PRIORITY)
                return carry

            lax.fori_loop(1, n_it, body, 0)
            free_out(n_it - 1)
            stage_b(midbuf, outbuf.at[lax.rem(n_it - 1, 2)])
            store(n_it - 1).start(priority=XFER_PRIORITY)

        @pl.when(n_it >= 2)
        def _():
            store(n_it - 2).wait()

        store(n_it - 1).wait()

    @pl.when(jnp.logical_not(is_last_step) & (nblk_ref[e_next] > 0))
    def _():
        load_of(start_ref[e_next], 0, 0).start(priority=XFER_PRIORITY)

    @pl.when(e == n_e - 1)
    def _():
        outbuf[0] = jnp.zeros(outbuf.shape[1:], outbuf.dtype)

        def tail(b):
            blk = pl.ds(pl.multiple_of(b * MOE_ROWS, MOE_ROWS), MOE_ROWS)
            return pltpu.make_async_copy(outbuf.at[0, pl.ds(0, MOE_ROWS)], dst_hbm.at[blk, dst_cols],
                                         outsem.at[0])

        def issue(b, carry):
            tail(b).start()
            return carry

        def drain(b, carry):
            tail(b).wait()
            return carry

        n_blocks = dst_hbm.shape[0] // MOE_ROWS
        lax.fori_loop(nu_ref[0], n_blocks, issue, 0)
        lax.fori_loop(nu_ref[0], n_blocks, drain, 0)


def _gate_up_kernel(start_ref, nblk_ref, nu_ref, w_hbm, b_ref, xs_hbm, h_hbm, wbf, gubuf, wraw, wsem,
                    xbuf, hbuf, xsem, hsem, *, layer):
    tq = hbuf.shape[2]
    cols = pl.ds(pl.multiple_of(pl.program_id(0) * tq, tq), tq)

    def prepare(w_ref):
        wbf[...] = w_ref[...].astype(BF16)

    def project(x_ref, gu_ref):
        lo, hi = _unpack_bf16_pairs(x_ref[...])
        half = lo.shape[1]
        gu_ref[...] = (jnp.dot(lo, wbf[:half, :], preferred_element_type=F32)
                       + jnp.dot(hi, wbf[half:, :], preferred_element_type=F32) + b_ref[0, 0])

    def activate(gu_ref, h_ref):
        even = (lax.broadcasted_iota(I32, (gu_ref.shape[0], 128), 1) & 1) == 0
        for c in range(tq // 128):
            a = gu_ref[:, c * 128:(c + 1) * 128]
            b = gu_ref[:, tq + c * 128:tq + (c + 1) * 128]
            gate = jnp.where(even, a, pltpu.roll(b, 1, 1))
            up = jnp.where(even, pltpu.roll(a, 127, 1), b)
            hg = jnp.minimum(gate, SWIGLU_LIMIT)
            hu = jnp.clip(up, -SWIGLU_LIMIT, SWIGLU_LIMIT)
            act = (hu + 1.0) * (hg / (1.0 + jnp.exp(-SWIGLU_ALPHA * hg)))
            h_ref[:, c * 128:(c + 1) * 128] = act.astype(h_ref.dtype)

    _stream_expert_blocks(layer, start_ref, nblk_ref, nu_ref, w_hbm, wraw, wsem, xs_hbm, h_hbm, cols,
                          xbuf, gubuf, hbuf, xsem, hsem, prepare, project, activate)


def _expert_call(kernel_fn, layer, starts, nblk, n_used, w, b, src, out_shape, tn, out_w, scratch, name):
    n_exp, d_in, n_out = w.shape[1:]
    assert d_in % (8 * MOE_WEIGHT_PIECES) == 0
    smem = pl.BlockSpec(memory_space=pltpu.SMEM)
    hbm = pl.BlockSpec(memory_space=pl.ANY)
    return pl.pallas_call(
        functools.partial(kernel_fn, layer=layer),
        grid=(n_out // tn, n_exp),
        in_specs=[smem, smem, smem, hbm,
                  pl.BlockSpec((1, 1, 1, tn), lambda j, e: (layer, e, 0, j)), hbm],
        out_specs=hbm,
        out_shape=out_shape,
        scratch_shapes=scratch + [pltpu.VMEM((2, d_in, tn), F32), pltpu.SemaphoreType.DMA((2,)),
                                  pltpu.VMEM((MOE_IN_BUFS, MOE_PAIR * MOE_ROWS, src.shape[1]), src.dtype),
                                  pltpu.VMEM((2, MOE_PAIR * MOE_ROWS, out_w), out_shape.dtype),
                                  pltpu.SemaphoreType.DMA((MOE_IN_BUFS,)), pltpu.SemaphoreType.DMA((2,))],
        compiler_params=_params(2),
        name=name,
    )(starts, nblk, n_used, w, b.reshape(b.shape[0], n_exp, 1, n_out), src)


def _gate_up(xs, w, b, layer, starts, nblk, n_used, *, tn):
    d, f2 = w.shape[2:]
    return _expert_call(_gate_up_kernel, layer, starts, nblk, n_used, w, b, xs,
                        jax.ShapeDtypeStruct((xs.shape[0], f2 // 2), BF16), tn, tn // 2,
                        [pltpu.VMEM((d, tn), BF16), pltpu.VMEM((MOE_PAIR * MOE_ROWS, tn), F32)], "moe_gate_up")


def _down_kernel(start_ref, nblk_ref, nu_ref, w_hbm, b_ref, h_hbm, y_hbm, wbf, stage, wraw, wsem,
                 hbuf, ybuf, hsem, ysem, *, group, layer):
    tn = wbf.shape[1]
    cols = pl.ds(pl.multiple_of(pl.program_id(0) * (tn // 2), tn // 2), tn // 2)

    def prepare(w_ref):
        half = group // 2
        for g0 in range(0, wbf.shape[0], group):
            for c in range(tn // 128):
                lanes = slice(c * 128, (c + 1) * 128)
                stage[c, pl.ds(0, half, stride=2), :] = w_ref[g0:g0 + half, lanes]
                stage[c, pl.ds(1, half, stride=2), :] = w_ref[g0 + half:g0 + group, lanes]
                wbf[g0:g0 + group, lanes] = stage[c].astype(BF16)

    def compute(h_ref, y_ref):
        y = jnp.dot(h_ref[...], wbf[...], preferred_element_type=F32) + b_ref[0, 0]
        y_ref[...] = _pack_bf16_pairs(y)

    _stream_expert_blocks(layer, start_ref, nblk_ref, nu_ref, w_hbm, wraw, wsem, h_hbm, y_hbm, cols,
                          hbuf, None, ybuf, hsem, ysem, prepare, compute)


def _down(h, w, b, layer, starts, nblk, n_used, *, group, tn):
    f, d = w.shape[2:]
    return _expert_call(functools.partial(_down_kernel, group=group), layer, starts, nblk, n_used, w, b, h,
                        jax.ShapeDtypeStruct((h.shape[0], d // 2), I32), tn, tn // 2,
                        [pltpu.VMEM((f, tn), BF16), pltpu.VMEM((tn // 128, group, 128), F32)], "moe_down")


def _combine_kernel(dest_ref, next_ref, gate_ref, x_ref, g_ref, b_ref, y_hbm, o_ref, buf, sems, *,
                    alpha, pair_tile):
    i = pl.program_id(0)
    tm = x_ref.shape[0]
    slot = lax.rem(i, 2)

    def gather(d_ref, s):
        def issue(t, carry):
            for k in range(TOP_K):
                pltpu.make_async_copy(y_hbm.at[pl.ds(d_ref[k, t], 1)], buf.at[s, k, pl.ds(t, 1)],
                                      sems.at[s]).start(priority=k % 2)
            return carry

        lax.fori_loop(0, tm, issue, 0)

    @pl.when(i == 0)
    def _():
        gather(dest_ref, 0)

    @pl.when(i + 1 < pl.num_programs(0))
    def _():
        gather(next_ref, 1 - slot)

    for k in range(TOP_K):
        pltpu.make_async_copy(y_hbm.at[pl.ds(0, tm)], buf.at[slot, k], sems.at[slot]).wait()
    gates = gate_ref[...]
    lo = hi = None
    for k in range(TOP_K):
        w = buf[slot, k]
        g = gates[:, k:k + 1]
        lo_k = g * lax.bitcast_convert_type(lax.shift_left(w, 16), F32)
        hi_k = g * lax.bitcast_convert_type(w & jnp.int32(-65536), F32)
        lo = lo_k if lo is None else lo + lo_k
        hi = hi_k if hi is None else hi + hi_k
    half = pair_tile // 2
    parts = []
    for c0 in range(0, lo.shape[1], half):
        parts += [lo[:, c0:c0 + half], hi[:, c0:c0 + half]]
    ffn = jnp.concatenate(parts, axis=1)
    o_ref[...] = _layer_norm(alpha * x_ref[...] + ffn, g_ref[...], b_ref[...])


def _combine(dest, gates, x, y, ln_g, ln_b, alpha, *, pair_tile, tm=256):
    t, d = x.shape
    tm = _tile(t, tm)
    n_steps = t // tm
    row = lambda i: (i, 0)
    fixed = lambda i: (0, 0)
    return pl.pallas_call(
        functools.partial(_combine_kernel, alpha=alpha, pair_tile=pair_tile),
        grid=(n_steps,),
        in_specs=[pl.BlockSpec((TOP_K, tm), lambda i: (0, i), memory_space=pltpu.SMEM),
                  pl.BlockSpec((TOP_K, tm), lambda i: (0, jnp.minimum(i + 1, n_steps - 1)),
                               memory_space=pltpu.SMEM),
                  pl.BlockSpec((tm, TOP_K), row), pl.BlockSpec((tm, d), row),
                  pl.BlockSpec((1, d), fixed), pl.BlockSpec((1, d), fixed),
                  pl.BlockSpec(memory_space=pl.ANY)],
        out_specs=pl.BlockSpec((tm, d), row),
        out_shape=jax.ShapeDtypeStruct((t, d), F32),
        scratch_shapes=[pltpu.VMEM((2, TOP_K, tm, d // 2), I32), pltpu.SemaphoreType.DMA((2,))],
        compiler_params=_params(1),
        name="moe_combine_ln",
    )(dest, dest, gates, x, ln_g.reshape(1, d), ln_b.reshape(1, d), y)


def _moe(x, xp, idx_t, gate_t, rank_t, counts2d, layer, w_gate_up, b_gate_up, w_down, b_down, ln_g, ln_b, alpha):
    t, d = x.shape
    n_exp, f = w_down.shape[1], w_down.shape[2]
    n_blocks = t * TOP_K // MOE_ROWS + n_exp + (MOE_PAIR - 1)
    n_rows = n_blocks * MOE_ROWS

    counts = counts2d[:, 0]
    nblk = (counts + MOE_ROWS - 1) // MOE_ROWS
    pad_end = jnp.cumsum(nblk) * MOE_ROWS
    starts = (pad_end - nblk * MOE_ROWS).astype(I32)
    nu = (pad_end[-1:] // MOE_ROWS).astype(I32)
    experts = jnp.arange(n_exp, dtype=I32)
    dest = jnp.sum(jnp.where(idx_t[:, :, None] == experts, starts, 0), axis=-1) + rank_t

    xs = _dispatch(dest, counts, starts, nu, xp, n_rows)
    tn = _tile(2 * f, MOE_GATE_UP_TILE)
    h = _gate_up(xs, w_gate_up, b_gate_up, layer, starts, nblk, nu, tn=tn)
    tn_down = _tile(d, 1024)
    y = _down(h, w_down, b_down, layer, starts, nblk, nu, group=tn // 2, tn=tn_down)
    return _combine(dest, gate_t.T, x, y, ln_g, ln_b, alpha, pair_tile=tn_down)


def kernel(x, attn_w_in, attn_q_gain, attn_k_gain, attn_w_out, ret_w_in, ret_decay_fwd, ret_decay_bwd,
           ret_w_out, ln_mix_g, ln_mix_b, router_w, router_b, expert_w_gate_up, expert_b_gate_up,
           expert_w_down, expert_b_down, ln_ffn_g, ln_ffn_b):
    bsz, seq, d = x.shape
    depth = ln_mix_g.shape[0]
    alpha = (2 * depth) ** 0.25
    nq = attn_w_out.shape[1]
    n_heads = nq // HEAD_DIM
    n_kv = (attn_w_in.shape[2] - nq) // (2 * HEAD_DIM)
    ret_heads = ret_decay_fwd.shape[1]
    dv = ret_w_out.shape[1] // ret_heads
    dk = (ret_w_in.shape[2] - 2 * ret_heads * dv) // (2 * ret_heads)

    xf = x.reshape(bsz * seq, d)
    for i in range(depth):
        j = i // 2
        if i % 2 == 0:
            h = _matmul(xf, attn_w_in[j].astype(BF16))
            mix = _attention(h, attn_q_gain[j], attn_k_gain[j], bsz, seq, n_heads, n_kv)
            w_out = attn_w_out[j]
        else:
            h = _matmul_rope(xf, ret_w_in[j].astype(BF16), seq, ret_heads, dk)
            lg_f = jnp.log1p(-jnp.exp(ret_decay_fwd[j].astype(F32)))
            lg_b = jnp.log1p(-jnp.exp(ret_decay_bwd[j].astype(F32)))
            mix = _retention(h, lg_f, lg_b, bsz, seq, ret_heads, dk, dv)
            w_out = ret_w_out[j]
        x1, xp, idx_t, gate_t, rank_t, counts = _proj_route(
            mix, w_out.astype(BF16), xf, ln_mix_g[i], ln_mix_b[i], router_w[i], router_b[i], alpha)
        xf = _moe(x1, xp, idx_t, gate_t, rank_t, counts, i, expert_w_gate_up, expert_b_gate_up,
                  expert_w_down, expert_b_down, ln_ffn_g[i], ln_ffn_b[i], alpha)
    return xf.reshape(bsz, seq, d)
```

```python
import functools

import jax
import jax.numpy as jnp
from jax import lax
from jax.experimental import pallas as pl
from jax.experimental.pallas import tpu as pltpu

F32, BF16, I32 = jnp.float32, jnp.bfloat16, jnp.int32

HEAD_DIM = 128
GRID_W = 64
ROPE_THETA = 10000.0
TOP_K = 4
QK_NORM_EPS = 1e-6
RET_SCAN_CHUNK = 256
RET_GN_EPS = 1e-5
SWIGLU_LIMIT = 7.0
SWIGLU_ALPHA = 1.702
LN_EPS = 1e-5
LOG2_E = 1.4426950408889634

V7X_VMEM_LIMIT_BYTES = 56 * 1024 * 1024
MOE_ROWS = 256
MOE_GATE_UP_TILE = 2048
MOE_PAIR = 1
MOE_LOAD_PIECES = 1
MOE_WEIGHT_PIECES = 16
MOE_IN_BUFS = 3
XFER_PRIORITY = 1

NT_DIMS = (((1,), (1,)), ((), ()))
TN_DIMS = (((0,), (0,)), ((), ()))


def _params(n_axes):
    return pltpu.CompilerParams(dimension_semantics=("arbitrary",) * n_axes,
                                vmem_limit_bytes=V7X_VMEM_LIMIT_BYTES)


def _tile(n, pref):
    t = min(n, pref)
    while n % t:
        t -= 128
    assert t > 0, (n, pref)
    return t


def _layer_norm(z, g, b):
    mu = jnp.mean(z, axis=-1, keepdims=True)
    zc = z - mu
    var = jnp.mean(zc * zc, axis=-1, keepdims=True)
    return zc * lax.rsqrt(var + LN_EPS) * g + b


def _rope_tables(seq, dim):
    rows = seq // GRID_W
    row_idx = jnp.repeat(jnp.arange(rows, dtype=F32), GRID_W)
    col_idx = jnp.tile(jnp.arange(GRID_W, dtype=F32), rows)
    quarter = dim // 4
    inv_freq = ROPE_THETA ** (-jnp.arange(quarter, dtype=F32) / quarter)
    ang_r = row_idx[:, None] * inv_freq[None, :]
    ang_c = col_idx[:, None] * inv_freq[None, :]
    ang = jnp.concatenate([ang_r, ang_r, ang_c, ang_c], axis=-1)
    cos, sin = jnp.cos(ang), jnp.sin(ang)
    first = (jnp.arange(dim) % (dim // 2)) < quarter
    return cos, jnp.where(first, -sin, 0.0), jnp.where(first, 0.0, sin)


def _rope(x, cos, sin_up, sin_dn):
    d = x.shape[-1]
    q = d // 4
    if 2 * q == 128:
        sin = sin_up + sin_dn
        parts = [pltpu.roll(x[:, c:c + 128], q, 1) * sin[:, c:c + 128] for c in range(0, d, 128)]
        return x * cos + jnp.concatenate(parts, axis=1)
    return x * cos + pltpu.roll(x, d - q, 1) * sin_up + pltpu.roll(x, q, 1) * sin_dn


def _matmul_kernel(x_ref, w_ref, o_ref):
    o_ref[...] = jnp.dot(x_ref[...].astype(BF16), w_ref[...],
                         preferred_element_type=F32).astype(o_ref.dtype)


def _matmul(x, w, *, tm=512, tn=1024):
    m, k = x.shape
    n = w.shape[1]
    tm, tn = _tile(m, tm), _tile(n, tn)
    return pl.pallas_call(
        _matmul_kernel,
        grid=(n // tn, m // tm),
        in_specs=[pl.BlockSpec((tm, k), lambda j, i: (i, 0)),
                  pl.BlockSpec((k, tn), lambda j, i: (0, j))],
        out_specs=pl.BlockSpec((tm, tn), lambda j, i: (i, j)),
        out_shape=jax.ShapeDtypeStruct((m, n), BF16),
        compiler_params=_params(2),
        name="dense_matmul",
    )(x, w)


def _matmul_rope_kernel(x_ref, w_ref, cos_ref, up_ref, dn_ref, o_ref, *, n_q_tiles, n_rope_tiles, k_scale):
    j = pl.program_id(0)
    acc = jnp.dot(x_ref[...].astype(BF16), w_ref[...], preferred_element_type=F32)

    @pl.when(j >= n_rope_tiles)
    def _():
        o_ref[...] = acc.astype(o_ref.dtype)

    @pl.when(j < n_rope_tiles)
    def _():
        scale = jnp.where(j >= n_q_tiles, k_scale, 1.0)
        dk = cos_ref.shape[1]
        cos, up, dn = cos_ref[...], up_ref[...], dn_ref[...]
        for hh in range(acc.shape[1] // dk):
            cols = slice(hh * dk, (hh + 1) * dk)
            o_ref[:, cols] = (_rope(acc[:, cols], cos, up, dn) * scale).astype(o_ref.dtype)


def _matmul_rope(x, w, seq, n_heads, dk, *, tm=512, tn=1024):
    m, k = x.shape
    n = w.shape[1]
    nqk = n_heads * dk
    tm, tn = _tile(seq, tm), _tile(nqk, tn)
    assert tn % dk == 0 and n % tn == 0
    cos, up, dn = _rope_tables(seq, dk)
    n_rope = 2 * nqk // tn
    tab_spec = pl.BlockSpec((tm, dk), lambda j, i: (jnp.where(j < n_rope, i % (seq // tm), 0), 0))
    return pl.pallas_call(
        functools.partial(_matmul_rope_kernel, n_q_tiles=nqk // tn, n_rope_tiles=2 * nqk // tn,
                          k_scale=dk ** -0.5),
        grid=(n // tn, m // tm),
        in_specs=[pl.BlockSpec((tm, k), lambda j, i: (i, 0)),
                  pl.BlockSpec((k, tn), lambda j, i: (0, j)),
                  tab_spec, tab_spec, tab_spec],
        out_specs=pl.BlockSpec((tm, tn), lambda j, i: (i, j)),
        out_shape=jax.ShapeDtypeStruct((m, n), BF16),
        compiler_params=_params(2),
        name="dense_matmul_rope",
    )(x, w, cos, up, dn)


def _attn_kernel(q_ref, k_ref, v_ref, cq_ref, uq_ref, dq_ref, ck_ref, uk_ref, dk_ref,
                 qg_ref, kg_ref, o_ref, k_scr, v_scr, *, groups, scale):
    def norm_rope(x, gain, cos, up, dn):
        xn = x * lax.rsqrt(jnp.mean(x * x, axis=-1, keepdims=True) + QK_NORM_EPS) * gain
        return _rope(xn, cos, up, dn)

    @pl.when(pl.program_id(2) == 0)
    def _():
        k = k_ref[...].astype(F32)
        k_scr[...] = norm_rope(k, kg_ref[...], ck_ref[...], uk_ref[...], dk_ref[...]).astype(BF16)
        v_scr[:, :HEAD_DIM] = v_ref[...]
        v_scr[:, HEAD_DIM:] = jnp.ones((v_scr.shape[0], HEAD_DIM), BF16)

    cos, up, dn = cq_ref[...], uq_ref[...], dq_ref[...]
    for g in range(groups):
        cols = slice(g * HEAD_DIM, (g + 1) * HEAD_DIM)
        q = q_ref[:, cols].astype(F32)
        qr = (norm_rope(q, qg_ref[...], cos, up, dn) * (scale * LOG2_E)).astype(BF16)
        s = lax.dot_general(qr, k_scr[...], NT_DIMS, preferred_element_type=F32)
        p = jnp.exp2(s - jnp.max(s, axis=-1, keepdims=True))
        o = jnp.dot(p.astype(BF16), v_scr[...], preferred_element_type=F32)
        o_ref[:, cols] = (o[:, :HEAD_DIM] / o[:, HEAD_DIM:HEAD_DIM + 1]).astype(o_ref.dtype)


def _attention(h, q_gain, k_gain, bsz, seq, n_heads, n_kv, *, tq=256):
    t = h.shape[0]
    groups = n_heads // n_kv
    tq = _tile(seq, tq)
    nq = seq // tq
    cos, up, dn = _rope_tables(seq, HEAD_DIM)
    gw = groups * HEAD_DIM
    q_spec = pl.BlockSpec((tq, gw), lambda b, kv, i: (b * nq + i, kv))
    k_spec = pl.BlockSpec((seq, HEAD_DIM), lambda b, kv, i: (b, n_heads + kv))
    v_spec = pl.BlockSpec((seq, HEAD_DIM), lambda b, kv, i: (b, n_heads + n_kv + kv))
    tq_spec = pl.BlockSpec((tq, HEAD_DIM), lambda b, kv, i: (i, 0))
    tk_spec = pl.BlockSpec((seq, HEAD_DIM), lambda b, kv, i: (0, 0))
    gain_spec = pl.BlockSpec((1, HEAD_DIM), lambda b, kv, i: (0, 0))
    return pl.pallas_call(
        functools.partial(_attn_kernel, groups=groups, scale=HEAD_DIM ** -0.5),
        grid=(bsz, n_kv, nq),
        in_specs=[q_spec, k_spec, v_spec, tq_spec, tq_spec, tq_spec, tk_spec, tk_spec, tk_spec,
                  gain_spec, gain_spec],
        out_specs=pl.BlockSpec((tq, gw), lambda b, kv, i: (b * nq + i, kv)),
        out_shape=jax.ShapeDtypeStruct((t, n_heads * HEAD_DIM), BF16),
        scratch_shapes=[pltpu.VMEM((seq, HEAD_DIM), BF16), pltpu.VMEM((seq, 2 * HEAD_DIM), BF16)],
        compiler_params=_params(3),
        name="gqa_attention",
    )(h, h, h, cos, up, dn, cos, up, dn, q_gain.reshape(1, HEAD_DIM), k_gain.reshape(1, HEAD_DIM))


def _ret_kernel(lg_ref, q_ref, k_ref, v_ref, g_ref, o_ref, state, o_acc, *, n_sub):
    d = pl.program_id(2)
    sc = pl.program_id(3)
    n_sc = pl.num_programs(3)
    c = RET_SCAN_CHUNK
    span = n_sub * c

    @pl.when(sc == 0)
    def _():
        state[...] = jnp.zeros_like(state)

    lg = lg_ref[0, 0][:1, :1]
    sgn = 1 - 2 * d
    dist = (lax.broadcasted_iota(I32, (c, c), 0) - lax.broadcasted_iota(I32, (c, c), 1)) * sgn
    intra = jnp.where(dist >= 0, jnp.exp(lg * jnp.maximum(dist, 0).astype(F32)), 0.0)
    pos = lax.broadcasted_iota(I32, (c, 1), 0)
    q_dec = jnp.exp(lg * jnp.where(d == 0, pos + 1, c - pos).astype(F32))
    k_dec = jnp.exp(lg * jnp.where(d == 0, c - 1 - pos, pos).astype(F32))
    chunk_dec = jnp.exp(lg * float(c))
    sci = sc + d * (n_sc - 1 - 2 * sc)

    for j in range(n_sub):
        cj = j + d * (n_sub - 1 - 2 * j)
        r0 = pl.multiple_of(cj * c, c)
        rows = pl.ds(r0, c)
        q = q_ref[rows, :]
        k = k_ref[rows, :]
        v = v_ref[rows, :]
        s = lax.dot_general(q, k, NT_DIMS, preferred_element_type=F32) * intra
        st = state[...]
        o = (jnp.dot(s.astype(BF16), v, preferred_element_type=F32)
             + q_dec * jnp.dot(q, st.astype(BF16), preferred_element_type=F32))
        kd = (k.astype(F32) * k_dec).astype(BF16)
        state[...] = st * chunk_dec + lax.dot_general(kd, v, TN_DIMS, preferred_element_type=F32)
        acc_rows = pl.ds(pl.multiple_of(sci * span + r0, c), c)

        @pl.when(d == 0)
        def _():
            o_acc[acc_rows, :] = o

        @pl.when(d == 1)
        def _():
            ot = o_acc[acc_rows, :] + o
            mu = jnp.mean(ot, axis=-1, keepdims=True)
            oc = ot - mu
            var = jnp.mean(oc * oc, axis=-1, keepdims=True)
            on = oc * lax.rsqrt(var + RET_GN_EPS)
            gate = g_ref[rows, :].astype(F32)
            o_ref[rows, :] = (gate / (1.0 + jnp.exp(-gate)) * on).astype(o_ref.dtype)


def _retention(h, lg_fwd, lg_bwd, bsz, seq, n_heads, dk, dv, *, span=1024):
    t = h.shape[0]
    span = _tile(seq, span)
    n_sub = span // RET_SCAN_CHUNK
    n_sc = seq // span
    lg = jnp.broadcast_to(jnp.stack([lg_fwd, lg_bwd])[:, :, None, None], (2, n_heads, 8, 128)).astype(F32)
    k_off = n_heads
    v_off = 2 * n_heads * dk // dv
    g_off = v_off + n_heads

    def blk(b, d, s):
        return b * n_sc + s + d * (n_sc - 1 - 2 * s)

    def held(b, d, s):
        return b * n_sc + jnp.where(d == 0, n_sc - 1, n_sc - 1 - s)

    return pl.pallas_call(
        functools.partial(_ret_kernel, n_sub=n_sub),
        grid=(bsz, n_heads, 2, n_sc),
        in_specs=[pl.BlockSpec((1, 1, 8, 128), lambda b, hh, d, s: (d, hh, 0, 0)),
                  pl.BlockSpec((span, dk), lambda b, hh, d, s: (blk(b, d, s), hh)),
                  pl.BlockSpec((span, dk), lambda b, hh, d, s: (blk(b, d, s), k_off + hh)),
                  pl.BlockSpec((span, dv), lambda b, hh, d, s: (blk(b, d, s), v_off + hh)),
                  pl.BlockSpec((span, dv), lambda b, hh, d, s: (held(b, d, s), g_off + hh))],
        out_specs=pl.BlockSpec((span, dv), lambda b, hh, d, s: (held(b, d, s), hh)),
        out_shape=jax.ShapeDtypeStruct((t, n_heads * dv), BF16),
        scratch_shapes=[pltpu.VMEM((dk, dv), F32), pltpu.VMEM((seq, dv), F32)],
        compiler_params=_params(4),
        name="retention",
    )(lg, h, h, h, h)


def _pack_bf16_pairs(y):
    half = y.shape[1] // 2
    bits = lax.bitcast_convert_type(y.astype(BF16).astype(F32), I32)
    return (bits[:, half:] & jnp.int32(-65536)) | lax.shift_right_logical(bits[:, :half], 16)


def _unpack_bf16_pairs(w):
    lo = lax.bitcast_convert_type(lax.shift_left(w, 16), F32).astype(BF16)
    hi = lax.bitcast_convert_type(w & jnp.int32(-65536), F32).astype(BF16)
    return lo, hi


def _proj_route_kernel(a_ref, w_ref, res_ref, g_ref, b_ref, rwh_ref, rwl_ref, rb_ref,
                       x_ref, xp_ref, idx_ref, gate_ref, rank_ref, cnt_ref, carry, *, alpha):
    @pl.when(pl.program_id(0) == 0)
    def _():
        carry[...] = jnp.zeros_like(carry)

    acc = jnp.dot(a_ref[...], w_ref[...], preferred_element_type=F32)
    y = _layer_norm(alpha * res_ref[...] + acc, g_ref[...], b_ref[...])
    x_ref[...] = y
    xp_ref[...] = _pack_bf16_pairs(y)

    yh = y.astype(BF16)
    yl = (y - yh.astype(F32)).astype(BF16)
    logits = (lax.dot_general(rwh_ref[...], yh, NT_DIMS, preferred_element_type=F32)
              + lax.dot_general(rwl_ref[...], yh, NT_DIMS, preferred_element_type=F32)
              + lax.dot_general(rwh_ref[...], yl, NT_DIMS, preferred_element_type=F32)
              + rb_ref[...])
    n_exp, tm = logits.shape
    eidx = lax.broadcasted_iota(I32, logits.shape, 0).astype(F32)
    vals, idxs = [], []
    rest = logits
    for _ in range(TOP_K):
        m = jnp.max(rest, axis=0, keepdims=True)
        am = jnp.min(jnp.where(rest == m, eidx, float(n_exp)), axis=0, keepdims=True)
        vals.append(m)
        idxs.append(am)
        rest = jnp.where(eidx == am, -jnp.inf, rest)
    exps = [jnp.exp(v - vals[0]) for v in vals]
    den = exps[0] + exps[1] + exps[2] + exps[3]
    hots = [eidx == am for am in idxs]
    multi = sum(h.astype(F32) for h in hots)

    before = (lax.broadcasted_iota(I32, (tm, tm), 0) < lax.broadcasted_iota(I32, (tm, tm), 1))
    prefix = jnp.dot(multi.astype(BF16), before.astype(BF16), preferred_element_type=F32) + carry[:, :1]
    for k in range(TOP_K):
        idx_ref[k:k + 1, :] = idxs[k].astype(I32)
        gate_ref[k:k + 1, :] = exps[k] / den
        rank_ref[k:k + 1, :] = jnp.sum(jnp.where(hots[k], prefix, 0.0), axis=0, keepdims=True).astype(I32)
    carry[...] = carry[...] + jnp.sum(multi, axis=1, keepdims=True)
    cnt_ref[...] = carry[...].astype(I32)


def _proj_route(a, w, resid, ln_g, ln_b, router_w, router_b, alpha, *, tm=256):
    t, k = a.shape
    d = w.shape[1]
    n_exp = router_w.shape[1]
    tm = _tile(t, tm)
    rwt = router_w.T
    rwh = rwt.astype(BF16)
    rwl = (rwt - rwh.astype(F32)).astype(BF16)
    row = lambda i: (i, 0)
    fixed = lambda i: (0, 0)
    col = lambda i: (0, i)
    outs = pl.pallas_call(
        functools.partial(_proj_route_kernel, alpha=alpha),
        grid=(t // tm,),
        in_specs=[pl.BlockSpec((tm, k), row),
                  pl.BlockSpec((k, d), fixed, pipeline_mode=pl.Buffered(1)),
                  pl.BlockSpec((tm, d), row),
                  pl.BlockSpec((1, d), fixed), pl.BlockSpec((1, d), fixed),
                  pl.BlockSpec((n_exp, d), fixed), pl.BlockSpec((n_exp, d), fixed),
                  pl.BlockSpec((n_exp, 1), fixed)],
        out_specs=[pl.BlockSpec((tm, d), row), pl.BlockSpec((tm, d // 2), row),
                   pl.BlockSpec((TOP_K, tm), col), pl.BlockSpec((TOP_K, tm), col),
                   pl.BlockSpec((TOP_K, tm), col), pl.BlockSpec((n_exp, 128), fixed)],
        out_shape=[jax.ShapeDtypeStruct((t, d), F32), jax.ShapeDtypeStruct((t, d // 2), I32),
                   jax.ShapeDtypeStruct((TOP_K, t), I32), jax.ShapeDtypeStruct((TOP_K, t), F32),
                   jax.ShapeDtypeStruct((TOP_K, t), I32), jax.ShapeDtypeStruct((n_exp, 128), I32)],
        scratch_shapes=[pltpu.VMEM((n_exp, 128), F32)],
        compiler_params=_params(1),
        name="proj_ln_route",
    )(a, w, resid, ln_g.reshape(1, d), ln_b.reshape(1, d), rwh, rwl, router_b.reshape(n_exp, 1))
    return outs


def _dispatch_kernel(dest_ref, cnt_ref, start_ref, nu_ref, xp_ref, xs_hbm, zblk, sem, *, n_tok_steps, n_exp):
    i = pl.program_id(0)
    tm = xp_ref.shape[0]
    n_blocks = xs_hbm.shape[0] // MOE_ROWS

    def row_copy(src, dst_row):
        return pltpu.make_async_copy(src, xs_hbm.at[pl.ds(dst_row, 1)], sem)

    def block_copy(b):
        return pltpu.make_async_copy(zblk, xs_hbm.at[pl.ds(pl.multiple_of(b * MOE_ROWS, MOE_ROWS), MOE_ROWS)], sem)

    @pl.when(i < n_tok_steps)
    def _():
        def issue(t, carry):
            for k in range(TOP_K):
                row_copy(xp_ref.at[pl.ds(t, 1)], dest_ref[k, t]).start(priority=k % 2)
            return carry

        lax.fori_loop(0, tm, issue, 0)
        for k in range(TOP_K):
            pltpu.make_async_copy(xp_ref, xs_hbm.at[pl.ds(0, tm)], sem).wait()

    @pl.when(i == n_tok_steps)
    def _():
        zblk[...] = jnp.zeros_like(zblk)
        zrow = zblk.at[pl.ds(0, 1)]

        def per_expert(e, carry):
            cnt = cnt_ref[e]
            n_pad = lax.rem(MOE_ROWS - lax.rem(cnt, MOE_ROWS), MOE_ROWS)
            first = start_ref[e] + cnt

            def issue(r, c2):
                row_copy(zrow, first + r).start()
                return c2

            def drain(r, c2):
                row_copy(zrow, 0).wait()
                return c2

            lax.fori_loop(0, n_pad, issue, 0)
            lax.fori_loop(0, n_pad, drain, 0)
            return carry

        lax.fori_loop(0, n_exp, per_expert, 0)

        def tail_issue(b, carry):
            block_copy(b).start()
            return carry

        def tail_drain(b, carry):
            block_copy(0).wait()
            return carry

        lax.fori_loop(nu_ref[0], n_blocks, tail_issue, 0)
        lax.fori_loop(nu_ref[0], n_blocks, tail_drain, 0)


def _dispatch(dest, counts, starts, n_used, xp, n_rows, *, tm=256):
    t, w = xp.shape
    tm = _tile(t, tm)
    n_steps = t // tm
    n_exp = counts.shape[0]
    smem = functools.partial(pl.BlockSpec, memory_space=pltpu.SMEM)
    return pl.pallas_call(
        functools.partial(_dispatch_kernel, n_tok_steps=n_steps, n_exp=n_exp),
        grid=(n_steps + 1,),
        in_specs=[smem((TOP_K, tm), lambda i: (0, jnp.minimum(i, n_steps - 1))),
                  smem(), smem(), smem(),
                  pl.BlockSpec((tm, w), lambda i: (jnp.minimum(i, n_steps - 1), 0))],
        out_specs=pl.BlockSpec(memory_space=pl.ANY),
        out_shape=jax.ShapeDtypeStruct((n_rows, w), I32),
        scratch_shapes=[pltpu.VMEM((MOE_ROWS, w), I32), pltpu.SemaphoreType.DMA(())],
        compiler_params=_params(1),
        name="moe_dispatch",
    )(dest, counts, starts, n_used, xp)


def _stream_expert_blocks(layer, start_ref, nblk_ref, nu_ref, w_hbm, wraw, wsem, src_hbm, dst_hbm, dst_cols,
                          inbuf, midbuf, outbuf, insem, outsem, prepare, stage_a, stage_b=None):
    j, e = pl.program_id(0), pl.program_id(1)
    n_j, n_e = pl.num_programs(0), pl.num_programs(1)
    nb = nblk_ref[e]
    base = start_ref[e]
    e_next = jnp.where(e + 1 < n_e, e + 1, 0)
    j_next = jnp.where(e + 1 < n_e, j, j + 1)
    is_last_step = (j == n_j - 1) & (e == n_e - 1)
    wslot = lax.rem(j * n_e + e, 2)
    d_in, tn = wraw.shape[1:]
    piece = d_in // MOE_WEIGHT_PIECES

    def fetch_weights(jj, ee, slot):
        cols = pl.ds(pl.multiple_of(jj * tn, tn), tn)
        for p in range(MOE_WEIGHT_PIECES):
            rows = pl.ds(p * piece, piece)
            pltpu.make_async_copy(w_hbm.at[layer, ee, rows, cols], wraw.at[slot, rows], wsem.at[slot]).start()

    @pl.when((j == 0) & (e == 0) & (nb > 0))
    def _():
        fetch_weights(j, e, wslot)

    @pl.when(jnp.logical_not(is_last_step) & (nblk_ref[e_next] > 0))
    def _():
        fetch_weights(j_next, e_next, 1 - wslot)

    step_rows = MOE_PAIR * MOE_ROWS
    n_it = lax.div(nb + MOE_PAIR - 1, MOE_PAIR)

    class _Load:
        def __init__(self, first_row, r, slot):
            self.row0, self.slot = first_row + r * step_rows, slot

        def start(self, priority):
            piece = step_rows // MOE_LOAD_PIECES
            for p in range(MOE_LOAD_PIECES):
                rows = pl.ds(pl.multiple_of(self.row0 + p * piece, piece), piece)
                pltpu.make_async_copy(src_hbm.at[rows], inbuf.at[self.slot, pl.ds(p * piece, piece)],
                                      insem.at[self.slot]).start(priority=(priority + p) % 2)

        def wait(self):
            rows = pl.ds(pl.multiple_of(self.row0, MOE_ROWS), step_rows)
            pltpu.make_async_copy(src_hbm.at[rows], inbuf.at[self.slot], insem.at[self.slot]).wait()

    load_of = _Load

    def load(r):
        return load_of(base, r, lax.rem(r, MOE_IN_BUFS))

    def store_half(r, hh):
        slot = lax.rem(r, 2)
        rows = pl.ds(pl.multiple_of(base + r * step_rows + hh * MOE_ROWS, MOE_ROWS), MOE_ROWS)
        return pltpu.make_async_copy(outbuf.at[slot, pl.ds(hh * MOE_ROWS, MOE_ROWS)],
                                     dst_hbm.at[rows, dst_cols], outsem.at[slot])

    class _Stores:
        def __init__(self, r):
            self.r = r

        def _each(self, fn):
            fn(store_half(self.r, 0))
            for hh in range(1, MOE_PAIR):
                @pl.when(MOE_PAIR * self.r + hh < nb)
                def _():
                    fn(store_half(self.r, hh))

        def start(self, priority):
            self._each(lambda c: c.start(priority=priority))

        def wait(self):
            self._each(lambda c: c.wait())

    store = _Stores

    def take(r):
        @pl.when(r + MOE_IN_BUFS - 1 < n_it)
        def _():
            load(r + MOE_IN_BUFS - 1).start(priority=XFER_PRIORITY)

        load(r).wait()

    def free_out(r):
        @pl.when(r >= 2)
        def _():
            store(r - 2).wait()

    @pl.when(nb > 0)
    def _():
        @pl.when((j == 0) & (e == 0))
        def _():
            load(0).start(priority=XFER_PRIORITY)

        for r in range(1, MOE_IN_BUFS - 1):
            @pl.when(r < n_it)
            def _():
                load(r).start(priority=XFER_PRIORITY)

        pltpu.make_async_copy(w_hbm.at[layer, 0, :, pl.ds(0, tn)], wraw.at[wslot], wsem.at[wslot]).wait()
        prepare(wraw.at[wslot])

        if stage_b is None:
            def body(r, carry):
                take(r)
                free_out(r)
                stage_a(inbuf.at[lax.rem(r, MOE_IN_BUFS)], outbuf.at[lax.rem(r, 2)])
                store(r).start(priority=XFER_PRIORITY)
                return carry

            lax.fori_loop(0, n_it, body, 0)
        else:
            take(0)
            stage_a(inbuf.at[0], midbuf)

            def body(r, carry):
                take(r)
                free_out(r - 1)
                stage_b(midbuf, outbuf.at[lax.rem(r - 1, 2)])
                stage_a(inbuf.at[lax.rem(r, MOE_IN_BUFS)], midbuf)
                store(r - 1).start(priority=XFER_PRIORITY)
                return carry

            lax.fori_loop(1, n_it, body, 0)
            free_out(n_it - 1)
            stage_b(midbuf, outbuf.at[lax.rem(n_it - 1, 2)])
            store(n_it - 1).start(priority=XFER_PRIORITY)

        @pl.when(n_it >= 2)
        def _():
            store(n_it - 2).wait()

        store(n_it - 1).wait()

    @pl.when(jnp.logical_not(is_last_step) & (nblk_ref[e_next] > 0))
    def _():
        load_of(start_ref[e_next], 0, 0).start(priority=XFER_PRIORITY)

    @pl.when(e == n_e - 1)
    def _():
        outbuf[0] = jnp.zeros(outbuf.shape[1:], outbuf.dtype)

        def tail(b):
            blk = pl.ds(pl.multiple_of(b * MOE_ROWS, MOE_ROWS), MOE_ROWS)
            return pltpu.make_async_copy(outbuf.at[0, pl.ds(0, MOE_ROWS)], dst_hbm.at[blk, dst_cols],
                                         outsem.at[0])

        def issue(b, carry):
            tail(b).start()
            return carry

        def drain(b, carry):
            tail(b).wait()
            return carry

        n_blocks = dst_hbm.shape[0] // MOE_ROWS
        lax.fori_loop(nu_ref[0], n_blocks, issue, 0)
        lax.fori_loop(nu_ref[0], n_blocks, drain, 0)


def _gate_up_kernel(start_ref, nblk_ref, nu_ref, w_hbm, b_ref, xs_hbm, h_hbm, wbf, gubuf, wraw, wsem,
                    xbuf, hbuf, xsem, hsem, *, layer):
    tq = hbuf.shape[2]
    cols = pl.ds(pl.multiple_of(pl.program_id(0) * tq, tq), tq)

    def prepare(w_ref):
        wbf[...] = w_ref[...].astype(BF16)

    def project(x_ref, gu_ref):
        lo, hi = _unpack_bf16_pairs(x_ref[...])
        half = lo.shape[1]
        gu_ref[...] = (jnp.dot(lo, wbf[:half, :], preferred_element_type=F32)
                       + jnp.dot(hi, wbf[half:, :], preferred_element_type=F32) + b_ref[0, 0])

    def activate(gu_ref, h_ref):
        even = (lax.broadcasted_iota(I32, (gu_ref.shape[0], 128), 1) & 1) == 0
        for c in range(tq // 128):
            a = gu_ref[:, c * 128:(c + 1) * 128]
            b = gu_ref[:, tq + c * 128:tq + (c + 1) * 128]
            gate = jnp.where(even, a, pltpu.roll(b, 1, 1))
            up = jnp.where(even, pltpu.roll(a, 127, 1), b)
            hg = jnp.minimum(gate, SWIGLU_LIMIT)
            hu = jnp.clip(up, -SWIGLU_LIMIT, SWIGLU_LIMIT)
            act = (hu + 1.0) * (hg / (1.0 + jnp.exp(-SWIGLU_ALPHA * hg)))
            h_ref[:, c * 128:(c + 1) * 128] = act.astype(h_ref.dtype)

    _stream_expert_blocks(layer, start_ref, nblk_ref, nu_ref, w_hbm, wraw, wsem, xs_hbm, h_hbm, cols,
                          xbuf, gubuf, hbuf, xsem, hsem, prepare, project, activate)


def _expert_call(kernel_fn, layer, starts, nblk, n_used, w, b, src, out_shape, tn, out_w, scratch, name):
    n_exp, d_in, n_out = w.shape[1:]
    assert d_in % (8 * MOE_WEIGHT_PIECES) == 0
    smem = pl.BlockSpec(memory_space=pltpu.SMEM)
    hbm = pl.BlockSpec(memory_space=pl.ANY)
    return pl.pallas_call(
        functools.partial(kernel_fn, layer=layer),
        grid=(n_out // tn, n_exp),
        in_specs=[smem, smem, smem, hbm,
                  pl.BlockSpec((1, 1, 1, tn), lambda j, e: (layer, e, 0, j)), hbm],
        out_specs=hbm,
        out_shape=out_shape,
        scratch_shapes=scratch + [pltpu.VMEM((2, d_in, tn), F32), pltpu.SemaphoreType.DMA((2,)),
                                  pltpu.VMEM((MOE_IN_BUFS, MOE_PAIR * MOE_ROWS, src.shape[1]), src.dtype),
                                  pltpu.VMEM((2, MOE_PAIR * MOE_ROWS, out_w), out_shape.dtype),
                                  pltpu.SemaphoreType.DMA((MOE_IN_BUFS,)), pltpu.SemaphoreType.DMA((2,))],
        compiler_params=_params(2),
        name=name,
    )(starts, nblk, n_used, w, b.reshape(b.shape[0], n_exp, 1, n_out), src)


def _gate_up(xs, w, b, layer, starts, nblk, n_used, *, tn):
    d, f2 = w.shape[2:]
    return _expert_call(_gate_up_kernel, layer, starts, nblk, n_used, w, b, xs,
                        jax.ShapeDtypeStruct((xs.shape[0], f2 // 2), BF16), tn, tn // 2,
                        [pltpu.VMEM((d, tn), BF16), pltpu.VMEM((MOE_PAIR * MOE_ROWS, tn), F32)], "moe_gate_up")


def _down_kernel(start_ref, nblk_ref, nu_ref, w_hbm, b_ref, h_hbm, y_hbm, wbf, stage, wraw, wsem,
                 hbuf, ybuf, hsem, ysem, *, group, layer):
    tn = wbf.shape[1]
    cols = pl.ds(pl.multiple_of(pl.program_id(0) * (tn // 2), tn // 2), tn // 2)

    def prepare(w_ref):
        half = group // 2
        for g0 in range(0, wbf.shape[0], group):
            for c in range(tn // 128):
                lanes = slice(c * 128, (c + 1) * 128)
                stage[c, pl.ds(0, half, stride=2), :] = w_ref[g0:g0 + half, lanes]
                stage[c, pl.ds(1, half, stride=2), :] = w_ref[g0 + half:g0 + group, lanes]
                wbf[g0:g0 + group, lanes] = stage[c].astype(BF16)

    def compute(h_ref, y_ref):
        y = jnp.dot(h_ref[...], wbf[...], preferred_element_type=F32) + b_ref[0, 0]
        y_ref[...] = _pack_bf16_pairs(y)

    _stream_expert_blocks(layer, start_ref, nblk_ref, nu_ref, w_hbm, wraw, wsem, h_hbm, y_hbm, cols,
                          hbuf, None, ybuf, hsem, ysem, prepare, compute)


def _down(h, w, b, layer, starts, nblk, n_used, *, group, tn):
    f, d = w.shape[2:]
    return _expert_call(functools.partial(_down_kernel, group=group), layer, starts, nblk, n_used, w, b, h,
                        jax.ShapeDtypeStruct((h.shape[0], d // 2), I32), tn, tn // 2,
                        [pltpu.VMEM((f, tn), BF16), pltpu.VMEM((tn // 128, group, 128), F32)], "moe_down")


def _combine_kernel(dest_ref, next_ref, gate_ref, x_ref, g_ref, b_ref, y_hbm, o_ref, ob_ref, buf, sems, *,
                    alpha, pair_tile):
    i = pl.program_id(0)
    tm = x_ref.shape[0]
    slot = lax.rem(i, 2)

    def gather(d_ref, s):
        def issue(t, carry):
            for k in range(TOP_K):
                pltpu.make_async_copy(y_hbm.at[pl.ds(d_ref[k, t], 1)], buf.at[s, k, pl.ds(t, 1)],
                                      sems.at[s]).start(priority=k % 2)
            return carry

        lax.fori_loop(0, tm, issue, 0)

    @pl.when(i == 0)
    def _():
        gather(dest_ref, 0)

    @pl.when(i + 1 < pl.num_programs(0))
    def _():
        gather(next_ref, 1 - slot)

    for k in range(TOP_K):
        pltpu.make_async_copy(y_hbm.at[pl.ds(0, tm)], buf.at[slot, k], sems.at[slot]).wait()
    gates = gate_ref[...]
    lo = hi = None
    for k in range(TOP_K):
        w = buf[slot, k]
        g = gates[:, k:k + 1]
        lo_k = g * lax.bitcast_convert_type(lax.shift_left(w, 16), F32)
        hi_k = g * lax.bitcast_convert_type(w & jnp.int32(-65536), F32)
        lo = lo_k if lo is None else lo + lo_k
        hi = hi_k if hi is None else hi + hi_k
    half = pair_tile // 2
    parts = []
    for c0 in range(0, lo.shape[1], half):
        parts += [lo[:, c0:c0 + half], hi[:, c0:c0 + half]]
    ffn = jnp.concatenate(parts, axis=1)
    out = _layer_norm(alpha * x_ref[...] + ffn, g_ref[...], b_ref[...])
    o_ref[...] = out
    ob_ref[...] = out.astype(ob_ref.dtype)


def _combine(dest, gates, x, y, ln_g, ln_b, alpha, *, pair_tile, tm=256):
    t, d = x.shape
    tm = _tile(t, tm)
    n_steps = t // tm
    row = lambda i: (i, 0)
    fixed = lambda i: (0, 0)
    return pl.pallas_call(
        functools.partial(_combine_kernel, alpha=alpha, pair_tile=pair_tile),
        grid=(n_steps,),
        in_specs=[pl.BlockSpec((TOP_K, tm), lambda i: (0, i), memory_space=pltpu.SMEM),
                  pl.BlockSpec((TOP_K, tm), lambda i: (0, jnp.minimum(i + 1, n_steps - 1)),
                               memory_space=pltpu.SMEM),
                  pl.BlockSpec((tm, TOP_K), row), pl.BlockSpec((tm, d), row),
                  pl.BlockSpec((1, d), fixed), pl.BlockSpec((1, d), fixed),
                  pl.BlockSpec(memory_space=pl.ANY)],
        out_specs=[pl.BlockSpec((tm, d), row), pl.BlockSpec((tm, d), row)],
        out_shape=[jax.ShapeDtypeStruct((t, d), F32), jax.ShapeDtypeStruct((t, d), BF16)],
        scratch_shapes=[pltpu.VMEM((2, TOP_K, tm, d // 2), I32), pltpu.SemaphoreType.DMA((2,))],
        compiler_params=_params(1),
        name="moe_combine_ln",
    )(dest, dest, gates, x, ln_g.reshape(1, d), ln_b.reshape(1, d), y)


def _moe(x, xp, idx_t, gate_t, rank_t, counts2d, layer, w_gate_up, b_gate_up, w_down, b_down, ln_g, ln_b, alpha):
    t, d = x.shape
    n_exp, f = w_down.shape[1], w_down.shape[2]
    n_blocks = t * TOP_K // MOE_ROWS + n_exp + (MOE_PAIR - 1)
    n_rows = n_blocks * MOE_ROWS

    counts = counts2d[:, 0]
    nblk = (counts + MOE_ROWS - 1) // MOE_ROWS
    pad_end = jnp.cumsum(nblk) * MOE_ROWS
    starts = (pad_end - nblk * MOE_ROWS).astype(I32)
    nu = (pad_end[-1:] // MOE_ROWS).astype(I32)
    experts = jnp.arange(n_exp, dtype=I32)
    dest = jnp.sum(jnp.where(idx_t[:, :, None] == experts, starts, 0), axis=-1) + rank_t

    xs = _dispatch(dest, counts, starts, nu, xp, n_rows)
    tn = _tile(2 * f, MOE_GATE_UP_TILE)
    h = _gate_up(xs, w_gate_up, b_gate_up, layer, starts, nblk, nu, tn=tn)
    tn_down = _tile(d, 1024)
    y = _down(h, w_down, b_down, layer, starts, nblk, nu, group=tn // 2, tn=tn_down)
    return _combine(dest, gate_t.T, x, y, ln_g, ln_b, alpha, pair_tile=tn_down)


def kernel(x, attn_w_in, attn_q_gain, attn_k_gain, attn_w_out, ret_w_in, ret_decay_fwd, ret_decay_bwd,
           ret_w_out, ln_mix_g, ln_mix_b, router_w, router_b, expert_w_gate_up, expert_b_gate_up,
           expert_w_down, expert_b_down, ln_ffn_g, ln_ffn_b):
    bsz, seq, d = x.shape
    depth = ln_mix_g.shape[0]
    alpha = (2 * depth) ** 0.25
    nq = attn_w_out.shape[1]
    n_heads = nq // HEAD_DIM
    n_kv = (attn_w_in.shape[2] - nq) // (2 * HEAD_DIM)
    ret_heads = ret_decay_fwd.shape[1]
    dv = ret_w_out.shape[1] // ret_heads
    dk = (ret_w_in.shape[2] - 2 * ret_heads * dv) // (2 * ret_heads)

    xf = x.reshape(bsz * seq, d)
    xin = xf
    for i in range(depth):
        j = i // 2
        if i % 2 == 0:
            h = _matmul(xin, attn_w_in[j].astype(BF16))
            mix = _attention(h, attn_q_gain[j], attn_k_gain[j], bsz, seq, n_heads, n_kv)
            w_out = attn_w_out[j]
        else:
            h = _matmul_rope(xin, ret_w_in[j].astype(BF16), seq, ret_heads, dk)
            lg_f = jnp.log1p(-jnp.exp(ret_decay_fwd[j].astype(F32)))
            lg_b = jnp.log1p(-jnp.exp(ret_decay_bwd[j].astype(F32)))
            mix = _retention(h, lg_f, lg_b, bsz, seq, ret_heads, dk, dv)
            w_out = ret_w_out[j]
        x1, xp, idx_t, gate_t, rank_t, counts = _proj_route(
            mix, w_out.astype(BF16), xf, ln_mix_g[i], ln_mix_b[i], router_w[i], router_b[i], alpha)
        xf, xin = _moe(x1, xp, idx_t, gate_t, rank_t, counts, i, expert_w_gate_up, expert_b_gate_up,
                       expert_w_down, expert_b_down, ln_ffn_g[i], ln_ffn_b[i], alpha)
    return xf.reshape(bsz, seq, d)
```

```python
import functools

import jax
import jax.numpy as jnp
from jax import lax
from jax.experimental import pallas as pl
from jax.experimental.pallas import tpu as pltpu

F32, BF16, I32 = jnp.float32, jnp.bfloat16, jnp.int32

HEAD_DIM = 128
GRID_W = 64
ROPE_THETA = 10000.0
TOP_K = 4
QK_NORM_EPS = 1e-6
RET_SCAN_CHUNK = 256
RET_GN_EPS = 1e-5
SWIGLU_LIMIT = 7.0
SWIGLU_ALPHA = 1.702
LN_EPS = 1e-5
LOG2_E = 1.4426950408889634

V7X_VMEM_LIMIT_BYTES = 56 * 1024 * 1024
MOE_ROWS = 256
MOE_GATE_UP_TILE = 2048
MOE_DOWN_TILE = 2048
MOE_PAIR = 1
MOE_LOAD_PIECES = 1
MOE_WEIGHT_PIECES = 16
MOE_IN_BUFS = 3
XFER_PRIORITY = 1

NT_DIMS = (((1,), (1,)), ((), ()))
TN_DIMS = (((0,), (0,)), ((), ()))


def _params(n_axes):
    return pltpu.CompilerParams(dimension_semantics=("arbitrary",) * n_axes,
                                vmem_limit_bytes=V7X_VMEM_LIMIT_BYTES)


def _tile(n, pref):
    t = min(n, pref)
    while n % t:
        t -= 128
    assert t > 0, (n, pref)
    return t


def _layer_norm(z, g, b):
    mu = jnp.mean(z, axis=-1, keepdims=True)
    zc = z - mu
    var = jnp.mean(zc * zc, axis=-1, keepdims=True)
    return zc * lax.rsqrt(var + LN_EPS) * g + b


def _rope_tables(seq, dim):
    rows = seq // GRID_W
    row_idx = jnp.repeat(jnp.arange(rows, dtype=F32), GRID_W)
    col_idx = jnp.tile(jnp.arange(GRID_W, dtype=F32), rows)
    quarter = dim // 4
    inv_freq = ROPE_THETA ** (-jnp.arange(quarter, dtype=F32) / quarter)
    ang_r = row_idx[:, None] * inv_freq[None, :]
    ang_c = col_idx[:, None] * inv_freq[None, :]
    ang = jnp.concatenate([ang_r, ang_r, ang_c, ang_c], axis=-1)
    cos, sin = jnp.cos(ang), jnp.sin(ang)
    first = (jnp.arange(dim) % (dim // 2)) < quarter
    return cos, jnp.where(first, -sin, 0.0), jnp.where(first, 0.0, sin)


def _rope(x, cos, sin_up, sin_dn):
    d = x.shape[-1]
    q = d // 4
    if 2 * q == 128:
        sin = sin_up + sin_dn
        parts = [pltpu.roll(x[:, c:c + 128], q, 1) * sin[:, c:c + 128] for c in range(0, d, 128)]
        return x * cos + jnp.concatenate(parts, axis=1)
    return x * cos + pltpu.roll(x, d - q, 1) * sin_up + pltpu.roll(x, q, 1) * sin_dn


def _matmul_kernel(x_ref, w_ref, o_ref):
    o_ref[...] = jnp.dot(x_ref[...].astype(BF16), w_ref[...],
                         preferred_element_type=F32).astype(o_ref.dtype)


def _matmul(x, w, *, tm=512, tn=1024):
    m, k = x.shape
    n = w.shape[1]
    tm, tn = _tile(m, tm), _tile(n, tn)
    return pl.pallas_call(
        _matmul_kernel,
        grid=(n // tn, m // tm),
        in_specs=[pl.BlockSpec((tm, k), lambda j, i: (i, 0)),
                  pl.BlockSpec((k, tn), lambda j, i: (0, j))],
        out_specs=pl.BlockSpec((tm, tn), lambda j, i: (i, j)),
        out_shape=jax.ShapeDtypeStruct((m, n), BF16),
        compiler_params=_params(2),
        name="dense_matmul",
    )(x, w)


def _matmul_rope_kernel(x_ref, w_ref, cos_ref, up_ref, dn_ref, o_ref, *, n_q_tiles, n_rope_tiles, k_scale):
    j = pl.program_id(0)
    acc = jnp.dot(x_ref[...].astype(BF16), w_ref[...], preferred_element_type=F32)

    @pl.when(j >= n_rope_tiles)
    def _():
        o_ref[...] = acc.astype(o_ref.dtype)

    @pl.when(j < n_rope_tiles)
    def _():
        scale = jnp.where(j >= n_q_tiles, k_scale, 1.0)
        dk = cos_ref.shape[1]
        cos, up, dn = cos_ref[...], up_ref[...], dn_ref[...]
        for hh in range(acc.shape[1] // dk):
            cols = slice(hh * dk, (hh + 1) * dk)
            o_ref[:, cols] = (_rope(acc[:, cols], cos, up, dn) * scale).astype(o_ref.dtype)


def _matmul_rope(x, w, seq, n_heads, dk, *, tm=512, tn=1024):
    m, k = x.shape
    n = w.shape[1]
    nqk = n_heads * dk
    tm, tn = _tile(seq, tm), _tile(nqk, tn)
    assert tn % dk == 0 and n % tn == 0
    cos, up, dn = _rope_tables(seq, dk)
    n_rope = 2 * nqk // tn
    tab_spec = pl.BlockSpec((tm, dk), lambda j, i: (jnp.where(j < n_rope, i % (seq // tm), 0), 0))
    return pl.pallas_call(
        functools.partial(_matmul_rope_kernel, n_q_tiles=nqk // tn, n_rope_tiles=2 * nqk // tn,
                          k_scale=dk ** -0.5),
        grid=(n // tn, m // tm),
        in_specs=[pl.BlockSpec((tm, k), lambda j, i: (i, 0)),
                  pl.BlockSpec((k, tn), lambda j, i: (0, j)),
                  tab_spec, tab_spec, tab_spec],
        out_specs=pl.BlockSpec((tm, tn), lambda j, i: (i, j)),
        out_shape=jax.ShapeDtypeStruct((m, n), BF16),
        compiler_params=_params(2),
        name="dense_matmul_rope",
    )(x, w, cos, up, dn)


def _attn_kernel(q_ref, k_ref, v_ref, cq_ref, uq_ref, dq_ref, ck_ref, uk_ref, dk_ref,
                 qg_ref, kg_ref, o_ref, k_scr, v_scr, *, groups, scale):
    def norm_rope(x, gain, cos, up, dn):
        xn = x * lax.rsqrt(jnp.mean(x * x, axis=-1, keepdims=True) + QK_NORM_EPS) * gain
        return _rope(xn, cos, up, dn)

    @pl.when(pl.program_id(2) == 0)
    def _():
        k = k_ref[...].astype(F32)
        k_scr[...] = norm_rope(k, kg_ref[...], ck_ref[...], uk_ref[...], dk_ref[...]).astype(BF16)
        v_scr[:, :HEAD_DIM] = v_ref[...]
        v_scr[:, HEAD_DIM:] = jnp.ones((v_scr.shape[0], HEAD_DIM), BF16)

    cos, up, dn = cq_ref[...], uq_ref[...], dq_ref[...]
    for g in range(groups):
        cols = slice(g * HEAD_DIM, (g + 1) * HEAD_DIM)
        q = q_ref[:, cols].astype(F32)
        qr = (norm_rope(q, qg_ref[...], cos, up, dn) * (scale * LOG2_E)).astype(BF16)
        s = lax.dot_general(qr, k_scr[...], NT_DIMS, preferred_element_type=F32)
        p = jnp.exp2(s - jnp.max(s, axis=-1, keepdims=True))
        o = jnp.dot(p.astype(BF16), v_scr[...], preferred_element_type=F32)
        o_ref[:, cols] = (o[:, :HEAD_DIM] / o[:, HEAD_DIM:HEAD_DIM + 1]).astype(o_ref.dtype)


def _attention(h, q_gain, k_gain, bsz, seq, n_heads, n_kv, *, tq=256):
    t = h.shape[0]
    groups = n_heads // n_kv
    tq = _tile(seq, tq)
    nq = seq // tq
    cos, up, dn = _rope_tables(seq, HEAD_DIM)
    gw = groups * HEAD_DIM
    q_spec = pl.BlockSpec((tq, gw), lambda b, kv, i: (b * nq + i, kv))
    k_spec = pl.BlockSpec((seq, HEAD_DIM), lambda b, kv, i: (b, n_heads + kv))
    v_spec = pl.BlockSpec((seq, HEAD_DIM), lambda b, kv, i: (b, n_heads + n_kv + kv))
    tq_spec = pl.BlockSpec((tq, HEAD_DIM), lambda b, kv, i: (i, 0))
    tk_spec = pl.BlockSpec((seq, HEAD_DIM), lambda b, kv, i: (0, 0))
    gain_spec = pl.BlockSpec((1, HEAD_DIM), lambda b, kv, i: (0, 0))
    return pl.pallas_call(
        functools.partial(_attn_kernel, groups=groups, scale=HEAD_DIM ** -0.5),
        grid=(bsz, n_kv, nq),
        in_specs=[q_spec, k_spec, v_spec, tq_spec, tq_spec, tq_spec, tk_spec, tk_spec, tk_spec,
                  gain_spec, gain_spec],
        out_specs=pl.BlockSpec((tq, gw), lambda b, kv, i: (b * nq + i, kv)),
        out_shape=jax.ShapeDtypeStruct((t, n_heads * HEAD_DIM), BF16),
        scratch_shapes=[pltpu.VMEM((seq, HEAD_DIM), BF16), pltpu.VMEM((seq, 2 * HEAD_DIM), BF16)],
        compiler_params=_params(3),
        name="gqa_attention",
    )(h, h, h, cos, up, dn, cos, up, dn, q_gain.reshape(1, HEAD_DIM), k_gain.reshape(1, HEAD_DIM))


def _ret_kernel(lg_ref, q_ref, k_ref, v_ref, g_ref, o_ref, state, o_acc, *, n_sub):
    d = pl.program_id(2)
    sc = pl.program_id(3)
    n_sc = pl.num_programs(3)
    c = RET_SCAN_CHUNK
    span = n_sub * c

    @pl.when(sc == 0)
    def _():
        state[...] = jnp.zeros_like(state)

    lg = lg_ref[0, 0][:1, :1]
    sgn = 1 - 2 * d
    dist = (lax.broadcasted_iota(I32, (c, c), 0) - lax.broadcasted_iota(I32, (c, c), 1)) * sgn
    intra = jnp.where(dist >= 0, jnp.exp(lg * jnp.maximum(dist, 0).astype(F32)), 0.0)
    pos = lax.broadcasted_iota(I32, (c, 1), 0)
    q_dec = jnp.exp(lg * jnp.where(d == 0, pos + 1, c - pos).astype(F32))
    k_dec = jnp.exp(lg * jnp.where(d == 0, c - 1 - pos, pos).astype(F32))
    chunk_dec = jnp.exp(lg * float(c))
    sci = sc + d * (n_sc - 1 - 2 * sc)

    for j in range(n_sub):
        cj = j + d * (n_sub - 1 - 2 * j)
        r0 = pl.multiple_of(cj * c, c)
        rows = pl.ds(r0, c)
        q = q_ref[rows, :]
        k = k_ref[rows, :]
        v = v_ref[rows, :]
        s = lax.dot_general(q, k, NT_DIMS, preferred_element_type=F32) * intra
        st = state[...]
        o = (jnp.dot(s.astype(BF16), v, preferred_element_type=F32)
             + q_dec * jnp.dot(q, st.astype(BF16), preferred_element_type=F32))
        kd = (k.astype(F32) * k_dec).astype(BF16)
        state[...] = st * chunk_dec + lax.dot_general(kd, v, TN_DIMS, preferred_element_type=F32)
        acc_rows = pl.ds(pl.multiple_of(sci * span + r0, c), c)

        @pl.when(d == 0)
        def _():
            o_acc[acc_rows, :] = o

        @pl.when(d == 1)
        def _():
            ot = o_acc[acc_rows, :] + o
            mu = jnp.mean(ot, axis=-1, keepdims=True)
            oc = ot - mu
            var = jnp.mean(oc * oc, axis=-1, keepdims=True)
            on = oc * lax.rsqrt(var + RET_GN_EPS)
            gate = g_ref[rows, :].astype(F32)
            o_ref[rows, :] = (gate / (1.0 + jnp.exp(-gate)) * on).astype(o_ref.dtype)


def _retention(h, lg_fwd, lg_bwd, bsz, seq, n_heads, dk, dv, *, span=1024):
    t = h.shape[0]
    span = _tile(seq, span)
    n_sub = span // RET_SCAN_CHUNK
    n_sc = seq // span
    lg = jnp.broadcast_to(jnp.stack([lg_fwd, lg_bwd])[:, :, None, None], (2, n_heads, 8, 128)).astype(F32)
    k_off = n_heads
    v_off = 2 * n_heads * dk // dv
    g_off = v_off + n_heads

    def blk(b, d, s):
        return b * n_sc + s + d * (n_sc - 1 - 2 * s)

    def held(b, d, s):
        return b * n_sc + jnp.where(d == 0, n_sc - 1, n_sc - 1 - s)

    return pl.pallas_call(
        functools.partial(_ret_kernel, n_sub=n_sub),
        grid=(bsz, n_heads, 2, n_sc),
        in_specs=[pl.BlockSpec((1, 1, 8, 128), lambda b, hh, d, s: (d, hh, 0, 0)),
                  pl.BlockSpec((span, dk), lambda b, hh, d, s: (blk(b, d, s), hh)),
                  pl.BlockSpec((span, dk), lambda b, hh, d, s: (blk(b, d, s), k_off + hh)),
                  pl.BlockSpec((span, dv), lambda b, hh, d, s: (blk(b, d, s), v_off + hh)),
                  pl.BlockSpec((span, dv), lambda b, hh, d, s: (held(b, d, s), g_off + hh))],
        out_specs=pl.BlockSpec((span, dv), lambda b, hh, d, s: (held(b, d, s), hh)),
        out_shape=jax.ShapeDtypeStruct((t, n_heads * dv), BF16),
        scratch_shapes=[pltpu.VMEM((dk, dv), F32), pltpu.VMEM((seq, dv), F32)],
        compiler_params=_params(4),
        name="retention",
    )(lg, h, h, h, h)


def _pack_bf16_pairs(y):
    half = y.shape[1] // 2
    bits = lax.bitcast_convert_type(y.astype(BF16).astype(F32), I32)
    return (bits[:, half:] & jnp.int32(-65536)) | lax.shift_right_logical(bits[:, :half], 16)


def _unpack_bf16_pairs(w):
    lo = lax.bitcast_convert_type(lax.shift_left(w, 16), F32).astype(BF16)
    hi = lax.bitcast_convert_type(w & jnp.int32(-65536), F32).astype(BF16)
    return lo, hi


def _proj_route_kernel(a_ref, w_ref, res_ref, g_ref, b_ref, rwh_ref, rwl_ref, rb_ref,
                       x_ref, xp_ref, idx_ref, gate_ref, rank_ref, cnt_ref, carry, *, alpha):
    @pl.when(pl.program_id(0) == 0)
    def _():
        carry[...] = jnp.zeros_like(carry)

    acc = jnp.dot(a_ref[...], w_ref[...], preferred_element_type=F32)
    y = _layer_norm(alpha * res_ref[...] + acc, g_ref[...], b_ref[...])
    x_ref[...] = y
    xp_ref[...] = _pack_bf16_pairs(y)

    yh = y.astype(BF16)
    yl = (y - yh.astype(F32)).astype(BF16)
    logits = (lax.dot_general(rwh_ref[...], yh, NT_DIMS, preferred_element_type=F32)
              + lax.dot_general(rwl_ref[...], yh, NT_DIMS, preferred_element_type=F32)
              + lax.dot_general(rwh_ref[...], yl, NT_DIMS, preferred_element_type=F32)
              + rb_ref[...])
    n_exp, tm = logits.shape
    eidx = lax.broadcasted_iota(I32, logits.shape, 0).astype(F32)
    vals, idxs = [], []
    rest = logits
    for _ in range(TOP_K):
        m = jnp.max(rest, axis=0, keepdims=True)
        am = jnp.min(jnp.where(rest == m, eidx, float(n_exp)), axis=0, keepdims=True)
        vals.append(m)
        idxs.append(am)
        rest = jnp.where(eidx == am, -jnp.inf, rest)
    exps = [jnp.exp(v - vals[0]) for v in vals]
    den = exps[0] + exps[1] + exps[2] + exps[3]
    hots = [eidx == am for am in idxs]
    multi = sum(h.astype(F32) for h in hots)

    before = (lax.broadcasted_iota(I32, (tm, tm), 0) < lax.broadcasted_iota(I32, (tm, tm), 1))
    prefix = jnp.dot(multi.astype(BF16), before.astype(BF16), preferred_element_type=F32) + carry[:, :1]
    for k in range(TOP_K):
        idx_ref[k:k + 1, :] = idxs[k].astype(I32)
        gate_ref[k:k + 1, :] = exps[k] / den
        rank_ref[k:k + 1, :] = jnp.sum(jnp.where(hots[k], prefix, 0.0), axis=0, keepdims=True).astype(I32)
    carry[...] = carry[...] + jnp.sum(multi, axis=1, keepdims=True)
    cnt_ref[...] = carry[...].astype(I32)


def _proj_route(a, w, resid, ln_g, ln_b, router_w, router_b, alpha, *, tm=256):
    t, k = a.shape
    d = w.shape[1]
    n_exp = router_w.shape[1]
    tm = _tile(t, tm)
    rwt = router_w.T
    rwh = rwt.astype(BF16)
    rwl = (rwt - rwh.astype(F32)).astype(BF16)
    row = lambda i: (i, 0)
    fixed = lambda i: (0, 0)
    col = lambda i: (0, i)
    outs = pl.pallas_call(
        functools.partial(_proj_route_kernel, alpha=alpha),
        grid=(t // tm,),
        in_specs=[pl.BlockSpec((tm, k), row),
                  pl.BlockSpec((k, d), fixed, pipeline_mode=pl.Buffered(1)),
                  pl.BlockSpec((tm, d), row),
                  pl.BlockSpec((1, d), fixed), pl.BlockSpec((1, d), fixed),
                  pl.BlockSpec((n_exp, d), fixed), pl.BlockSpec((n_exp, d), fixed),
                  pl.BlockSpec((n_exp, 1), fixed)],
        out_specs=[pl.BlockSpec((tm, d), row), pl.BlockSpec((tm, d // 2), row),
                   pl.BlockSpec((TOP_K, tm), col), pl.BlockSpec((TOP_K, tm), col),
                   pl.BlockSpec((TOP_K, tm), col), pl.BlockSpec((n_exp, 128), fixed)],
        out_shape=[jax.ShapeDtypeStruct((t, d), F32), jax.ShapeDtypeStruct((t, d // 2), I32),
                   jax.ShapeDtypeStruct((TOP_K, t), I32), jax.ShapeDtypeStruct((TOP_K, t), F32),
                   jax.ShapeDtypeStruct((TOP_K, t), I32), jax.ShapeDtypeStruct((n_exp, 128), I32)],
        scratch_shapes=[pltpu.VMEM((n_exp, 128), F32)],
        compiler_params=_params(1),
        name="proj_ln_route",
    )(a, w, resid, ln_g.reshape(1, d), ln_b.reshape(1, d), rwh, rwl, router_b.reshape(n_exp, 1))
    return outs


def _dispatch_kernel(dest_ref, cnt_ref, start_ref, nu_ref, xp_ref, xs_hbm, zblk, sem, *, n_tok_steps, n_exp):
    i = pl.program_id(0)
    tm = xp_ref.shape[0]
    n_blocks = xs_hbm.shape[0] // MOE_ROWS

    def row_copy(src, dst_row):
        return pltpu.make_async_copy(src, xs_hbm.at[pl.ds(dst_row, 1)], sem)

    def block_copy(b):
        return pltpu.make_async_copy(zblk, xs_hbm.at[pl.ds(pl.multiple_of(b * MOE_ROWS, MOE_ROWS), MOE_ROWS)], sem)

    @pl.when(i < n_tok_steps)
    def _():
        def issue(t, carry):
            for k in range(TOP_K):
                row_copy(xp_ref.at[pl.ds(t, 1)], dest_ref[k, t]).start(priority=k % 2)
            return carry

        lax.fori_loop(0, tm, issue, 0)
        for k in range(TOP_K):
            pltpu.make_async_copy(xp_ref, xs_hbm.at[pl.ds(0, tm)], sem).wait()

    @pl.when(i == n_tok_steps)
    def _():
        zblk[...] = jnp.zeros_like(zblk)
        zrow = zblk.at[pl.ds(0, 1)]

        def per_expert(e, carry):
            cnt = cnt_ref[e]
            n_pad = lax.rem(MOE_ROWS - lax.rem(cnt, MOE_ROWS), MOE_ROWS)
            first = start_ref[e] + cnt

            def issue(r, c2):
                row_copy(zrow, first + r).start()
                return c2

            def drain(r, c2):
                row_copy(zrow, 0).wait()
                return c2

            lax.fori_loop(0, n_pad, issue, 0)
            lax.fori_loop(0, n_pad, drain, 0)
            return carry

        lax.fori_loop(0, n_exp, per_expert, 0)

        def tail_issue(b, carry):
            block_copy(b).start()
            return carry

        def tail_drain(b, carry):
            block_copy(0).wait()
            return carry

        lax.fori_loop(nu_ref[0], n_blocks, tail_issue, 0)
        lax.fori_loop(nu_ref[0], n_blocks, tail_drain, 0)


def _dispatch(dest, counts, starts, n_used, xp, n_rows, *, tm=256):
    t, w = xp.shape
    tm = _tile(t, tm)
    n_steps = t // tm
    n_exp = counts.shape[0]
    smem = functools.partial(pl.BlockSpec, memory_space=pltpu.SMEM)
    return pl.pallas_call(
        functools.partial(_dispatch_kernel, n_tok_steps=n_steps, n_exp=n_exp),
        grid=(n_steps + 1,),
        in_specs=[smem((TOP_K, tm), lambda i: (0, jnp.minimum(i, n_steps - 1))),
                  smem(), smem(), smem(),
                  pl.BlockSpec((tm, w), lambda i: (jnp.minimum(i, n_steps - 1), 0))],
        out_specs=pl.BlockSpec(memory_space=pl.ANY),
        out_shape=jax.ShapeDtypeStruct((n_rows, w), I32),
        scratch_shapes=[pltpu.VMEM((MOE_ROWS, w), I32), pltpu.SemaphoreType.DMA(())],
        compiler_params=_params(1),
        name="moe_dispatch",
    )(dest, counts, starts, n_used, xp)


def _stream_expert_blocks(layer, start_ref, nblk_ref, nu_ref, w_hbm, wraw, wsem, src_hbm, dst_hbm, dst_cols,
                          inbuf, midbuf, outbuf, insem, outsem, prepare, stage_a, stage_b=None):
    j, e = pl.program_id(0), pl.program_id(1)
    n_j, n_e = pl.num_programs(0), pl.num_programs(1)
    nb = nblk_ref[e]
    base = start_ref[e]
    e_next = jnp.where(e + 1 < n_e, e + 1, 0)
    j_next = jnp.where(e + 1 < n_e, j, j + 1)
    is_last_step = (j == n_j - 1) & (e == n_e - 1)
    wslot = lax.rem(j * n_e + e, 2)
    d_in, tn = wraw.shape[1:]
    piece = d_in // MOE_WEIGHT_PIECES

    def fetch_weights(jj, ee, slot):
        cols = pl.ds(pl.multiple_of(jj * tn, tn), tn)
        for p in range(MOE_WEIGHT_PIECES):
            rows = pl.ds(p * piece, piece)
            pltpu.make_async_copy(w_hbm.at[layer, ee, rows, cols], wraw.at[slot, rows], wsem.at[slot]).start()

    @pl.when((j == 0) & (e == 0) & (nb > 0))
    def _():
        fetch_weights(j, e, wslot)

    @pl.when(jnp.logical_not(is_last_step) & (nblk_ref[e_next] > 0))
    def _():
        fetch_weights(j_next, e_next, 1 - wslot)

    step_rows = MOE_PAIR * MOE_ROWS
    n_it = lax.div(nb + MOE_PAIR - 1, MOE_PAIR)

    class _Load:
        def __init__(self, first_row, r, slot):
            self.row0, self.slot = first_row + r * step_rows, slot

        def start(self, priority):
            piece = step_rows // MOE_LOAD_PIECES
            for p in range(MOE_LOAD_PIECES):
                rows = pl.ds(pl.multiple_of(self.row0 + p * piece, piece), piece)
                pltpu.make_async_copy(src_hbm.at[rows], inbuf.at[self.slot, pl.ds(p * piece, piece)],
                                      insem.at[self.slot]).start(priority=(priority + p) % 2)

        def wait(self):
            rows = pl.ds(pl.multiple_of(self.row0, MOE_ROWS), step_rows)
            pltpu.make_async_copy(src_hbm.at[rows], inbuf.at[self.slot], insem.at[self.slot]).wait()

    load_of = _Load

    def load(r):
        return load_of(base, r, lax.rem(r, MOE_IN_BUFS))

    def store_half(r, hh):
        slot = lax.rem(r, 2)
        rows = pl.ds(pl.multiple_of(base + r * step_rows + hh * MOE_ROWS, MOE_ROWS), MOE_ROWS)
        return pltpu.make_async_copy(outbuf.at[slot, pl.ds(hh * MOE_ROWS, MOE_ROWS)],
                                     dst_hbm.at[rows, dst_cols], outsem.at[slot])

    class _Stores:
        def __init__(self, r):
            self.r = r

        def _each(self, fn):
            fn(store_half(self.r, 0))
            for hh in range(1, MOE_PAIR):
                @pl.when(MOE_PAIR * self.r + hh < nb)
                def _():
                    fn(store_half(self.r, hh))

        def start(self, priority):
            self._each(lambda c: c.start(priority=priority))

        def wait(self):
            self._each(lambda c: c.wait())

    store = _Stores

    def take(r):
        @pl.when(r + MOE_IN_BUFS - 1 < n_it)
        def _():
            load(r + MOE_IN_BUFS - 1).start(priority=XFER_PRIORITY)

        load(r).wait()

    def free_out(r):
        @pl.when(r >= 2)
        def _():
            store(r - 2).wait()

    @pl.when(nb > 0)
    def _():
        @pl.when((j == 0) & (e == 0))
        def _():
            load(0).start(priority=XFER_PRIORITY)

        for r in range(1, MOE_IN_BUFS - 1):
            @pl.when(r < n_it)
            def _():
                load(r).start(priority=XFER_PRIORITY)

        pltpu.make_async_copy(w_hbm.at[layer, 0, :, pl.ds(0, tn)], wraw.at[wslot], wsem.at[wslot]).wait()
        prepare(wraw.at[wslot])

        if stage_b is None:
            def body(r, carry):
                take(r)
                free_out(r)
                stage_a(inbuf.at[lax.rem(r, MOE_IN_BUFS)], outbuf.at[lax.rem(r, 2)])
                store(r).start(priority=XFER_PRIORITY)
                return carry

            lax.fori_loop(0, n_it, body, 0)
        else:
            take(0)
            stage_a(inbuf.at[0], midbuf)

            def body(r, carry):
                take(r)
                free_out(r - 1)
                stage_b(midbuf, outbuf.at[lax.rem(r - 1, 2)])
                stage_a(inbuf.at[lax.rem(r, MOE_IN_BUFS)], midbuf)
                store(r - 1).start(priority=XFER_PRIORITY)
                return carry

            lax.fori_loop(1, n_it, body, 0)
            free_out(n_it - 1)
            stage_b(midbuf, outbuf.at[lax.rem(n_it - 1, 2)])
            store(n_it - 1).start(priority=XFER_PRIORITY)

        @pl.when(n_it >= 2)
        def _():
            store(n_it - 2).wait()

        store(n_it - 1).wait()

    @pl.when(jnp.logical_not(is_last_step) & (nblk_ref[e_next] > 0))
    def _():
        load_of(start_ref[e_next], 0, 0).start(priority=XFER_PRIORITY)

    @pl.when(e == n_e - 1)
    def _():
        outbuf[0] = jnp.zeros(outbuf.shape[1:], outbuf.dtype)

        def tail(b):
            blk = pl.ds(pl.multiple_of(b * MOE_ROWS, MOE_ROWS), MOE_ROWS)
            return pltpu.make_async_copy(outbuf.at[0, pl.ds(0, MOE_ROWS)], dst_hbm.at[blk, dst_cols],
                                         outsem.at[0])

        def issue(b, carry):
            tail(b).start()
            return carry

        def drain(b, carry):
            tail(b).wait()
            return carry

        n_blocks = dst_hbm.shape[0] // MOE_ROWS
        lax.fori_loop(nu_ref[0], n_blocks, issue, 0)
        lax.fori_loop(nu_ref[0], n_blocks, drain, 0)


def _gate_up_kernel(start_ref, nblk_ref, nu_ref, w_hbm, b_ref, xs_hbm, h_hbm, wbf, gubuf, wraw, wsem,
                    xbuf, hbuf, xsem, hsem, *, layer):
    tq = hbuf.shape[2]
    cols = pl.ds(pl.multiple_of(pl.program_id(0) * tq, tq), tq)

    def prepare(w_ref):
        wbf[...] = w_ref[...].astype(BF16)

    def project(x_ref, gu_ref):
        lo, hi = _unpack_bf16_pairs(x_ref[...])
        half = lo.shape[1]
        gu_ref[...] = (jnp.dot(lo, wbf[:half, :], preferred_element_type=F32)
                       + jnp.dot(hi, wbf[half:, :], preferred_element_type=F32) + b_ref[0, 0])

    def activate(gu_ref, h_ref):
        even = (lax.broadcasted_iota(I32, (gu_ref.shape[0], 128), 1) & 1) == 0
        for c in range(tq // 128):
            a = gu_ref[:, c * 128:(c + 1) * 128]
            b = gu_ref[:, tq + c * 128:tq + (c + 1) * 128]
            gate = jnp.where(even, a, pltpu.roll(b, 1, 1))
            up = jnp.where(even, pltpu.roll(a, 127, 1), b)
            hg = jnp.minimum(gate, SWIGLU_LIMIT)
            hu = jnp.clip(up, -SWIGLU_LIMIT, SWIGLU_LIMIT)
            act = (hu + 1.0) * (hg / (1.0 + jnp.exp(-SWIGLU_ALPHA * hg)))
            h_ref[:, c * 128:(c + 1) * 128] = act.astype(h_ref.dtype)

    _stream_expert_blocks(layer, start_ref, nblk_ref, nu_ref, w_hbm, wraw, wsem, xs_hbm, h_hbm, cols,
                          xbuf, gubuf, hbuf, xsem, hsem, prepare, project, activate)


def _expert_call(kernel_fn, layer, starts, nblk, n_used, w, b, src, out_shape, tn, out_w, scratch, name):
    n_exp, d_in, n_out = w.shape[1:]
    assert d_in % (8 * MOE_WEIGHT_PIECES) == 0
    smem = pl.BlockSpec(memory_space=pltpu.SMEM)
    hbm = pl.BlockSpec(memory_space=pl.ANY)
    return pl.pallas_call(
        functools.partial(kernel_fn, layer=layer),
        grid=(n_out // tn, n_exp),
        in_specs=[smem, smem, smem, hbm,
                  pl.BlockSpec((1, 1, 1, tn), lambda j, e: (layer, e, 0, j)), hbm],
        out_specs=hbm,
        out_shape=out_shape,
        scratch_shapes=scratch + [pltpu.VMEM((2, d_in, tn), F32), pltpu.SemaphoreType.DMA((2,)),
                                  pltpu.VMEM((MOE_IN_BUFS, MOE_PAIR * MOE_ROWS, src.shape[1]), src.dtype),
                                  pltpu.VMEM((2, MOE_PAIR * MOE_ROWS, out_w), out_shape.dtype),
                                  pltpu.SemaphoreType.DMA((MOE_IN_BUFS,)), pltpu.SemaphoreType.DMA((2,))],
        compiler_params=_params(2),
        name=name,
    )(starts, nblk, n_used, w, b.reshape(b.shape[0], n_exp, 1, n_out), src)


def _gate_up(xs, w, b, layer, starts, nblk, n_used, *, tn):
    d, f2 = w.shape[2:]
    return _expert_call(_gate_up_kernel, layer, starts, nblk, n_used, w, b, xs,
                        jax.ShapeDtypeStruct((xs.shape[0], f2 // 2), BF16), tn, tn // 2,
                        [pltpu.VMEM((d, tn), BF16), pltpu.VMEM((MOE_PAIR * MOE_ROWS, tn), F32)], "moe_gate_up")


def _down_kernel(start_ref, nblk_ref, nu_ref, w_hbm, b_ref, h_hbm, y_hbm, wbf, stage, wraw, wsem,
                 hbuf, ybuf, hsem, ysem, *, group, layer):
    tn = wbf.shape[1]
    cols = pl.ds(pl.multiple_of(pl.program_id(0) * (tn // 2), tn // 2), tn // 2)

    def prepare(w_ref):
        half = group // 2
        for g0 in range(0, wbf.shape[0], group):
            for c in range(tn // 128):
                lanes = slice(c * 128, (c + 1) * 128)
                s = c % 2
                stage[s, pl.ds(0, half, stride=2), :] = w_ref[g0:g0 + half, lanes]
                stage[s, pl.ds(1, half, stride=2), :] = w_ref[g0 + half:g0 + group, lanes]
                wbf[g0:g0 + group, lanes] = stage[s].astype(BF16)

    def compute(h_ref, y_ref):
        y = jnp.dot(h_ref[...], wbf[...], preferred_element_type=F32) + b_ref[0, 0]
        y_ref[...] = _pack_bf16_pairs(y)

    _stream_expert_blocks(layer, start_ref, nblk_ref, nu_ref, w_hbm, wraw, wsem, h_hbm, y_hbm, cols,
                          hbuf, None, ybuf, hsem, ysem, prepare, compute)


def _down(h, w, b, layer, starts, nblk, n_used, *, group, tn):
    f, d = w.shape[2:]
    return _expert_call(functools.partial(_down_kernel, group=group), layer, starts, nblk, n_used, w, b, h,
                        jax.ShapeDtypeStruct((h.shape[0], d // 2), I32), tn, tn // 2,
                        [pltpu.VMEM((f, tn), BF16), pltpu.VMEM((2, group, 128), F32)], "moe_down")


def _combine_kernel(dest_ref, next_ref, gate_ref, x_ref, g_ref, b_ref, y_hbm, o_ref, ob_ref, buf, sems, *,
                    alpha, pair_tile):
    i = pl.program_id(0)
    tm = x_ref.shape[0]
    slot = lax.rem(i, 2)

    def gather(d_ref, s):
        def issue(t, carry):
            for k in range(TOP_K):
                pltpu.make_async_copy(y_hbm.at[pl.ds(d_ref[k, t], 1)], buf.at[s, k, pl.ds(t, 1)],
                                      sems.at[s]).start(priority=k % 2)
            return carry

        lax.fori_loop(0, tm, issue, 0)

    @pl.when(i == 0)
    def _():
        gather(dest_ref, 0)

    @pl.when(i + 1 < pl.num_programs(0))
    def _():
        gather(next_ref, 1 - slot)

    for k in range(TOP_K):
        pltpu.make_async_copy(y_hbm.at[pl.ds(0, tm)], buf.at[slot, k], sems.at[slot]).wait()
    gates = gate_ref[...]
    lo = hi = None
    for k in range(TOP_K):
        w = buf[slot, k]
        g = gates[:, k:k + 1]
        lo_k = g * lax.bitcast_convert_type(lax.shift_left(w, 16), F32)
        hi_k = g * lax.bitcast_convert_type(w & jnp.int32(-65536), F32)
        lo = lo_k if lo is None else lo + lo_k
        hi = hi_k if hi is None else hi + hi_k
    half = pair_tile // 2
    parts = []
    for c0 in range(0, lo.shape[1], half):
        parts += [lo[:, c0:c0 + half], hi[:, c0:c0 + half]]
    ffn = jnp.concatenate(parts, axis=1)
    out = _layer_norm(alpha * x_ref[...] + ffn, g_ref[...], b_ref[...])
    o_ref[...] = out
    ob_ref[...] = out.astype(ob_ref.dtype)


def _combine(dest, gates, x, y, ln_g, ln_b, alpha, *, pair_tile, tm=256):
    t, d = x.shape
    tm = _tile(t, tm)
    n_steps = t // tm
    row = lambda i: (i, 0)
    fixed = lambda i: (0, 0)
    return pl.pallas_call(
        functools.partial(_combine_kernel, alpha=alpha, pair_tile=pair_tile),
        grid=(n_steps,),
        in_specs=[pl.BlockSpec((TOP_K, tm), lambda i: (0, i), memory_space=pltpu.SMEM),
                  pl.BlockSpec((TOP_K, tm), lambda i: (0, jnp.minimum(i + 1, n_steps - 1)),
                               memory_space=pltpu.SMEM),
                  pl.BlockSpec((tm, TOP_K), row), pl.BlockSpec((tm, d), row),
                  pl.BlockSpec((1, d), fixed), pl.BlockSpec((1, d), fixed),
                  pl.BlockSpec(memory_space=pl.ANY)],
        out_specs=[pl.BlockSpec((tm, d), row), pl.BlockSpec((tm, d), row)],
        out_shape=[jax.ShapeDtypeStruct((t, d), F32), jax.ShapeDtypeStruct((t, d), BF16)],
        scratch_shapes=[pltpu.VMEM((2, TOP_K, tm, d // 2), I32), pltpu.SemaphoreType.DMA((2,))],
        compiler_params=_params(1),
        name="moe_combine_ln",
    )(dest, dest, gates, x, ln_g.reshape(1, d), ln_b.reshape(1, d), y)


def _moe(x, xp, idx_t, gate_t, rank_t, counts2d, layer, w_gate_up, b_gate_up, w_down, b_down, ln_g, ln_b, alpha):
    t, d = x.shape
    n_exp, f = w_down.shape[1], w_down.shape[2]
    n_blocks = t * TOP_K // MOE_ROWS + n_exp + (MOE_PAIR - 1)
    n_rows = n_blocks * MOE_ROWS

    counts = counts2d[:, 0]
    nblk = (counts + MOE_ROWS - 1) // MOE_ROWS
    pad_end = jnp.cumsum(nblk) * MOE_ROWS
    starts = (pad_end - nblk * MOE_ROWS).astype(I32)
    nu = (pad_end[-1:] // MOE_ROWS).astype(I32)
    experts = jnp.arange(n_exp, dtype=I32)
    dest = jnp.sum(jnp.where(idx_t[:, :, None] == experts, starts, 0), axis=-1) + rank_t

    xs = _dispatch(dest, counts, starts, nu, xp, n_rows)
    tn = _tile(2 * f, MOE_GATE_UP_TILE)
    h = _gate_up(xs, w_gate_up, b_gate_up, layer, starts, nblk, nu, tn=tn)
    tn_down = _tile(d, MOE_DOWN_TILE)
    y = _down(h, w_down, b_down, layer, starts, nblk, nu, group=tn // 2, tn=tn_down)
    return _combine(dest, gate_t.T, x, y, ln_g, ln_b, alpha, pair_tile=tn_down)


def kernel(x, attn_w_in, attn_q_gain, attn_k_gain, attn_w_out, ret_w_in, ret_decay_fwd, ret_decay_bwd,
           ret_w_out, ln_mix_g, ln_mix_b, router_w, router_b, expert_w_gate_up, expert_b_gate_up,
           expert_w_down, expert_b_down, ln_ffn_g, ln_ffn_b):
    bsz, seq, d = x.shape
    depth = ln_mix_g.shape[0]
    alpha = (2 * depth) ** 0.25
    nq = attn_w_out.shape[1]
    n_heads = nq // HEAD_DIM
    n_kv = (attn_w_in.shape[2] - nq) // (2 * HEAD_DIM)
    ret_heads = ret_decay_fwd.shape[1]
    dv = ret_w_out.shape[1] // ret_heads
    dk = (ret_w_in.shape[2] - 2 * ret_heads * dv) // (2 * ret_heads)

    xf = x.reshape(bsz * seq, d)
    xin = xf
    for i in range(depth):
        j = i // 2
        if i % 2 == 0:
            h = _matmul(xin, attn_w_in[j].astype(BF16))
            mix = _attention(h, attn_q_gain[j], attn_k_gain[j], bsz, seq, n_heads, n_kv)
            w_out = attn_w_out[j]
        else:
            h = _matmul_rope(xin, ret_w_in[j].astype(BF16), seq, ret_heads, dk)
            lg_f = jnp.log1p(-jnp.exp(ret_decay_fwd[j].astype(F32)))
            lg_b = jnp.log1p(-jnp.exp(ret_decay_bwd[j].astype(F32)))
            mix = _retention(h, lg_f, lg_b, bsz, seq, ret_heads, dk, dv)
            w_out = ret_w_out[j]
        x1, xp, idx_t, gate_t, rank_t, counts = _proj_route(
            mix, w_out.astype(BF16), xf, ln_mix_g[i], ln_mix_b[i], router_w[i], router_b[i], alpha)
        xf, xin = _moe(x1, xp, idx_t, gate_t, rank_t, counts, i, expert_w_gate_up, expert_b_gate_up,
                       expert_w_down, expert_b_down, ln_ffn_g[i], ln_ffn_b[i], alpha)
    return xf.reshape(bsz, seq, d)
```

```python
import functools

import jax
import jax.numpy as jnp
from jax import lax
from jax.experimental import pallas as pl
from jax.experimental.pallas import tpu as pltpu

F32, BF16, I32 = jnp.float32, jnp.bfloat16, jnp.int32

HEAD_DIM = 128
GRID_W = 64
ROPE_THETA = 10000.0
TOP_K = 4
QK_NORM_EPS = 1e-6
RET_SCAN_CHUNK = 256
RET_GN_EPS = 1e-5
SWIGLU_LIMIT = 7.0
SWIGLU_ALPHA = 1.702
LN_EPS = 1e-5
LOG2_E = 1.4426950408889634
ATTN_KV_CHUNK = 512

V7X_VMEM_LIMIT_BYTES = 56 * 1024 * 1024
MOE_ROWS = 256
MOE_GATE_UP_TILE = 2048
MOE_DOWN_TILE = 2048
MOE_PAIR = 1
MOE_LOAD_PIECES = 1
MOE_WEIGHT_PIECES = 16
MOE_IN_BUFS = 3
XFER_PRIORITY = 1

NT_DIMS = (((1,), (1,)), ((), ()))
TN_DIMS = (((0,), (0,)), ((), ()))


def _params(n_axes):
    return pltpu.CompilerParams(dimension_semantics=("arbitrary",) * n_axes,
                                vmem_limit_bytes=V7X_VMEM_LIMIT_BYTES)


def _tile(n, pref):
    t = min(n, pref)
    while n % t:
        t -= 128
    assert t > 0, (n, pref)
    return t


def _layer_norm(z, g, b):
    mu = jnp.mean(z, axis=-1, keepdims=True)
    zc = z - mu
    var = jnp.mean(zc * zc, axis=-1, keepdims=True)
    return zc * lax.rsqrt(var + LN_EPS) * g + b


def _rope_tables(seq, dim):
    rows = seq // GRID_W
    row_idx = jnp.repeat(jnp.arange(rows, dtype=F32), GRID_W)
    col_idx = jnp.tile(jnp.arange(GRID_W, dtype=F32), rows)
    quarter = dim // 4
    inv_freq = ROPE_THETA ** (-jnp.arange(quarter, dtype=F32) / quarter)
    ang_r = row_idx[:, None] * inv_freq[None, :]
    ang_c = col_idx[:, None] * inv_freq[None, :]
    ang = jnp.concatenate([ang_r, ang_r, ang_c, ang_c], axis=-1)
    cos, sin = jnp.cos(ang), jnp.sin(ang)
    first = (jnp.arange(dim) % (dim // 2)) < quarter
    return cos, jnp.where(first, -sin, 0.0), jnp.where(first, 0.0, sin)


def _rope(x, cos, sin_up, sin_dn):
    d = x.shape[-1]
    q = d // 4
    if 2 * q == 128:
        sin = sin_up + sin_dn
        parts = [pltpu.roll(x[:, c:c + 128], q, 1) * sin[:, c:c + 128] for c in range(0, d, 128)]
        return x * cos + jnp.concatenate(parts, axis=1)
    return x * cos + pltpu.roll(x, d - q, 1) * sin_up + pltpu.roll(x, q, 1) * sin_dn


def _matmul_kernel(x_ref, w_ref, o_ref):
    o_ref[...] = jnp.dot(x_ref[...].astype(BF16), w_ref[...],
                         preferred_element_type=F32).astype(o_ref.dtype)


def _matmul(x, w, *, tm=512, tn=1024):
    m, k = x.shape
    n = w.shape[1]
    tm, tn = _tile(m, tm), _tile(n, tn)
    return pl.pallas_call(
        _matmul_kernel,
        grid=(n // tn, m // tm),
        in_specs=[pl.BlockSpec((tm, k), lambda j, i: (i, 0)),
                  pl.BlockSpec((k, tn), lambda j, i: (0, j))],
        out_specs=pl.BlockSpec((tm, tn), lambda j, i: (i, j)),
        out_shape=jax.ShapeDtypeStruct((m, n), BF16),
        compiler_params=_params(2),
        name="dense_matmul",
    )(x, w)


def _matmul_rope_kernel(x_ref, w_ref, cos_ref, up_ref, dn_ref, o_ref, *, n_q_tiles, n_rope_tiles, k_scale):
    j = pl.program_id(0)
    acc = jnp.dot(x_ref[...].astype(BF16), w_ref[...], preferred_element_type=F32)

    @pl.when(j >= n_rope_tiles)
    def _():
        o_ref[...] = acc.astype(o_ref.dtype)

    @pl.when(j < n_rope_tiles)
    def _():
        scale = jnp.where(j >= n_q_tiles, k_scale, 1.0)
        dk = cos_ref.shape[1]
        cos, up, dn = cos_ref[...], up_ref[...], dn_ref[...]
        for hh in range(acc.shape[1] // dk):
            cols = slice(hh * dk, (hh + 1) * dk)
            o_ref[:, cols] = (_rope(acc[:, cols], cos, up, dn) * scale).astype(o_ref.dtype)


def _matmul_rope(x, w, seq, n_heads, dk, *, tm=512, tn=1024):
    m, k = x.shape
    n = w.shape[1]
    nqk = n_heads * dk
    tm, tn = _tile(seq, tm), _tile(nqk, tn)
    assert tn % dk == 0 and n % tn == 0
    cos, up, dn = _rope_tables(seq, dk)
    n_rope = 2 * nqk // tn
    tab_spec = pl.BlockSpec((tm, dk), lambda j, i: (jnp.where(j < n_rope, i % (seq // tm), 0), 0))
    return pl.pallas_call(
        functools.partial(_matmul_rope_kernel, n_q_tiles=nqk // tn, n_rope_tiles=2 * nqk // tn,
                          k_scale=dk ** -0.5),
        grid=(n // tn, m // tm),
        in_specs=[pl.BlockSpec((tm, k), lambda j, i: (i, 0)),
                  pl.BlockSpec((k, tn), lambda j, i: (0, j)),
                  tab_spec, tab_spec, tab_spec],
        out_specs=pl.BlockSpec((tm, tn), lambda j, i: (i, j)),
        out_shape=jax.ShapeDtypeStruct((m, n), BF16),
        compiler_params=_params(2),
        name="dense_matmul_rope",
    )(x, w, cos, up, dn)


def _attn_kernel(q_ref, k_ref, v_ref, cq_ref, uq_ref, dq_ref, ck_ref, uk_ref, dk_ref,
                 qg_ref, kg_ref, o_ref, k_scr, v_scr, *, groups, scale):
    def norm_rope(x, gain, cos, up, dn):
        xn = x * lax.rsqrt(jnp.mean(x * x, axis=-1, keepdims=True) + QK_NORM_EPS) * gain
        return _rope(xn, cos, up, dn)

    @pl.when(pl.program_id(2) == 0)
    def _():
        k = k_ref[...].astype(F32)
        k_scr[...] = norm_rope(k, kg_ref[...], ck_ref[...], uk_ref[...], dk_ref[...]).astype(BF16)
        v_scr[:, :HEAD_DIM] = v_ref[...]
        v_scr[:, HEAD_DIM:] = jnp.ones((v_scr.shape[0], HEAD_DIM), BF16)

    cos, up, dn = cq_ref[...], uq_ref[...], dq_ref[...]
    for g in range(groups):
        cols = slice(g * HEAD_DIM, (g + 1) * HEAD_DIM)
        q = q_ref[:, cols].astype(F32)
        qr = (norm_rope(q, qg_ref[...], cos, up, dn) * (scale * LOG2_E)).astype(BF16)
        kc = min(ATTN_KV_CHUNK, k_scr.shape[0])
        m = o = None
        for c0 in range(0, k_scr.shape[0], kc):
            s = lax.dot_general(qr, k_scr[c0:c0 + kc, :], NT_DIMS, preferred_element_type=F32)
            m_c = jnp.max(s, axis=-1, keepdims=True)
            m_new = m_c if m is None else jnp.maximum(m, m_c)
            pv = jnp.dot(jnp.exp2(s - m_new).astype(BF16), v_scr[c0:c0 + kc, :],
                         preferred_element_type=F32)
            o = pv if o is None else o * jnp.exp2(m - m_new) + pv
            m = m_new
        o_ref[:, cols] = (o[:, :HEAD_DIM] / o[:, HEAD_DIM:HEAD_DIM + 1]).astype(o_ref.dtype)


def _attention(h, q_gain, k_gain, bsz, seq, n_heads, n_kv, *, tq=256):
    t = h.shape[0]
    groups = n_heads // n_kv
    tq = _tile(seq, tq)
    nq = seq // tq
    cos, up, dn = _rope_tables(seq, HEAD_DIM)
    gw = groups * HEAD_DIM
    q_spec = pl.BlockSpec((tq, gw), lambda b, kv, i: (b * nq + i, kv))
    k_spec = pl.BlockSpec((seq, HEAD_DIM), lambda b, kv, i: (b, n_heads + kv))
    v_spec = pl.BlockSpec((seq, HEAD_DIM), lambda b, kv, i: (b, n_heads + n_kv + kv))
    tq_spec = pl.BlockSpec((tq, HEAD_DIM), lambda b, kv, i: (i, 0))
    tk_spec = pl.BlockSpec((seq, HEAD_DIM), lambda b, kv, i: (0, 0))
    gain_spec = pl.BlockSpec((1, HEAD_DIM), lambda b, kv, i: (0, 0))
    return pl.pallas_call(
        functools.partial(_attn_kernel, groups=groups, scale=HEAD_DIM ** -0.5),
        grid=(bsz, n_kv, nq),
        in_specs=[q_spec, k_spec, v_spec, tq_spec, tq_spec, tq_spec, tk_spec, tk_spec, tk_spec,
                  gain_spec, gain_spec],
        out_specs=pl.BlockSpec((tq, gw), lambda b, kv, i: (b * nq + i, kv)),
        out_shape=jax.ShapeDtypeStruct((t, n_heads * HEAD_DIM), BF16),
        scratch_shapes=[pltpu.VMEM((seq, HEAD_DIM), BF16), pltpu.VMEM((seq, 2 * HEAD_DIM), BF16)],
        compiler_params=_params(3),
        name="gqa_attention",
    )(h, h, h, cos, up, dn, cos, up, dn, q_gain.reshape(1, HEAD_DIM), k_gain.reshape(1, HEAD_DIM))


def _ret_kernel(lg_ref, q_ref, k_ref, v_ref, g_ref, o_ref, state, o_acc, *, n_sub):
    d = pl.program_id(2)
    sc = pl.program_id(3)
    n_sc = pl.num_programs(3)
    c = RET_SCAN_CHUNK
    span = n_sub * c

    @pl.when(sc == 0)
    def _():
        state[...] = jnp.zeros_like(state)

    lg = lg_ref[0, 0][:1, :1]
    sgn = 1 - 2 * d
    dist = (lax.broadcasted_iota(I32, (c, c), 0) - lax.broadcasted_iota(I32, (c, c), 1)) * sgn
    intra = jnp.where(dist >= 0, jnp.exp(lg * jnp.maximum(dist, 0).astype(F32)), 0.0)
    pos = lax.broadcasted_iota(I32, (c, 1), 0)
    q_dec = jnp.exp(lg * jnp.where(d == 0, pos + 1, c - pos).astype(F32))
    k_dec = jnp.exp(lg * jnp.where(d == 0, c - 1 - pos, pos).astype(F32))
    chunk_dec = jnp.exp(lg * float(c))
    sci = sc + d * (n_sc - 1 - 2 * sc)

    for j in range(n_sub):
        cj = j + d * (n_sub - 1 - 2 * j)
        r0 = pl.multiple_of(cj * c, c)
        rows = pl.ds(r0, c)
        q = q_ref[rows, :]
        k = k_ref[rows, :]
        v = v_ref[rows, :]
        s = lax.dot_general(q, k, NT_DIMS, preferred_element_type=F32) * intra
        st = state[...]
        o = (jnp.dot(s.astype(BF16), v, preferred_element_type=F32)
             + q_dec * jnp.dot(q, st.astype(BF16), preferred_element_type=F32))
        kd = (k.astype(F32) * k_dec).astype(BF16)
        state[...] = st * chunk_dec + lax.dot_general(kd, v, TN_DIMS, preferred_element_type=F32)
        acc_rows = pl.ds(pl.multiple_of(sci * span + r0, c), c)

        @pl.when(d == 0)
        def _():
            o_acc[acc_rows, :] = o

        @pl.when(d == 1)
        def _():
            ot = o_acc[acc_rows, :] + o
            mu = jnp.mean(ot, axis=-1, keepdims=True)
            oc = ot - mu
            var = jnp.mean(oc * oc, axis=-1, keepdims=True)
            on = oc * lax.rsqrt(var + RET_GN_EPS)
            gate = g_ref[rows, :].astype(F32)
            o_ref[rows, :] = (gate / (1.0 + jnp.exp(-gate)) * on).astype(o_ref.dtype)


def _retention(h, lg_fwd, lg_bwd, bsz, seq, n_heads, dk, dv, *, span=1024):
    t = h.shape[0]
    span = _tile(seq, span)
    n_sub = span // RET_SCAN_CHUNK
    n_sc = seq // span
    lg = jnp.broadcast_to(jnp.stack([lg_fwd, lg_bwd])[:, :, None, None], (2, n_heads, 8, 128)).astype(F32)
    k_off = n_heads
    v_off = 2 * n_heads * dk // dv
    g_off = v_off + n_heads

    def blk(b, d, s):
        return b * n_sc + s + d * (n_sc - 1 - 2 * s)

    def held(b, d, s):
        return b * n_sc + jnp.where(d == 0, n_sc - 1, n_sc - 1 - s)

    return pl.pallas_call(
        functools.partial(_ret_kernel, n_sub=n_sub),
        grid=(bsz, n_heads, 2, n_sc),
        in_specs=[pl.BlockSpec((1, 1, 8, 128), lambda b, hh, d, s: (d, hh, 0, 0)),
                  pl.BlockSpec((span, dk), lambda b, hh, d, s: (blk(b, d, s), hh)),
                  pl.BlockSpec((span, dk), lambda b, hh, d, s: (blk(b, d, s), k_off + hh)),
                  pl.BlockSpec((span, dv), lambda b, hh, d, s: (blk(b, d, s), v_off + hh)),
                  pl.BlockSpec((span, dv), lambda b, hh, d, s: (held(b, d, s), g_off + hh))],
        out_specs=pl.BlockSpec((span, dv), lambda b, hh, d, s: (held(b, d, s), hh)),
        out_shape=jax.ShapeDtypeStruct((t, n_heads * dv), BF16),
        scratch_shapes=[pltpu.VMEM((dk, dv), F32), pltpu.VMEM((seq, dv), F32)],
        compiler_params=_params(4),
        name="retention",
    )(lg, h, h, h, h)


def _pack_bf16_pairs(y):
    half = y.shape[1] // 2
    bits = lax.bitcast_convert_type(y.astype(BF16).astype(F32), I32)
    return (bits[:, half:] & jnp.int32(-65536)) | lax.shift_right_logical(bits[:, :half], 16)


def _unpack_bf16_pairs(w):
    lo = lax.bitcast_convert_type(lax.shift_left(w, 16), F32).astype(BF16)
    hi = lax.bitcast_convert_type(w & jnp.int32(-65536), F32).astype(BF16)
    return lo, hi


def _proj_route_kernel(a_ref, w_ref, res_ref, g_ref, b_ref, rwh_ref, rwl_ref, rb_ref,
                       x_ref, xp_ref, idx_ref, gate_ref, rank_ref, cnt_ref, carry, *, alpha):
    @pl.when(pl.program_id(0) == 0)
    def _():
        carry[...] = jnp.zeros_like(carry)

    acc = jnp.dot(a_ref[...], w_ref[...], preferred_element_type=F32)
    y = _layer_norm(alpha * res_ref[...] + acc, g_ref[...], b_ref[...])
    x_ref[...] = y
    xp_ref[...] = _pack_bf16_pairs(y)

    yh = y.astype(BF16)
    yl = (y - yh.astype(F32)).astype(BF16)
    logits = (lax.dot_general(rwh_ref[...], yh, NT_DIMS, preferred_element_type=F32)
              + lax.dot_general(rwl_ref[...], yh, NT_DIMS, preferred_element_type=F32)
              + lax.dot_general(rwh_ref[...], yl, NT_DIMS, preferred_element_type=F32)
              + rb_ref[...])
    n_exp, tm = logits.shape
    eidx = lax.broadcasted_iota(I32, logits.shape, 0).astype(F32)
    vals, idxs = [], []
    rest = logits
    for _ in range(TOP_K):
        m = jnp.max(rest, axis=0, keepdims=True)
        am = jnp.min(jnp.where(rest == m, eidx, float(n_exp)), axis=0, keepdims=True)
        vals.append(m)
        idxs.append(am)
        rest = jnp.where(eidx == am, -jnp.inf, rest)
    exps = [jnp.exp(v - vals[0]) for v in vals]
    den = exps[0] + exps[1] + exps[2] + exps[3]
    hots = [eidx == am for am in idxs]
    multi = sum(h.astype(F32) for h in hots)

    before = (lax.broadcasted_iota(I32, (tm, tm), 0) < lax.broadcasted_iota(I32, (tm, tm), 1))
    prefix = jnp.dot(multi.astype(BF16), before.astype(BF16), preferred_element_type=F32) + carry[:, :1]
    for k in range(TOP_K):
        idx_ref[k:k + 1, :] = idxs[k].astype(I32)
        gate_ref[k:k + 1, :] = exps[k] / den
        rank_ref[k:k + 1, :] = jnp.sum(jnp.where(hots[k], prefix, 0.0), axis=0, keepdims=True).astype(I32)
    carry[...] = carry[...] + jnp.sum(multi, axis=1, keepdims=True)
    cnt_ref[...] = carry[...].astype(I32)


def _proj_route(a, w, resid, ln_g, ln_b, router_w, router_b, alpha, *, tm=256):
    t, k = a.shape
    d = w.shape[1]
    n_exp = router_w.shape[1]
    tm = _tile(t, tm)
    rwt = router_w.T
    rwh = rwt.astype(BF16)
    rwl = (rwt - rwh.astype(F32)).astype(BF16)
    row = lambda i: (i, 0)
    fixed = lambda i: (0, 0)
    col = lambda i: (0, i)
    outs = pl.pallas_call(
        functools.partial(_proj_route_kernel, alpha=alpha),
        grid=(t // tm,),
        in_specs=[pl.BlockSpec((tm, k), row),
                  pl.BlockSpec((k, d), fixed, pipeline_mode=pl.Buffered(1)),
                  pl.BlockSpec((tm, d), row),
                  pl.BlockSpec((1, d), fixed), pl.BlockSpec((1, d), fixed),
                  pl.BlockSpec((n_exp, d), fixed), pl.BlockSpec((n_exp, d), fixed),
                  pl.BlockSpec((n_exp, 1), fixed)],
        out_specs=[pl.BlockSpec((tm, d), row), pl.BlockSpec((tm, d // 2), row),
                   pl.BlockSpec((TOP_K, tm), col), pl.BlockSpec((TOP_K, tm), col),
                   pl.BlockSpec((TOP_K, tm), col), pl.BlockSpec((n_exp, 128), fixed)],
        out_shape=[jax.ShapeDtypeStruct((t, d), F32), jax.ShapeDtypeStruct((t, d // 2), I32),
                   jax.ShapeDtypeStruct((TOP_K, t), I32), jax.ShapeDtypeStruct((TOP_K, t), F32),
                   jax.ShapeDtypeStruct((TOP_K, t), I32), jax.ShapeDtypeStruct((n_exp, 128), I32)],
        scratch_shapes=[pltpu.VMEM((n_exp, 128), F32)],
        compiler_params=_params(1),
        name="proj_ln_route",
    )(a, w, resid, ln_g.reshape(1, d), ln_b.reshape(1, d), rwh, rwl, router_b.reshape(n_exp, 1))
    return outs


def _dispatch_kernel(dest_ref, cnt_ref, start_ref, nu_ref, xp_ref, xs_hbm, zblk, sem, *, n_tok_steps, n_exp):
    i = pl.program_id(0)
    tm = xp_ref.shape[0]
    n_blocks = xs_hbm.shape[0] // MOE_ROWS

    def row_copy(src, dst_row):
        return pltpu.make_async_copy(src, xs_hbm.at[pl.ds(dst_row, 1)], sem)

    def block_copy(b):
        return pltpu.make_async_copy(zblk, xs_hbm.at[pl.ds(pl.multiple_of(b * MOE_ROWS, MOE_ROWS), MOE_ROWS)], sem)

    @pl.when(i < n_tok_steps)
    def _():
        def issue(t, carry):
            for k in range(TOP_K):
                row_copy(xp_ref.at[pl.ds(t, 1)], dest_ref[k, t]).start(priority=k % 2)
            return carry

        lax.fori_loop(0, tm, issue, 0)
        for k in range(TOP_K):
            pltpu.make_async_copy(xp_ref, xs_hbm.at[pl.ds(0, tm)], sem).wait()

    @pl.when(i == n_tok_steps)
    def _():
        zblk[...] = jnp.zeros_like(zblk)
        zrow = zblk.at[pl.ds(0, 1)]

        def per_expert(e, carry):
            cnt = cnt_ref[e]
            n_pad = lax.rem(MOE_ROWS - lax.rem(cnt, MOE_ROWS), MOE_ROWS)
            first = start_ref[e] + cnt

            def issue(r, c2):
                row_copy(zrow, first + r).start()
                return c2

            def drain(r, c2):
                row_copy(zrow, 0).wait()
                return c2

            lax.fori_loop(0, n_pad, issue, 0)
            lax.fori_loop(0, n_pad, drain, 0)
            return carry

        lax.fori_loop(0, n_exp, per_expert, 0)

        def tail_issue(b, carry):
            block_copy(b).start()
            return carry

        def tail_drain(b, carry):
            block_copy(0).wait()
            return carry

        lax.fori_loop(nu_ref[0], n_blocks, tail_issue, 0)
        lax.fori_loop(nu_ref[0], n_blocks, tail_drain, 0)


def _dispatch(dest, counts, starts, n_used, xp, n_rows, *, tm=256):
    t, w = xp.shape
    tm = _tile(t, tm)
    n_steps = t // tm
    n_exp = counts.shape[0]
    smem = functools.partial(pl.BlockSpec, memory_space=pltpu.SMEM)
    return pl.pallas_call(
        functools.partial(_dispatch_kernel, n_tok_steps=n_steps, n_exp=n_exp),
        grid=(n_steps + 1,),
        in_specs=[smem((TOP_K, tm), lambda i: (0, jnp.minimum(i, n_steps - 1))),
                  smem(), smem(), smem(),
                  pl.BlockSpec((tm, w), lambda i: (jnp.minimum(i, n_steps - 1), 0))],
        out_specs=pl.BlockSpec(memory_space=pl.ANY),
        out_shape=jax.ShapeDtypeStruct((n_rows, w), I32),
        scratch_shapes=[pltpu.VMEM((MOE_ROWS, w), I32), pltpu.SemaphoreType.DMA(())],
        compiler_params=_params(1),
        name="moe_dispatch",
    )(dest, counts, starts, n_used, xp)


def _stream_expert_blocks(layer, start_ref, nblk_ref, nu_ref, w_hbm, wraw, wsem, src_hbm, dst_hbm, dst_cols,
                          inbuf, midbuf, outbuf, insem, outsem, prepare, stage_a, stage_b=None):
    j, e = pl.program_id(0), pl.program_id(1)
    n_j, n_e = pl.num_programs(0), pl.num_programs(1)
    nb = nblk_ref[e]
    base = start_ref[e]
    e_next = jnp.where(e + 1 < n_e, e + 1, 0)
    j_next = jnp.where(e + 1 < n_e, j, j + 1)
    is_last_step = (j == n_j - 1) & (e == n_e - 1)
    wslot = lax.rem(j * n_e + e, 2)
    d_in, tn = wraw.shape[1:]
    piece = d_in // MOE_WEIGHT_PIECES

    def fetch_weights(jj, ee, slot):
        cols = pl.ds(pl.multiple_of(jj * tn, tn), tn)
        for p in range(MOE_WEIGHT_PIECES):
            rows = pl.ds(p * piece, piece)
            pltpu.make_async_copy(w_hbm.at[layer, ee, rows, cols], wraw.at[slot, rows], wsem.at[slot]).start()

    @pl.when((j == 0) & (e == 0) & (nb > 0))
    def _():
        fetch_weights(j, e, wslot)

    @pl.when(jnp.logical_not(is_last_step) & (nblk_ref[e_next] > 0))
    def _():
        fetch_weights(j_next, e_next, 1 - wslot)

    step_rows = MOE_PAIR * MOE_ROWS
    n_it = lax.div(nb + MOE_PAIR - 1, MOE_PAIR)

    class _Load:
        def __init__(self, first_row, r, slot):
            self.row0, self.slot = first_row + r * step_rows, slot

        def start(self, priority):
            piece = step_rows // MOE_LOAD_PIECES
            for p in range(MOE_LOAD_PIECES):
                rows = pl.ds(pl.multiple_of(self.row0 + p * piece, piece), piece)
                pltpu.make_async_copy(src_hbm.at[rows], inbuf.at[self.slot, pl.ds(p * piece, piece)],
                                      insem.at[self.slot]).start(priority=(priority + p) % 2)

        def wait(self):
            rows = pl.ds(pl.multiple_of(self.row0, MOE_ROWS), step_rows)
            pltpu.make_async_copy(src_hbm.at[rows], inbuf.at[self.slot], insem.at[self.slot]).wait()

    load_of = _Load

    def load(r):
        return load_of(base, r, lax.rem(r, MOE_IN_BUFS))

    def store_half(r, hh):
        slot = lax.rem(r, 2)
        rows = pl.ds(pl.multiple_of(base + r * step_rows + hh * MOE_ROWS, MOE_ROWS), MOE_ROWS)
        return pltpu.make_async_copy(outbuf.at[slot, pl.ds(hh * MOE_ROWS, MOE_ROWS)],
                                     dst_hbm.at[rows, dst_cols], outsem.at[slot])

    class _Stores:
        def __init__(self, r):
            self.r = r

        def _each(self, fn):
            fn(store_half(self.r, 0))
            for hh in range(1, MOE_PAIR):
                @pl.when(MOE_PAIR * self.r + hh < nb)
                def _():
                    fn(store_half(self.r, hh))

        def start(self, priority):
            self._each(lambda c: c.start(priority=priority))

        def wait(self):
            self._each(lambda c: c.wait())

    store = _Stores

    def take(r):
        @pl.when(r + MOE_IN_BUFS - 1 < n_it)
        def _():
            load(r + MOE_IN_BUFS - 1).start(priority=XFER_PRIORITY)

        load(r).wait()

    def free_out(r):
        @pl.when(r >= 2)
        def _():
            store(r - 2).wait()

    @pl.when(nb > 0)
    def _():
        @pl.when((j == 0) & (e == 0))
        def _():
            load(0).start(priority=XFER_PRIORITY)

        for r in range(1, MOE_IN_BUFS - 1):
            @pl.when(r < n_it)
            def _():
                load(r).start(priority=XFER_PRIORITY)

        pltpu.make_async_copy(w_hbm.at[layer, 0, :, pl.ds(0, tn)], wraw.at[wslot], wsem.at[wslot]).wait()
        prepare(wraw.at[wslot])

        if stage_b is None:
            def body(r, carry):
                take(r)
                free_out(r)
                stage_a(inbuf.at[lax.rem(r, MOE_IN_BUFS)], outbuf.at[lax.rem(r, 2)])
                store(r).start(priority=XFER_PRIORITY)
                return carry

            lax.fori_loop(0, n_it, body, 0)
        else:
            take(0)
            stage_a(inbuf.at[0], midbuf)

            def body(r, carry):
                take(r)
                free_out(r - 1)
                stage_b(midbuf, outbuf.at[lax.rem(r - 1, 2)])
                stage_a(inbuf.at[lax.rem(r, MOE_IN_BUFS)], midbuf)
                store(r - 1).start(priority=XFER_PRIORITY)
                return carry

            lax.fori_loop(1, n_it, body, 0)
            free_out(n_it - 1)
            stage_b(midbuf, outbuf.at[lax.rem(n_it - 1, 2)])
            store(n_it - 1).start(priority=XFER_PRIORITY)

        @pl.when(n_it >= 2)
        def _():
            store(n_it - 2).wait()

        store(n_it - 1).wait()

    @pl.when(jnp.logical_not(is_last_step) & (nblk_ref[e_next] > 0))
    def _():
        load_of(start_ref[e_next], 0, 0).start(priority=XFER_PRIORITY)

    @pl.when(e == n_e - 1)
    def _():
        outbuf[0] = jnp.zeros(outbuf.shape[1:], outbuf.dtype)

        def tail(b):
            blk = pl.ds(pl.multiple_of(b * MOE_ROWS, MOE_ROWS), MOE_ROWS)
            return pltpu.make_async_copy(outbuf.at[0, pl.ds(0, MOE_ROWS)], dst_hbm.at[blk, dst_cols],
                                         outsem.at[0])

        def issue(b, carry):
            tail(b).start()
            return carry

        def drain(b, carry):
            tail(b).wait()
            return carry

        n_blocks = dst_hbm.shape[0] // MOE_ROWS
        lax.fori_loop(nu_ref[0], n_blocks, issue, 0)
        lax.fori_loop(nu_ref[0], n_blocks, drain, 0)


def _gate_up_kernel(start_ref, nblk_ref, nu_ref, w_hbm, b_ref, xs_hbm, h_hbm, wbf, gubuf, wraw, wsem,
                    xbuf, hbuf, xsem, hsem, *, layer):
    tq = hbuf.shape[2]
    cols = pl.ds(pl.multiple_of(pl.program_id(0) * tq, tq), tq)

    def prepare(w_ref):
        wbf[...] = w_ref[...].astype(BF16)

    def project(x_ref, gu_ref):
        lo, hi = _unpack_bf16_pairs(x_ref[...])
        half = lo.shape[1]
        gu_ref[...] = (jnp.dot(lo, wbf[:half, :], preferred_element_type=F32)
                       + jnp.dot(hi, wbf[half:, :], preferred_element_type=F32) + b_ref[0, 0])

    def activate(gu_ref, h_ref):
        even = (lax.broadcasted_iota(I32, (gu_ref.shape[0], 128), 1) & 1) == 0
        for c in range(tq // 128):
            a = gu_ref[:, c * 128:(c + 1) * 128]
            b = gu_ref[:, tq + c * 128:tq + (c + 1) * 128]
            gate = jnp.where(even, a, pltpu.roll(b, 1, 1))
            up = jnp.where(even, pltpu.roll(a, 127, 1), b)
            hg = jnp.minimum(gate, SWIGLU_LIMIT)
            hu = jnp.clip(up, -SWIGLU_LIMIT, SWIGLU_LIMIT)
            act = (hu + 1.0) * (hg / (1.0 + jnp.exp(-SWIGLU_ALPHA * hg)))
            h_ref[:, c * 128:(c + 1) * 128] = act.astype(h_ref.dtype)

    _stream_expert_blocks(layer, start_ref, nblk_ref, nu_ref, w_hbm, wraw, wsem, xs_hbm, h_hbm, cols,
                          xbuf, gubuf, hbuf, xsem, hsem, prepare, project, activate)


def _expert_call(kernel_fn, layer, starts, nblk, n_used, w, b, src, out_shape, tn, out_w, scratch, name):
    n_exp, d_in, n_out = w.shape[1:]
    assert d_in % (8 * MOE_WEIGHT_PIECES) == 0
    smem = pl.BlockSpec(memory_space=pltpu.SMEM)
    hbm = pl.BlockSpec(memory_space=pl.ANY)
    return pl.pallas_call(
        functools.partial(kernel_fn, layer=layer),
        grid=(n_out // tn, n_exp),
        in_specs=[smem, smem, smem, hbm,
                  pl.BlockSpec((1, 1, 1, tn), lambda j, e: (layer, e, 0, j)), hbm],
        out_specs=hbm,
        out_shape=out_shape,
        scratch_shapes=scratch + [pltpu.VMEM((2, d_in, tn), F32), pltpu.SemaphoreType.DMA((2,)),
                                  pltpu.VMEM((MOE_IN_BUFS, MOE_PAIR * MOE_ROWS, src.shape[1]), src.dtype),
                                  pltpu.VMEM((2, MOE_PAIR * MOE_ROWS, out_w), out_shape.dtype),
                                  pltpu.SemaphoreType.DMA((MOE_IN_BUFS,)), pltpu.SemaphoreType.DMA((2,))],
        compiler_params=_params(2),
        name=name,
    )(starts, nblk, n_used, w, b.reshape(b.shape[0], n_exp, 1, n_out), src)


def _gate_up(xs, w, b, layer, starts, nblk, n_used, *, tn):
    d, f2 = w.shape[2:]
    return _expert_call(_gate_up_kernel, layer, starts, nblk, n_used, w, b, xs,
                        jax.ShapeDtypeStruct((xs.shape[0], f2 // 2), BF16), tn, tn // 2,
                        [pltpu.VMEM((d, tn), BF16), pltpu.VMEM((MOE_PAIR * MOE_ROWS, tn), F32)], "moe_gate_up")


def _down_kernel(start_ref, nblk_ref, nu_ref, w_hbm, b_ref, h_hbm, y_hbm, wbf, stage, wraw, wsem,
                 hbuf, ybuf, hsem, ysem, *, group, layer):
    tn = wbf.shape[1]
    cols = pl.ds(pl.multiple_of(pl.program_id(0) * (tn // 2), tn // 2), tn // 2)

    def prepare(w_ref):
        half = group // 2
        for g0 in range(0, wbf.shape[0], group):
            for c in range(tn // 128):
                lanes = slice(c * 128, (c + 1) * 128)
                s = c % 2
                stage[s, pl.ds(0, half, stride=2), :] = w_ref[g0:g0 + half, lanes]
                stage[s, pl.ds(1, half, stride=2), :] = w_ref[g0 + half:g0 + group, lanes]
                wbf[g0:g0 + group, lanes] = stage[s].astype(BF16)

    def compute(h_ref, y_ref):
        y = jnp.dot(h_ref[...], wbf[...], preferred_element_type=F32) + b_ref[0, 0]
        y_ref[...] = _pack_bf16_pairs(y)

    _stream_expert_blocks(layer, start_ref, nblk_ref, nu_ref, w_hbm, wraw, wsem, h_hbm, y_hbm, cols,
                          hbuf, None, ybuf, hsem, ysem, prepare, compute)


def _down(h, w, b, layer, starts, nblk, n_used, *, group, tn):
    f, d = w.shape[2:]
    return _expert_call(functools.partial(_down_kernel, group=group), layer, starts, nblk, n_used, w, b, h,
                        jax.ShapeDtypeStruct((h.shape[0], d // 2), I32), tn, tn // 2,
                        [pltpu.VMEM((f, tn), BF16), pltpu.VMEM((2, group, 128), F32)], "moe_down")


def _combine_kernel(dest_ref, next_ref, gate_ref, x_ref, g_ref, b_ref, y_hbm, o_ref, ob_ref, buf, sems, *,
                    alpha, pair_tile):
    i = pl.program_id(0)
    tm = x_ref.shape[0]
    slot = lax.rem(i, 2)

    def gather(d_ref, s):
        def issue(t, carry):
            for k in range(TOP_K):
                pltpu.make_async_copy(y_hbm.at[pl.ds(d_ref[k, t], 1)], buf.at[s, k, pl.ds(t, 1)],
                                      sems.at[s]).start(priority=k % 2)
            return carry

        lax.fori_loop(0, tm, issue, 0)

    @pl.when(i == 0)
    def _():
        gather(dest_ref, 0)

    @pl.when(i + 1 < pl.num_programs(0))
    def _():
        gather(next_ref, 1 - slot)

    for k in range(TOP_K):
        pltpu.make_async_copy(y_hbm.at[pl.ds(0, tm)], buf.at[slot, k], sems.at[slot]).wait()
    gates = gate_ref[...]
    lo = hi = None
    for k in range(TOP_K):
        w = buf[slot, k]
        g = gates[:, k:k + 1]
        lo_k = g * lax.bitcast_convert_type(lax.shift_left(w, 16), F32)
        hi_k = g * lax.bitcast_convert_type(w & jnp.int32(-65536), F32)
        lo = lo_k if lo is None else lo + lo_k
        hi = hi_k if hi is None else hi + hi_k
    half = pair_tile // 2
    parts = []
    for c0 in range(0, lo.shape[1], half):
        parts += [lo[:, c0:c0 + half], hi[:, c0:c0 + half]]
    ffn = jnp.concatenate(parts, axis=1)
    out = _layer_norm(alpha * x_ref[...] + ffn, g_ref[...], b_ref[...])
    o_ref[...] = out
    ob_ref[...] = out.astype(ob_ref.dtype)


def _combine(dest, gates, x, y, ln_g, ln_b, alpha, *, pair_tile, tm=256):
    t, d = x.shape
    tm = _tile(t, tm)
    n_steps = t // tm
    row = lambda i: (i, 0)
    fixed = lambda i: (0, 0)
    return pl.pallas_call(
        functools.partial(_combine_kernel, alpha=alpha, pair_tile=pair_tile),
        grid=(n_steps,),
        in_specs=[pl.BlockSpec((TOP_K, tm), lambda i: (0, i), memory_space=pltpu.SMEM),
                  pl.BlockSpec((TOP_K, tm), lambda i: (0, jnp.minimum(i + 1, n_steps - 1)),
                               memory_space=pltpu.SMEM),
                  pl.BlockSpec((tm, TOP_K), row), pl.BlockSpec((tm, d), row),
                  pl.BlockSpec((1, d), fixed), pl.BlockSpec((1, d), fixed),
                  pl.BlockSpec(memory_space=pl.ANY)],
        out_specs=[pl.BlockSpec((tm, d), row), pl.BlockSpec((tm, d), row)],
        out_shape=[jax.ShapeDtypeStruct((t, d), F32), jax.ShapeDtypeStruct((t, d), BF16)],
        scratch_shapes=[pltpu.VMEM((2, TOP_K, tm, d // 2), I32), pltpu.SemaphoreType.DMA((2,))],
        compiler_params=_params(1),
        name="moe_combine_ln",
    )(dest, dest, gates, x, ln_g.reshape(1, d), ln_b.reshape(1, d), y)


def _moe(x, xp, idx_t, gate_t, rank_t, counts2d, layer, w_gate_up, b_gate_up, w_down, b_down, ln_g, ln_b, alpha):
    t, d = x.shape
    n_exp, f = w_down.shape[1], w_down.shape[2]
    n_blocks = t * TOP_K // MOE_ROWS + n_exp + (MOE_PAIR - 1)
    n_rows = n_blocks * MOE_ROWS

    counts = counts2d[:, 0]
    nblk = (counts + MOE_ROWS - 1) // MOE_ROWS
    pad_end = jnp.cumsum(nblk) * MOE_ROWS
    starts = (pad_end - nblk * MOE_ROWS).astype(I32)
    nu = (pad_end[-1:] // MOE_ROWS).astype(I32)
    experts = jnp.arange(n_exp, dtype=I32)
    dest = jnp.sum(jnp.where(idx_t[:, :, None] == experts, starts, 0), axis=-1) + rank_t

    xs = _dispatch(dest, counts, starts, nu, xp, n_rows)
    tn = _tile(2 * f, MOE_GATE_UP_TILE)
    h = _gate_up(xs, w_gate_up, b_gate_up, layer, starts, nblk, nu, tn=tn)
    tn_down = _tile(d, MOE_DOWN_TILE)
    y = _down(h, w_down, b_down, layer, starts, nblk, nu, group=tn // 2, tn=tn_down)
    return _combine(dest, gate_t.T, x, y, ln_g, ln_b, alpha, pair_tile=tn_down)


def kernel(x, attn_w_in, attn_q_gain, attn_k_gain, attn_w_out, ret_w_in, ret_decay_fwd, ret_decay_bwd,
           ret_w_out, ln_mix_g, ln_mix_b, router_w, router_b, expert_w_gate_up, expert_b_gate_up,
           expert_w_down, expert_b_down, ln_ffn_g, ln_ffn_b):
    bsz, seq, d = x.shape
    depth = ln_mix_g.shape[0]
    alpha = (2 * depth) ** 0.25
    nq = attn_w_out.shape[1]
    n_heads = nq // HEAD_DIM
    n_kv = (attn_w_in.shape[2] - nq) // (2 * HEAD_DIM)
    ret_heads = ret_decay_fwd.shape[1]
    dv = ret_w_out.shape[1] // ret_heads
    dk = (ret_w_in.shape[2] - 2 * ret_heads * dv) // (2 * ret_heads)

    xf = x.reshape(bsz * seq, d)
    xin = xf
    for i in range(depth):
        j = i // 2
        if i % 2 == 0:
            h = _matmul(xin, attn_w_in[j].astype(BF16))
            mix = _attention(h, attn_q_gain[j], attn_k_gain[j], bsz, seq, n_heads, n_kv)
            w_out = attn_w_out[j]
        else:
            h = _matmul_rope(xin, ret_w_in[j].astype(BF16), seq, ret_heads, dk)
            lg_f = jnp.log1p(-jnp.exp(ret_decay_fwd[j].astype(F32)))
            lg_b = jnp.log1p(-jnp.exp(ret_decay_bwd[j].astype(F32)))
            mix = _retention(h, lg_f, lg_b, bsz, seq, ret_heads, dk, dv)
            w_out = ret_w_out[j]
        x1, xp, idx_t, gate_t, rank_t, counts = _proj_route(
            mix, w_out.astype(BF16), xf, ln_mix_g[i], ln_mix_b[i], router_w[i], router_b[i], alpha)
        xf, xin = _moe(x1, xp, idx_t, gate_t, rank_t, counts, i, expert_w_gate_up, expert_b_gate_up,
                       expert_w_down, expert_b_down, ln_ffn_g[i], ln_ffn_b[i], alpha)
    return xf.reshape(bsz, seq, d)
```
